```python
import jax
import jax.numpy as jnp
from jax import lax
import numpy as np

D_MODEL = 2048
BATCH = 4
SEQ = 4096
DEPTH = 4

N_A_LAYERS = DEPTH // 2
N_B_LAYERS = DEPTH - N_A_LAYERS
NORM_EPS = 1e-6

RWKV_HEAD_SIZE = 64
RWKV_HEADS = D_MODEL // RWKV_HEAD_SIZE
DECAY_LORA = 96
AAA_LORA = 96
MV_LORA = 64
GN_EPS = 64e-5

NSA_HEADS = 16
NSA_GROUPS = 4
NSA_REP = NSA_HEADS // NSA_GROUPS
NSA_HEAD_DIM = D_MODEL // NSA_HEADS
NSA_WIDTH = NSA_HEADS * NSA_HEAD_DIM
KV_WIDTH = NSA_GROUPS * NSA_HEAD_DIM
N_BRANCH = 3
CMP_BLOCK = 32
CMP_STRIDE = 16
CMP_HIDDEN = 256
SEL_BLOCK = 64
SEL_TOP_N = 16
WINDOW = 512
Q_BLOCK = 64
ROPE_DIM = NSA_HEAD_DIM // 4
ROPE_THETA = 500000.0
NEG_INF = -1e30
FORCE_BONUS = 1e4

kernel_name = 'hybrid_rwkv7_nsa_yoco'


def rmsnorm(x, g):
    xf = x.astype(jnp.float32)
    y = xf * lax.rsqrt(jnp.mean(xf * xf, axis=-1, keepdims=True) + NORM_EPS)
    return (y * g.astype(jnp.float32)).astype(x.dtype)


def rope_partial(x, pos):
    half = ROPE_DIM // 2
    inv = ROPE_THETA ** (-jnp.arange(half, dtype=jnp.float32) / half)
    ang = pos.astype(jnp.float32)[:, None] * inv[None, :]
    cos = jnp.cos(ang)[None, :, None, :]
    sin = jnp.sin(ang)[None, :, None, :]
    xr = x[..., :ROPE_DIM].astype(jnp.float32)
    x1, x2 = xr[..., :half], xr[..., half:]
    rot = jnp.concatenate([x1 * cos - x2 * sin, x2 * cos + x1 * sin], axis=-1)
    return jnp.concatenate([rot.astype(x.dtype), x[..., ROPE_DIM:]], axis=-1)


def token_shift(x):
    return jnp.pad(x, ((0, 0), (1, 0), (0, 0)))[:, :-1]


def wkv7_scan(r, w, k, v, a, b):
    B, T, H, N = r.shape
    seq = tuple(jnp.moveaxis(t.astype(jnp.float32), 1, 0) for t in (r, w, k, v, a, b))

    def step(S, inp):
        r_t, w_t, k_t, v_t, a_t, b_t = inp
        sa = jnp.einsum('bhvk,bhk->bhv', S, a_t)
        S = S * w_t[:, :, None, :] + sa[..., None] * b_t[:, :, None, :] + v_t[..., None] * k_t[:, :, None, :]
        return S, jnp.einsum('bhvk,bhk->bhv', S, r_t)

    S0 = jnp.zeros((B, H, N, N), jnp.float32)
    _, o = lax.scan(step, S0, seq)
    return jnp.moveaxis(o, 0, 1)


def rwkv7_mixer(u, v_first, mu, w_rkvz, w0, w1, w2, a0, a1, a2, vres, k_k, k_a, r_k, ln_g, ln_b, w_o):
    B, T, D = u.shape
    xx = token_shift(u) - u
    mixed = u[None] + xx[None] * mu[:, None, None, :]
    rkvz = jnp.einsum('nbtd,nde->nbte', mixed[:4], w_rkvz)
    r, k, v, z = rkvz[0], rkvz[1], rkvz[2], rkvz[3]
    xv, xw, xa = mixed[2], mixed[4], mixed[5]
    w_log = -jax.nn.softplus(-(w0 + jnp.tanh(xw @ w1) @ w2)) - 0.5
    decay = jnp.exp(-jnp.exp(w_log.astype(jnp.float32)))
    a = jax.nn.sigmoid(a0 + (xa @ a1) @ a2)
    if vres is not None:
        v0, v1, v2 = vres
        v = v + (v_first - v) * jax.nn.sigmoid(v0 + (xv @ v1) @ v2)
    heads = lambda t: t.reshape(B, T, RWKV_HEADS, RWKV_HEAD_SIZE)
    kk = heads(k * k_k).astype(jnp.float32)
    kk = kk / jnp.maximum(jnp.sqrt(jnp.sum(kk * kk, axis=-1, keepdims=True)), 1e-12)
    k = k * (1.0 + (a - 1.0) * k_a)
    rh, kh, vh, ah = heads(r), heads(k), heads(v), heads(a)
    o = wkv7_scan(rh, heads(decay), kh, vh, -kk, kk * ah.astype(jnp.float32))
    mean = jnp.mean(o, axis=-1, keepdims=True)
    var = jnp.mean(jnp.square(o - mean), axis=-1, keepdims=True)
    o = ((o - mean) * lax.rsqrt(var + GN_EPS)).reshape(B, T, D) * ln_g + ln_b
    bonus = jnp.sum((rh * kh * r_k).astype(jnp.float32), axis=-1, keepdims=True) * vh
    o = (o + bonus.reshape(B, T, D)) * jax.nn.silu(z)
    return o.astype(u.dtype) @ w_o, v


def nsa_shared_kv(h, kv_norm_g, kv_w, cmp_pe, cmp_w1, cmp_w2):
    B, T, D = h.shape
    hn = rmsnorm(h, kv_norm_g)
    kv = (hn @ kv_w).reshape(B, T, 6, NSA_GROUPS, NSA_HEAD_DIM)
    k_c, v_c, k_s, v_s, k_w, v_w = [kv[:, :, i] for i in range(6)]
    pos = jnp.arange(T)
    k_s = rope_partial(k_s, pos)
    k_w = rope_partial(k_w, pos)
    n_cmp = (T - CMP_BLOCK) // CMP_STRIDE + 1
    idx = jnp.arange(n_cmp)[:, None] * CMP_STRIDE + jnp.arange(CMP_BLOCK)[None, :]

    def compress(t, j):
        blocks = t[:, idx] + cmp_pe[j][None, None, :, None, :]
        blocks = jnp.moveaxis(blocks, 3, 2).reshape(B, n_cmp, NSA_GROUPS, CMP_BLOCK * NSA_HEAD_DIM)
        return jax.nn.silu(blocks @ cmp_w1[j]) @ cmp_w2[j]

    k_cmp = rope_partial(compress(k_c, 0), idx[:, -1])
    v_cmp = compress(v_c, 1)
    to_g = lambda t: jnp.moveaxis(t, 2, 1)
    n_blk = T // SEL_BLOCK
    k_sel = to_g(k_s).reshape(B, NSA_GROUPS, n_blk, SEL_BLOCK, NSA_HEAD_DIM)
    v_sel = to_g(v_s).reshape(B, NSA_GROUPS, n_blk, SEL_BLOCK, NSA_HEAD_DIM)
    pad = ((0, 0), (0, 0), (WINDOW, 0), (0, 0))
    k_win = jnp.pad(to_g(k_w), pad)
    v_win = jnp.pad(to_g(v_w), pad)
    return (to_g(k_cmp), to_g(v_cmp), k_sel, v_sel, k_win, v_win)


def nsa_mixer(u, shared, w_in, gate_b, w_o):
    B, T, D = u.shape
    H, G, R, d = NSA_HEADS, NSA_GROUPS, NSA_REP, NSA_HEAD_DIM
    k_cmp, v_cmp, k_sel, v_sel, k_win, v_win = shared
    proj = u @ w_in
    q = rope_partial(proj[..., :NSA_WIDTH].reshape(B, T, H, d), jnp.arange(T))
    gl = proj[..., NSA_WIDTH:NSA_WIDTH + N_BRANCH * H] + gate_b
    z = proj[..., NSA_WIDTH + N_BRANCH * H:]
    gates = jax.nn.sigmoid(gl.astype(jnp.float32)).reshape(B, T, G, R, N_BRANCH)
    nq = T // Q_BLOCK
    q_blocks = q.reshape(B, nq, Q_BLOCK, G, R, d).transpose(1, 0, 3, 4, 2, 5)
    g_blocks = gates.reshape(B, nq, Q_BLOCK, G, R, N_BRANCH).transpose(1, 0, 3, 4, 2, 5)
    n_cmp = k_cmp.shape[2]
    n_blk = k_sel.shape[2]
    n_sel = min(SEL_TOP_N, n_blk)
    cmp_start = jnp.arange(n_cmp) * CMP_STRIDE
    cmp_end = cmp_start + CMP_BLOCK - 1
    cpos = cmp_start[:, None] + jnp.arange(CMP_BLOCK)[None, :]
    overlap = jax.nn.one_hot(cpos // SEL_BLOCK, n_blk, dtype=jnp.float32).mean(axis=1)
    blk = jnp.arange(n_blk)
    scale = d ** -0.5
    b_ix = jnp.arange(B)[:, None, None, None]
    g_ix = jnp.arange(G)[None, :, None, None]

    def block_fn(args):
        qb, gb, start = args
        tq = start + jnp.arange(Q_BLOCK)
        s = jnp.einsum('bgrqd,bgcd->bgrqc', qb, k_cmp).astype(jnp.float32) * scale
        vis = cmp_end[None, :] <= tq[:, None]
        p_cmp = jax.nn.softmax(jnp.where(vis, s, NEG_INF), axis=-1) * jnp.any(vis, axis=-1)[:, None]
        o_cmp = jnp.einsum('bgrqc,bgcd->bgrqd', p_cmp.astype(v_cmp.dtype), v_cmp)
        imp = jnp.einsum('bgrqc,cj->bgqj', p_cmp, overlap)
        cur = tq // SEL_BLOCK
        forced = (blk[None, :] == 0) | (blk[None, :] == cur[:, None]) | (blk[None, :] == cur[:, None] - 1)
        future = blk[None, :] > cur[:, None]
        imp = jnp.where(forced, FORCE_BONUS, jnp.where(future, NEG_INF, imp))
        _, sel = lax.top_k(imp, n_sel)
        ks = k_sel[b_ix, g_ix, sel]
        vs = v_sel[b_ix, g_ix, sel]
        s = jnp.einsum('bgrqd,bgqsld->bgrqsl', qb, ks).astype(jnp.float32) * scale
        kpos = sel[..., None] * SEL_BLOCK + jnp.arange(SEL_BLOCK)
        ok = kpos <= tq[None, None, :, None, None]
        s = jnp.where(ok[:, :, None], s, NEG_INF)
        p = jax.nn.softmax(s.reshape(B, G, R, Q_BLOCK, n_sel * SEL_BLOCK), axis=-1).reshape(s.shape)
        o_sel = jnp.einsum('bgrqsl,bgqsld->bgrqd', p.astype(vs.dtype), vs)
        kw = lax.dynamic_slice_in_dim(k_win, start, WINDOW + Q_BLOCK, axis=2)
        vw = lax.dynamic_slice_in_dim(v_win, start, WINDOW + Q_BLOCK, axis=2)
        wpos = start - WINDOW + jnp.arange(WINDOW + Q_BLOCK)
        diff = tq[:, None] - wpos[None, :]
        okw = (diff >= 0) & (diff < WINDOW) & (wpos[None, :] >= 0)
        s = jnp.einsum('bgrqd,bgkd->bgrqk', qb, kw).astype(jnp.float32) * scale
        p = jax.nn.softmax(jnp.where(okw, s, NEG_INF), axis=-1)
        o_win = jnp.einsum('bgrqk,bgkd->bgrqd', p.astype(vw.dtype), vw)
        o = gb[..., 0:1] * o_cmp + gb[..., 1:2] * o_sel + gb[..., 2:3] * o_win
        return o.astype(qb.dtype)

    starts = jnp.arange(nq) * Q_BLOCK
    o = lax.map(block_fn, (q_blocks, g_blocks, starts))
    o = o.transpose(1, 0, 4, 2, 3, 5).reshape(B, T, NSA_WIDTH)
    o = o * jax.nn.silu(z)
    return o @ w_o


def setup_inputs(seed: int = 0) -> dict:
    key = jax.random.key(seed)
    ks = jax.random.split(key, 32)
    D = D_MODEL
    na, nb = N_A_LAYERS, N_B_LAYERS
    nv = max(na - 1, 0)
    nrm = lambda i, shape, s: jax.random.normal(ks[i], shape, jnp.float32) * s
    return {
        'x': nrm(0, (BATCH, SEQ, D), 1.0),
        'a_norm_g': 1.0 + nrm(1, (na, D), 0.02),
        'a_mu': jax.random.uniform(ks[2], (na, 6, D), jnp.float32),
        'a_w_rkvz': nrm(3, (na, 4, D, D), D ** -0.5),
        'a_w0': jax.random.uniform(ks[4], (na, D), jnp.float32, minval=-6.0, maxval=0.0),
        'a_w1': nrm(5, (na, D, DECAY_LORA), D ** -0.5),
        'a_w2': nrm(6, (na, DECAY_LORA, D), 0.1 * DECAY_LORA ** -0.5),
        'a_a0': nrm(7, (na, D), 0.1),
        'a_a1': nrm(8, (na, D, AAA_LORA), D ** -0.5),
        'a_a2': nrm(9, (na, AAA_LORA, D), 0.1 * AAA_LORA ** -0.5),
        'a_v0': nrm(10, (nv, D), 0.1),
        'a_v1': nrm(11, (nv, D, MV_LORA), D ** -0.5),
        'a_v2': nrm(12, (nv, MV_LORA, D), 0.1 * MV_LORA ** -0.5),
        'a_k_k': 0.85 + nrm(13, (na, D), 0.02),
        'a_k_a': 1.0 + nrm(14, (na, D), 0.02),
        'a_r_k': nrm(15, (na, RWKV_HEADS, RWKV_HEAD_SIZE), 0.1),
        'a_ln_g': 1.0 + nrm(16, (na, D), 0.02),
        'a_ln_b': nrm(17, (na, D), 0.02),
        'a_w_o': nrm(18, (na, D, D), D ** -0.5),
        'kv_norm_g': 1.0 + nrm(19, (D,), 0.02),
        'kv_w': nrm(20, (D, 6 * KV_WIDTH), D ** -0.5),
        'cmp_pe': nrm(21, (2, CMP_BLOCK, NSA_HEAD_DIM), 0.1),
        'cmp_w1': nrm(22, (2, CMP_BLOCK * NSA_HEAD_DIM, CMP_HIDDEN), (CMP_BLOCK * NSA_HEAD_DIM) ** -0.5),
        'cmp_w2': nrm(23, (2, CMP_HIDDEN, NSA_HEAD_DIM), CMP_HIDDEN ** -0.5),
        'b_norm_g': 1.0 + nrm(24, (nb, D), 0.02),
        'b_w_in': nrm(25, (nb, D, 2 * NSA_WIDTH + N_BRANCH * NSA_HEADS), D ** -0.5),
        'b_gate_b': nrm(26, (nb, N_BRANCH * NSA_HEADS), 0.1),
        'b_w_o': nrm(27, (nb, NSA_WIDTH, D), NSA_WIDTH ** -0.5),
        'final_g': 1.0 + nrm(28, (D,), 0.02),
    }


def reference(x, a_norm_g, a_mu, a_w_rkvz, a_w0, a_w1, a_w2, a_a0, a_a1, a_a2, a_v0, a_v1, a_v2,
              a_k_k, a_k_a, a_r_k, a_ln_g, a_ln_b, a_w_o, kv_norm_g, kv_w, cmp_pe, cmp_w1, cmp_w2,
              b_norm_g, b_w_in, b_gate_b, b_w_o, final_g):
    h = x
    v_first = None
    shared = None
    for layer in range(DEPTH):
        if layer < N_A_LAYERS:
            i = layer
            vres = None if i == 0 else (a_v0[i - 1], a_v1[i - 1], a_v2[i - 1])
            y, v = rwkv7_mixer(rmsnorm(h, a_norm_g[i]), v_first, a_mu[i], a_w_rkvz[i],
                               a_w0[i], a_w1[i], a_w2[i], a_a0[i], a_a1[i], a_a2[i], vres,
                               a_k_k[i], a_k_a[i], a_r_k[i], a_ln_g[i], a_ln_b[i], a_w_o[i])
            if i == 0:
                v_first = v
            h = h + y
        else:
            if shared is None:
                shared = nsa_shared_kv(h, kv_norm_g, kv_w, cmp_pe, cmp_w1, cmp_w2)
            j = layer - N_A_LAYERS
            h = h + nsa_mixer(rmsnorm(h, b_norm_g[j]), shared, b_w_in[j], b_gate_b[j], b_w_o[j])
    return rmsnorm(h, final_g)
```

```python
import functools
import math

import jax
import jax.numpy as jnp
from jax import lax
from jax.experimental import pallas as pl
from jax.experimental.pallas import tpu as pltpu

F32 = jnp.float32
BF16 = jnp.bfloat16

NORM_EPS = 1e-6
GN_EPS = 64e-5
HEAD = 64
CHUNK = 64
UNIT = 256
UNIT_HEADS = UNIT // HEAD
EXP_M05 = math.exp(-0.5)

NSA_HEADS = 16
NSA_GROUPS = 4
NSA_REP = NSA_HEADS // NSA_GROUPS
HD = 128
N_BRANCH = 3
CMP_BLOCK = 32
CMP_STRIDE = 16
SEL_BLOCK = 64
SEL_TOP_N = 16
WINDOW = 512
QB = 64
ROPE_DIM = HD // 4
ROPE_THETA = 500000.0
NEG = -1e30
FORCE_BONUS = 1e4
SEL_TILE = 512
WIN_BAND = WINDOW + 128

VMEM_LIMIT = 56 * 1024 * 1024


def _dot(a, b):
    return jnp.dot(a, b, preferred_element_type=F32)


def _dot_nt(a, b):
    return lax.dot_general(a, b, (((1,), (1,)), ((), ())), preferred_element_type=F32)


def _dot_tn(a, b):
    return lax.dot_general(a, b, (((0,), (0,)), ((), ())), preferred_element_type=F32)


def _split3(x):
    h = x.astype(BF16)
    r1 = x - h.astype(F32)
    m = r1.astype(BF16)
    l = (r1 - m.astype(F32)).astype(BF16)
    return h, m, l


def _sigmoid(x):
    return 1.0 / (1.0 + jnp.exp(-x))


def _silu(x):
    return x * _sigmoid(x)


def _mm_body(*refs, has_res):
    if has_res:
        a_ref, w_ref, r_ref, o_ref = refs
    else:
        a_ref, w_ref, o_ref = refs
    acc = _dot(a_ref[...].astype(BF16), w_ref[...].astype(BF16))
    if has_res:
        acc = acc + r_ref[...]
    o_ref[...] = acc.astype(o_ref.dtype)


def matmul(a, w, res=None, out_dtype=F32, tm=1024, tn=1024):
    squeeze = a.ndim == 2
    if squeeze:
        a, w = a[None], w[None]
        res = None if res is None else res[None]
    G, M, K = a.shape
    N = w.shape[2]
    tm = min(tm, M)
    tn = min(tn, N)
    assert M % tm == 0 and N % tn == 0, (M, N, tm, tn)
    in_specs = [pl.BlockSpec((None, tm, K), lambda g, i, j: (g, i, 0)),
                pl.BlockSpec((None, K, tn), lambda g, i, j: (g, 0, j))]
    args = [a, w]
    if res is not None:
        in_specs.append(pl.BlockSpec((None, tm, tn), lambda g, i, j: (g, i, j)))
        args.append(res)
    out = pl.pallas_call(
        functools.partial(_mm_body, has_res=res is not None),
        out_shape=jax.ShapeDtypeStruct((G, M, N), out_dtype),
        grid=(G, M // tm, N // tn),
        in_specs=in_specs,
        out_specs=pl.BlockSpec((None, tm, tn), lambda g, i, j: (g, i, j)),
        compiler_params=pltpu.CompilerParams(
            dimension_semantics=("parallel", "parallel", "arbitrary"), vmem_limit_bytes=VMEM_LIMIT),
        name="matmul",
    )(*args)
    return out[0] if squeeze else out


def _lora_body(x_ref, a_ref, b_ref, bias_ref, o_ref):
    n = pl.program_id(0)
    mid = _dot(x_ref[...], a_ref[...])
    mid = jnp.where(n == 0, jnp.tanh(mid), mid)
    o_ref[...] = _dot(mid.astype(BF16), b_ref[...]) + bias_ref[...]


def lora(mixed, idx, a_w, b_w, bias, tm=1024):
    L, D, R = a_w.shape
    N = mixed.shape[1]
    tm = min(tm, N)

    def pick(n):
        out = jnp.int32(idx[-1])
        for t in range(L - 2, -1, -1):
            out = jnp.where(n == t, jnp.int32(idx[t]), out)
        return out

    return pl.pallas_call(
        _lora_body,
        out_shape=jax.ShapeDtypeStruct((L, N, D), F32),
        grid=(L, N // tm),
        in_specs=[pl.BlockSpec((None, tm, D), lambda n, i: (pick(n), i, 0)),
                  pl.BlockSpec((None, D, R), lambda n, i: (n, 0, 0)),
                  pl.BlockSpec((None, R, D), lambda n, i: (n, 0, 0)),
                  pl.BlockSpec((None, 1, D), lambda n, i: (n, 0, 0))],
        out_specs=pl.BlockSpec((None, tm, D), lambda n, i: (n, i, 0)),
        compiler_params=pltpu.CompilerParams(
            dimension_semantics=("parallel", "arbitrary"), vmem_limit_bytes=VMEM_LIMIT),
        name="lora",
    )(mixed, a_w, b_w, bias)


def _expand(x, lane_head):
    return jnp.concatenate([jnp.where(lane_head == h, x, 0.0) for h in range(UNIT_HEADS)], axis=0)


def _compact(xe):
    out = xe[0:CHUNK]
    for h in range(1, UNIT_HEADS):
        out = out + xe[h * CHUNK:(h + 1) * CHUNK]
    return out


def _segsum(x, ones_bd):
    h, m, l = _split3(x)
    return _dot(h, ones_bd) + _dot(m, ones_bd) + _dot(l, ones_bd)


def _wkv_unit(r, k, v, z, wl, al, vl, vf, k_k, k_a, r_k, ln_g, ln_b, s_ref, consts):
    tri, ones_bd, same, strict, incl, lane_head, eye = consts
    a = _sigmoid(al)
    lw = -EXP_M05 * _sigmoid(wl)
    if vl is not None:
        v = v + (vf - v) * _sigmoid(vl)
    kk = k * k_k
    kk = kk / jnp.maximum(jnp.sqrt(_segsum(kk * kk, ones_bd)), 1e-12)
    k2 = k * (1.0 + (a - 1.0) * k_a)
    bb = kk * a

    lh, lm, ll = _split3(lw)
    cum = _dot(tri, lh) + _dot(tri, lm) + _dot(tri, ll)
    p_incl = jnp.exp(cum)
    p_prev = jnp.exp(cum - lw)
    p_inv = jnp.exp(-cum)
    p_tot = p_incl[CHUNK - 1:CHUNK, :]

    at = -kk * p_prev
    rt = r * p_incl
    bt = bb * p_inv
    kt = k2 * p_inv

    ea = _expand(at, lane_head).astype(BF16)
    er32 = _expand(rt, lane_head)
    er = er32.astype(BF16)
    eb = _expand(bt, lane_head).astype(BF16)
    ek = _expand(kt, lane_head).astype(BF16)
    ev = _expand(v, lane_head).astype(BF16)

    a_ab = jnp.where(strict, _dot_nt(ea, eb), 0.0)
    a_ak = jnp.where(strict, _dot_nt(ea, ek), 0.0)
    a_rb = jnp.where(incl, _dot_nt(er, eb), 0.0).astype(BF16)
    a_rk = jnp.where(incl, _dot_nt(er, ek), 0.0).astype(BF16)

    tm = eye + a_ab
    pw = a_ab.astype(BF16)
    for _ in range(5):
        pw32 = _dot(pw, pw)
        pw = pw32.astype(BF16)
        tm = tm + _dot(pw, tm.astype(BF16))
    tb = tm.astype(BF16)

    akv = _dot(a_ak.astype(BF16), ev)
    u0 = _dot(tb, akv.astype(BF16))
    wt = _dot(tb, ea)
    u0b = u0.astype(BF16)
    o0 = _dot(a_rb, u0b) + _dot(a_rk, ev)
    qt = er32 + _dot(a_rb, wt.astype(BF16))

    qc = _compact(qt)
    wc = _compact(wt)
    u0c = _compact(u0)
    o0c = _compact(o0)

    s = s_ref[...]
    qw = _dot_nt(jnp.concatenate([qc, wc], axis=0).astype(BF16), s.astype(BF16))
    o = qw[0:CHUNK] + o0c
    u = u0c + qw[CHUNK:2 * CHUNK]
    upd = _dot_tn(jnp.concatenate([v, u], axis=0).astype(BF16),
                  jnp.concatenate([kt, bt], axis=0).astype(BF16))
    s_ref[...] = (s + jnp.where(same, upd, 0.0)) * p_tot

    mean = _segsum(o, ones_bd) * (1.0 / HEAD)
    oc = o - mean
    var = _segsum(oc * oc, ones_bd) * (1.0 / HEAD)
    on = oc * lax.rsqrt(var + GN_EPS) * ln_g + ln_b
    bonus = _segsum(r * k2 * r_k, ones_bd) * v
    return (on + bonus) * _silu(z)


def _wkv_body(*refs, has_vres, units):
    if has_vres:
        (r_ref, k_ref, v_ref, z_ref, wl_ref, al_ref, vl_ref, vf_ref,
         kk_ref, ka_ref, rk_ref, lg_ref, lb_ref, o_ref, s_ref) = refs
    else:
        (r_ref, k_ref, v_ref, z_ref, wl_ref, al_ref,
         kk_ref, ka_ref, rk_ref, lg_ref, lb_ref, o_ref, s_ref) = refs
        vl_ref = vf_ref = None

    @pl.when(pl.program_id(2) == 0)
    def _():
        s_ref[...] = jnp.zeros_like(s_ref)

    row = lax.broadcasted_iota(jnp.int32, (UNIT, UNIT), 0)
    col = lax.broadcasted_iota(jnp.int32, (UNIT, UNIT), 1)
    same = (row >> 6) == (col >> 6)
    strict = same & ((col & 63) < (row & 63))
    incl = same & ((col & 63) <= (row & 63))
    eye = jnp.where(row == col, 1.0, 0.0).astype(F32)
    ones_bd = jnp.where(same, 1.0, 0.0).astype(BF16)
    tr = lax.broadcasted_iota(jnp.int32, (CHUNK, CHUNK), 0)
    tc = lax.broadcasted_iota(jnp.int32, (CHUNK, CHUNK), 1)
    tri = jnp.where(tc <= tr, 1.0, 0.0).astype(BF16)
    lane_head = lax.broadcasted_iota(jnp.int32, (CHUNK, UNIT), 1) >> 6
    consts = (tri, ones_bd, same, strict, incl, lane_head, eye)

    for uu in range(units):
        sl = slice(uu * UNIT, (uu + 1) * UNIT)
        out = _wkv_unit(
            r_ref[:, sl], k_ref[:, sl], v_ref[:, sl], z_ref[:, sl], wl_ref[:, sl], al_ref[:, sl],
            None if vl_ref is None else vl_ref[:, sl], None if vf_ref is None else vf_ref[:, sl],
            kk_ref[:, sl], ka_ref[:, sl], rk_ref[:, sl], lg_ref[:, sl], lb_ref[:, sl],
            s_ref.at[uu], consts)
        o_ref[:, sl] = out.astype(o_ref.dtype)


def wkv(rkvz, lor, v_first, k_k, k_a, r_k, ln_g, ln_b, batch, units=2):
    _, N, D = rkvz.shape
    T = N // batch
    nc = T // CHUNK
    W = units * UNIT
    has_vres = v_first is not None
    row_map = lambda b, u, c: (b * nc + c, u)

    def lead(n):
        return pl.BlockSpec((None, CHUNK, W), lambda b, u, c: (n, b * nc + c, u))

    in_specs = [lead(0), lead(1), lead(2), lead(3), lead(0), lead(1)]
    args = [rkvz, rkvz, rkvz, rkvz, lor, lor]
    if has_vres:
        in_specs += [lead(2), pl.BlockSpec((CHUNK, W), row_map)]
        args += [lor, v_first]
    par = pl.BlockSpec((1, W), lambda b, u, c: (0, u))
    in_specs += [par] * 5
    args += [k_k.reshape(1, D), k_a.reshape(1, D), r_k.reshape(1, D), ln_g.reshape(1, D), ln_b.reshape(1, D)]
    return pl.pallas_call(
        functools.partial(_wkv_body, has_vres=has_vres, units=units),
        out_shape=jax.ShapeDtypeStruct((N, D), BF16),
        grid=(batch, D // W, nc),
        in_specs=in_specs,
        out_specs=pl.BlockSpec((CHUNK, W), row_map),
        scratch_shapes=[pltpu.VMEM((units, UNIT, UNIT), F32)],
        compiler_params=pltpu.CompilerParams(
            dimension_semantics=("parallel", "parallel", "arbitrary"), vmem_limit_bytes=VMEM_LIMIT),
        name="wkv7",
    )(*args)


def _softmax_rows(s):
    m = jnp.max(s, axis=1, keepdims=True)
    e = jnp.exp(s - m)
    return e * (1.0 / jnp.sum(e, axis=1, keepdims=True))


def _nsa_body(q_ref, z_ref, g_ref, rc_ref, rs1_ref, rs2_ref, ovt_ref,
              kc_ref, vc_ref, ks_ref, vs_ref, kw_ref, vw_ref, o_ref):
    i = pl.program_id(2)
    rows = NSA_REP * QB
    W = NSA_REP * HD

    q4 = q_ref[...]
    q4 = (q4 * rc_ref[...] + pltpu.roll(q4, W - ROPE_DIM // 2, 1) * rs1_ref[...]
          + pltpu.roll(q4, ROPE_DIM // 2, 1) * rs2_ref[...]) * (HD ** -0.5)
    qs = jnp.concatenate([q4[:, r * HD:(r + 1) * HD] for r in range(NSA_REP)], axis=0)
    qb = qs.astype(BF16)
    tq_col = i * QB + (lax.broadcasted_iota(jnp.int32, (rows, 1), 0) & (QB - 1))

    ncmp = kc_ref.shape[0]
    s = _dot_nt(qb, kc_ref[...])
    cend = lax.broadcasted_iota(jnp.int32, (rows, ncmp), 1) * CMP_STRIDE + (CMP_BLOCK - 1)
    s = jnp.where(cend <= tq_col, s, NEG)
    p = _softmax_rows(s) * jnp.where(tq_col >= CMP_BLOCK - 1, 1.0, 0.0)
    o_cmp = _dot(p.astype(BF16), vc_ref[...])

    psum = p[0:QB]
    for r in range(1, NSA_REP):
        psum = psum + p[r * QB:(r + 1) * QB]
    ph, pm, pl_ = _split3(psum)
    ov_t = ovt_ref[...]
    n_blk = ovt_ref.shape[0]
    imp_t = _dot_nt(ov_t, ph) + _dot_nt(ov_t, pm) + _dot_nt(ov_t, pl_)
    jsub = lax.broadcasted_iota(jnp.int32, (n_blk, QB), 0)
    forced = (jsub == 0) | (jsub == i) | (jsub == i - 1)
    imp_t = jnp.where(forced, FORCE_BONUS, jnp.where(jsub > i, NEG, imp_t))
    cnt = jnp.zeros((n_blk, QB), F32)
    for jp in range(n_blk):
        rowv = imp_t[jp:jp + 1, :]
        tie = jnp.where(jsub > jp, 1.0, 0.0)
        cnt = cnt + jnp.where(rowv > imp_t, 1.0, jnp.where(rowv == imp_t, tie, 0.0))
    bias_t = jnp.where(cnt < float(min(SEL_TOP_N, n_blk)), 0.0, NEG).astype(BF16)
    qi = lax.broadcasted_iota(jnp.int32, (QB, QB), 0)
    qj = lax.broadcasted_iota(jnp.int32, (QB, QB), 1)
    eye_q = jnp.where(qi == qj, 1.0, 0.0).astype(BF16)
    bias = _dot_nt(eye_q, bias_t)
    bias4 = jnp.concatenate([bias] * NSA_REP, axis=0).astype(BF16)
    pad = ks_ref.shape[1] - HD - n_blk
    parts = [qb, bias4] + ([jnp.zeros((rows, pad), BF16)] if pad else [])
    qa = jnp.concatenate(parts, axis=1)

    def tile(t, carry):
        m, l, acc = carry
        k0 = pl.multiple_of(t * SEL_TILE, SEL_TILE)
        s = _dot_nt(qa, ks_ref[pl.ds(k0, SEL_TILE), :])
        kpos = k0 + lax.broadcasted_iota(jnp.int32, (rows, SEL_TILE), 1)
        s = jnp.where(kpos <= tq_col, s, NEG)
        m_new = jnp.maximum(m, jnp.max(s, axis=1, keepdims=True))
        alpha = jnp.exp(m - m_new)
        e = jnp.exp(s - m_new)
        l = alpha * l + jnp.sum(e, axis=1, keepdims=True)
        acc = alpha * acc + _dot(e.astype(BF16), vs_ref[pl.ds(k0, SEL_TILE), :])
        return m_new, l, acc

    n_tiles = (i * QB) // SEL_TILE + 1
    m0 = jnp.full((rows, 1), NEG, F32)
    _, l, acc = lax.fori_loop(0, n_tiles, tile, (m0, jnp.zeros((rows, 1), F32), jnp.zeros((rows, HD), F32)))
    o_sel = acc * (1.0 / l)

    band = WIN_BAND
    w0 = pl.multiple_of(jnp.maximum((i + 1) * QB - band, 0), QB)
    s = _dot_nt(qb, kw_ref[pl.ds(w0, band), :])
    diff = tq_col - (w0 + lax.broadcasted_iota(jnp.int32, (rows, band), 1))
    s = jnp.where((diff >= 0) & (diff < WINDOW), s, NEG)
    o_win = _dot(_softmax_rows(s).astype(BF16), vw_ref[pl.ds(w0, band), :])

    g = _sigmoid(g_ref[...])
    outs = []
    for r in range(NSA_REP):
        rs = slice(r * QB, (r + 1) * QB)
        outs.append(g[:, 3 * r:3 * r + 1] * o_cmp[rs] + g[:, 3 * r + 1:3 * r + 2] * o_sel[rs]
                    + g[:, 3 * r + 2:3 * r + 3] * o_win[rs])
    o4 = jnp.concatenate(outs, axis=1)
    o_ref[...] = (o4 * _silu(z_ref[...])).astype(o_ref.dtype)


def nsa_attention(qz, gl, rope_q, overlap, shared, batch):
    N = qz.shape[0]
    T = N // batch
    nq = T // QB
    W = NSA_REP * HD
    k_cmp, v_cmp, k_sel, v_sel, k_win, v_win = shared
    rc, rs1, rs2 = rope_q
    ncmp = k_cmp.shape[2]
    n_blk = T // SEL_BLOCK
    row_blk = lambda b, g, i: (b * nq + i, g)
    kv = lambda w: pl.BlockSpec((None, None, T, w), lambda b, g, i: (b, g, 0, 0))
    cm = pl.BlockSpec((None, None, ncmp, HD), lambda b, g, i: (b, g, 0, 0))
    rope = pl.BlockSpec((QB, W), lambda b, g, i: (i, 0))
    return pl.pallas_call(
        _nsa_body,
        out_shape=jax.ShapeDtypeStruct((N, NSA_HEADS * HD), BF16),
        grid=(batch, NSA_GROUPS, nq),
        in_specs=[pl.BlockSpec((QB, W), row_blk),
                  pl.BlockSpec((QB, W), lambda b, g, i: (b * nq + i, NSA_GROUPS + g)),
                  pl.BlockSpec((None, QB, 128), lambda b, g, i: (g, b * nq + i, 0)),
                  rope, rope, rope,
                  pl.BlockSpec((n_blk, ncmp), lambda b, g, i: (0, 0)),
                  cm, cm, kv(k_sel.shape[3]), kv(HD), kv(HD), kv(HD)],
        out_specs=pl.BlockSpec((QB, W), row_blk),
        compiler_params=pltpu.CompilerParams(
            dimension_semantics=("parallel", "parallel", "arbitrary"), vmem_limit_bytes=VMEM_LIMIT),
        name="nsa_attention",
    )(qz, qz, gl, rc, rs1, rs2, overlap, k_cmp, v_cmp, k_sel, v_sel, k_win, v_win)


def _rmsnorm(x, g):
    return x * lax.rsqrt(jnp.mean(x * x, axis=-1, keepdims=True) + NORM_EPS) * g


def _rope_tables(pos, reps):
    half = ROPE_DIM // 2
    inv = ROPE_THETA ** (-jnp.arange(half, dtype=F32) / half)
    ang = pos.astype(F32)[:, None] * inv[None, :]
    cos, sin = jnp.cos(ang), jnp.sin(ang)
    n = pos.shape[0]
    c = jnp.concatenate([cos, cos, jnp.ones((n, HD - ROPE_DIM), F32)], axis=1)
    s1 = jnp.concatenate([-sin, jnp.zeros((n, HD - half), F32)], axis=1)
    s2 = jnp.concatenate([jnp.zeros((n, half), F32), sin, jnp.zeros((n, HD - ROPE_DIM), F32)], axis=1)
    return tuple(jnp.tile(t, (1, reps)) for t in (c, s1, s2))


def _rope_glue(x, tables):
    c, s1, s2 = tables
    half = ROPE_DIM // 2
    return x * c + jnp.roll(x, -half, axis=-1) * s1 + jnp.roll(x, half, axis=-1) * s2


def _rwkv_layer(h, batch, v_first, norm_g, mu, w_rkvz, w0, w1, w2, a0, a1, a2, vres, k_k, k_a, r_k, ln_g, ln_b, w_o):
    N, D = h.shape
    T = N // batch
    u = _rmsnorm(h, norm_g)
    prev = jnp.pad(u.reshape(batch, T, D), ((0, 0), (1, 0), (0, 0)))[:, :-1].reshape(N, D)
    xx = prev - u
    mixed = (u[None] + xx[None] * mu[:, None, :]).astype(BF16)
    rkvz = matmul(mixed[:4], w_rkvz.astype(BF16))

    loras = [(w1, w2, w0), (a1, a2, a0)]
    idx = [4, 5]
    if vres is not None:
        loras.append((vres[1], vres[2], vres[0]))
        idx.append(2)
    rpad = 128
    a_w = jnp.stack([jnp.pad(l[0], ((0, 0), (0, rpad - l[0].shape[1]))) for l in loras]).astype(BF16)
    b_w = jnp.stack([jnp.pad(l[1], ((0, rpad - l[1].shape[0]), (0, 0))) for l in loras]).astype(BF16)
    bias = jnp.stack([l[2].reshape(1, D) for l in loras])
    lor = lora(mixed, idx, a_w, b_w, bias)

    o = wkv(rkvz, lor, v_first, k_k, k_a, r_k.reshape(-1), ln_g, ln_b, batch)
    h = matmul(o, w_o.astype(BF16), res=h)
    return h, rkvz[2]


def _shared_kv(h, batch, kv_norm_g, kv_w, cmp_pe, cmp_w1, cmp_w2):
    N, D = h.shape
    T = N // batch
    G = NSA_GROUPS
    hn = _rmsnorm(h, kv_norm_g).astype(BF16)
    kv = matmul(hn, kv_w.astype(BF16))
    kv = kv.reshape(batch, T, 6, G, HD).transpose(2, 0, 3, 1, 4)
    k_c, v_c, k_s, v_s, k_w, v_w = [kv[j] for j in range(6)]
    tabs = _rope_tables(jnp.arange(T), 1)
    k_s = _rope_glue(k_s, tabs)
    k_w = _rope_glue(k_w, tabs)

    ng = T // CMP_STRIDE
    half = CMP_STRIDE * HD
    x16 = jnp.stack([k_c, v_c]).reshape(2, batch * G * ng, half)
    pe = cmp_pe.reshape(2, 1, CMP_BLOCK * HD)
    a_in = jnp.concatenate([x16 + pe[:, :, :half], x16 + pe[:, :, half:]], axis=0).astype(BF16)
    w_in = jnp.concatenate([cmp_w1[:, :half], cmp_w1[:, half:]], axis=0).astype(BF16)
    hd = matmul(a_in, w_in).reshape(2, 2, batch * G, ng, -1)
    hidden = hd[0] + jnp.roll(hd[1], -1, axis=2)
    hidden = _silu(hidden).astype(BF16).reshape(2, batch * G * ng, -1)
    cmp = matmul(hidden, cmp_w2.astype(BF16)).reshape(2, batch, G, ng, HD)
    k_cmp = _rope_glue(cmp[0], _rope_tables(jnp.arange(ng) * CMP_STRIDE + CMP_BLOCK - 1, 1))
    v_cmp = cmp[1]

    n_blk = T // SEL_BLOCK
    onehot = (jnp.arange(T)[:, None] // SEL_BLOCK == jnp.arange(n_blk)[None, :]).astype(BF16)
    aug = HD + n_blk
    aug_pad = (-aug) % 128
    k_sel = jnp.concatenate([k_s.astype(BF16), jnp.broadcast_to(onehot, (batch, G, T, n_blk)),
                             jnp.zeros((batch, G, T, aug_pad), BF16)], axis=-1)
    return (k_cmp.astype(BF16), v_cmp.astype(BF16), k_sel, v_s.astype(BF16), k_w.astype(BF16), v_w.astype(BF16))


def _nsa_layer(h, batch, shared, rope_q, overlap, norm_g, w_in, gate_b, w_o):
    N, D = h.shape
    G = NSA_GROUPS
    width = NSA_HEADS * HD
    ng = N_BRANCH * NSA_HEADS
    u = _rmsnorm(h, norm_g).astype(BF16)
    w_qz = jnp.concatenate([w_in[:, :width], w_in[:, width + ng:]], axis=1).astype(BF16)
    qz = matmul(u, w_qz)
    w_g = jnp.pad(w_in[:, width:width + ng], ((0, 0), (0, 128 - ng))).astype(BF16)
    gl = matmul(u, w_g)[:, :ng] + gate_b
    per = N_BRANCH * NSA_REP
    gl = jnp.pad(gl.reshape(N, G, per).transpose(1, 0, 2), ((0, 0), (0, 0), (0, 128 - per)))
    o = nsa_attention(qz, gl, rope_q, overlap, shared, batch)
    return matmul(o, w_o.astype(BF16), res=h)


def _final_norm_body(x_ref, g_ref, o_ref):
    x = x_ref[...]
    o_ref[...] = x * lax.rsqrt(jnp.mean(x * x, axis=-1, keepdims=True) + NORM_EPS) * g_ref[...]


def final_norm(h, g, tm=512):
    N, D = h.shape
    return pl.pallas_call(
        _final_norm_body,
        out_shape=jax.ShapeDtypeStruct((N, D), F32),
        grid=(N // tm,),
        in_specs=[pl.BlockSpec((tm, D), lambda i: (i, 0)), pl.BlockSpec((1, D), lambda i: (0, 0))],
        out_specs=pl.BlockSpec((tm, D), lambda i: (i, 0)),
        name="final_norm",
    )(h, g.reshape(1, D))


def kernel(x, a_norm_g, a_mu, a_w_rkvz, a_w0, a_w1, a_w2, a_a0, a_a1, a_a2, a_v0, a_v1, a_v2, a_k_k, a_k_a, a_r_k,
           a_ln_g, a_ln_b, a_w_o, kv_norm_g, kv_w, cmp_pe, cmp_w1, cmp_w2, b_norm_g, b_w_in, b_gate_b, b_w_o, final_g):
    B, T, D = x.shape
    N = B * T
    n_a = a_norm_g.shape[0]
    n_b = b_norm_g.shape[0]
    h = x.reshape(N, D)
    v_first = None
    for i in range(n_a):
        vres = None if i == 0 else (a_v0[i - 1], a_v1[i - 1], a_v2[i - 1])
        h, v = _rwkv_layer(h, B, v_first, a_norm_g[i], a_mu[i], a_w_rkvz[i], a_w0[i], a_w1[i], a_w2[i],
                           a_a0[i], a_a1[i], a_a2[i], vres, a_k_k[i], a_k_a[i], a_r_k[i],
                           a_ln_g[i], a_ln_b[i], a_w_o[i])
        if i == 0:
            v_first = v
    if n_b:
        shared = _shared_kv(h, B, kv_norm_g, kv_w, cmp_pe, cmp_w1, cmp_w2)
        rope_q = _rope_tables(jnp.arange(T), NSA_REP)
        ncmp = T // CMP_STRIDE
        n_blk = T // SEL_BLOCK
        cpos = jnp.arange(ncmp)[:, None] * CMP_STRIDE + jnp.arange(CMP_BLOCK)[None, :]
        overlap = jax.nn.one_hot(cpos // SEL_BLOCK, n_blk, dtype=F32).mean(axis=1).T.astype(BF16)
        for j in range(n_b):
            h = _nsa_layer(h, B, shared, rope_q, overlap, b_norm_g[j], b_w_in[j], b_gate_b[j], b_w_o[j])
    return final_norm(h, final_g).reshape(B, T, D)
```

```python
import functools
import math

import jax
import jax.numpy as jnp
from jax import lax
from jax.experimental import pallas as pl
from jax.experimental.pallas import tpu as pltpu

F32 = jnp.float32
BF16 = jnp.bfloat16

NORM_EPS = 1e-6
GN_EPS = 64e-5
HEAD = 64
CHUNK = 64
UNIT = 256
UNIT_HEADS = UNIT // HEAD
EXP_M05 = math.exp(-0.5)

NSA_HEADS = 16
NSA_GROUPS = 4
NSA_REP = NSA_HEADS // NSA_GROUPS
HD = 128
N_BRANCH = 3
CMP_BLOCK = 32
CMP_STRIDE = 16
SEL_BLOCK = 64
SEL_TOP_N = 16
WINDOW = 512
QB = 128
LOG2E = math.log2(math.e)
ROPE_DIM = HD // 4
ROPE_THETA = 500000.0
NEG = -1e30
FORCE_BONUS = 1e4
SEL_TILE = 512
WIN_BAND = WINDOW + 128

VMEM_LIMIT = 56 * 1024 * 1024


def _dot(a, b):
    return jnp.dot(a, b, preferred_element_type=F32)


def _dot_nt(a, b):
    return lax.dot_general(a, b, (((1,), (1,)), ((), ())), preferred_element_type=F32)


def _dot_tn(a, b):
    return lax.dot_general(a, b, (((0,), (0,)), ((), ())), preferred_element_type=F32)


def _split3(x):
    h = x.astype(BF16)
    r1 = x - h.astype(F32)
    m = r1.astype(BF16)
    l = (r1 - m.astype(F32)).astype(BF16)
    return h, m, l


def _sigmoid(x):
    return 1.0 / (1.0 + jnp.exp(-x))


def _silu(x):
    return x * _sigmoid(x)


def _mm_body(*refs, has_res):
    if has_res:
        a_ref, w_ref, r_ref, o_ref = refs
    else:
        a_ref, w_ref, o_ref = refs
    acc = _dot(a_ref[...].astype(BF16), w_ref[...].astype(BF16))
    if has_res:
        acc = acc + r_ref[...]
    o_ref[...] = acc.astype(o_ref.dtype)


def matmul(a, w, res=None, out_dtype=F32, tm=1024, tn=1024):
    squeeze = a.ndim == 2
    if squeeze:
        a, w = a[None], w[None]
        res = None if res is None else res[None]
    G, M, K = a.shape
    N = w.shape[2]
    tm = min(tm, M)
    tn = min(tn, N)
    assert M % tm == 0 and N % tn == 0, (M, N, tm, tn)
    in_specs = [pl.BlockSpec((None, tm, K), lambda g, i, j: (g, i, 0)),
                pl.BlockSpec((None, K, tn), lambda g, i, j: (g, 0, j))]
    args = [a, w]
    if res is not None:
        in_specs.append(pl.BlockSpec((None, tm, tn), lambda g, i, j: (g, i, j)))
        args.append(res)
    out = pl.pallas_call(
        functools.partial(_mm_body, has_res=res is not None),
        out_shape=jax.ShapeDtypeStruct((G, M, N), out_dtype),
        grid=(G, M // tm, N // tn),
        in_specs=in_specs,
        out_specs=pl.BlockSpec((None, tm, tn), lambda g, i, j: (g, i, j)),
        compiler_params=pltpu.CompilerParams(
            dimension_semantics=("parallel", "parallel", "arbitrary"), vmem_limit_bytes=VMEM_LIMIT),
        name="matmul",
    )(*args)
    return out[0] if squeeze else out


def _lora_body(x_ref, a_ref, b_ref, bias_ref, o_ref):
    n = pl.program_id(0)
    mid = _dot(x_ref[...], a_ref[...])
    mid = jnp.where(n == 0, jnp.tanh(mid), mid)
    o_ref[...] = _dot(mid.astype(BF16), b_ref[...]) + bias_ref[...]


def lora(mixed, idx, a_w, b_w, bias, tm=1024):
    L, D, R = a_w.shape
    N = mixed.shape[1]
    tm = min(tm, N)

    def pick(n):
        out = jnp.int32(idx[-1])
        for t in range(L - 2, -1, -1):
            out = jnp.where(n == t, jnp.int32(idx[t]), out)
        return out

    return pl.pallas_call(
        _lora_body,
        out_shape=jax.ShapeDtypeStruct((L, N, D), F32),
        grid=(L, N // tm),
        in_specs=[pl.BlockSpec((None, tm, D), lambda n, i: (pick(n), i, 0)),
                  pl.BlockSpec((None, D, R), lambda n, i: (n, 0, 0)),
                  pl.BlockSpec((None, R, D), lambda n, i: (n, 0, 0)),
                  pl.BlockSpec((None, 1, D), lambda n, i: (n, 0, 0))],
        out_specs=pl.BlockSpec((None, tm, D), lambda n, i: (n, i, 0)),
        compiler_params=pltpu.CompilerParams(
            dimension_semantics=("parallel", "arbitrary"), vmem_limit_bytes=VMEM_LIMIT),
        name="lora",
    )(mixed, a_w, b_w, bias)


def _expand(x, lane_head):
    return jnp.concatenate([jnp.where(lane_head == h, x, 0.0) for h in range(UNIT_HEADS)], axis=0)


def _compact(xe):
    out = xe[0:CHUNK]
    for h in range(1, UNIT_HEADS):
        out = out + xe[h * CHUNK:(h + 1) * CHUNK]
    return out


def _segsum(x, ones_bd):
    h, m, l = _split3(x)
    return _dot(h, ones_bd) + _dot(m, ones_bd) + _dot(l, ones_bd)


def _each(f, *lists):
    return [f(*xs) for xs in zip(*lists)]


def _wkv_units(r, k, v, z, wl, al, vl, vf, k_k, k_a, r_k, ln_g, ln_b, s_refs, consts):
    tri, ones_bd, same, strict, incl, lane_head, eye = consts
    bf = lambda x: x.astype(BF16)
    a = _each(_sigmoid, al)
    lw = _each(lambda x: -EXP_M05 * _sigmoid(x), wl)
    if vl is not None:
        v = _each(lambda v_, f_, g_: v_ + (f_ - v_) * _sigmoid(g_), v, vf, vl)
    kk = _each(lambda k_, c_: k_ * c_, k, k_k)
    n2 = _each(lambda x: _segsum(x * x, ones_bd), kk)
    kk = _each(lambda x, n_: x / jnp.maximum(jnp.sqrt(n_), 1e-12), kk, n2)
    k2 = _each(lambda k_, a_, c_: k_ * (1.0 + (a_ - 1.0) * c_), k, a, k_a)

    def cumsum(x):
        h, m, l = _split3(x)
        return _dot(tri, h) + _dot(tri, m) + _dot(tri, l)

    cum = _each(cumsum, lw)
    p_incl = _each(jnp.exp, cum)
    p_inv = _each(lambda c_: jnp.exp(-c_), cum)
    at = _each(lambda kk_, c_, l_: -kk_ * jnp.exp(c_ - l_), kk, cum, lw)
    rt = _each(lambda r_, p_: r_ * p_, r, p_incl)
    bt = _each(lambda kk_, a_, p_: kk_ * a_ * p_, kk, a, p_inv)
    kt = _each(lambda k_, p_: k_ * p_, k2, p_inv)

    ex = lambda x: _expand(x, lane_head)
    ea = _each(lambda x: bf(ex(x)), at)
    er32 = _each(ex, rt)
    er = _each(bf, er32)
    eb = _each(lambda x: bf(ex(x)), bt)
    ek = _each(lambda x: bf(ex(x)), kt)
    ev = _each(lambda x: bf(ex(x)), v)

    a_ab = _each(lambda x, y: jnp.where(strict, _dot_nt(x, y), 0.0), ea, eb)
    a_ak = _each(lambda x, y: bf(jnp.where(strict, _dot_nt(x, y), 0.0)), ea, ek)
    a_rb = _each(lambda x, y: bf(jnp.where(incl, _dot_nt(x, y), 0.0)), er, eb)
    a_rk = _each(lambda x, y: bf(jnp.where(incl, _dot_nt(x, y), 0.0)), er, ek)
    akv = _each(lambda x, y: bf(_dot(x, y)), a_ak, ev)

    tm = _each(lambda x: eye + x, a_ab)
    pw = _each(bf, a_ab)
    for _ in range(5):
        pw = _each(lambda x: bf(_dot(x, x)), pw)
        tm = _each(lambda t_, p_: t_ + _dot(p_, bf(t_)), tm, pw)
    tb = _each(bf, tm)

    u0 = _each(_dot, tb, akv)
    wt = _each(_dot, tb, ea)
    o0 = _each(lambda x, u_, y, v_: _dot(x, bf(u_)) + _dot(y, v_), a_rb, u0, a_rk, ev)
    qt = _each(lambda e_, x, w_: e_ + _dot(x, bf(w_)), er32, a_rb, wt)

    qwc = _each(lambda q_, w_: bf(jnp.concatenate([_compact(q_), _compact(w_)], axis=0)), qt, wt)
    s = [ref[...] for ref in s_refs]
    qw = _each(lambda x, s_: _dot_nt(x, bf(s_)), qwc, s)
    o = _each(lambda x, o_: x[0:CHUNK] + _compact(o_), qw, o0)
    u = _each(lambda x, u_: _compact(u_) + x[CHUNK:2 * CHUNK], qw, u0)
    upd = _each(lambda v_, u_, k_, b_: _dot_tn(bf(jnp.concatenate([v_, u_], axis=0)),
                                               bf(jnp.concatenate([k_, b_], axis=0))), v, u, kt, bt)
    for ref, s_, d_, p_ in zip(s_refs, s, upd, p_incl):
        ref[...] = (s_ + jnp.where(same, d_, 0.0)) * p_[CHUNK - 1:CHUNK, :]

    mean = _each(lambda x: _segsum(x, ones_bd) * (1.0 / HEAD), o)
    oc = _each(lambda x, m_: x - m_, o, mean)
    var = _each(lambda x: _segsum(x * x, ones_bd) * (1.0 / HEAD), oc)
    bonus = _each(lambda r_, k_, c_: _segsum(r_ * k_ * c_, ones_bd), r, k2, r_k)
    return _each(lambda oc_, var_, g_, b_, bo_, v_, z_:
                 (oc_ * lax.rsqrt(var_ + GN_EPS) * g_ + b_ + bo_ * v_) * _silu(z_),
                 oc, var, ln_g, ln_b, bonus, v, z)


def _wkv_body(*refs, has_vres, units):
    if has_vres:
        (r_ref, k_ref, v_ref, z_ref, wl_ref, al_ref, vl_ref, vf_ref,
         kk_ref, ka_ref, rk_ref, lg_ref, lb_ref, o_ref, s_ref) = refs
    else:
        (r_ref, k_ref, v_ref, z_ref, wl_ref, al_ref,
         kk_ref, ka_ref, rk_ref, lg_ref, lb_ref, o_ref, s_ref) = refs
        vl_ref = vf_ref = None

    @pl.when(pl.program_id(2) == 0)
    def _():
        s_ref[...] = jnp.zeros_like(s_ref)

    row = lax.broadcasted_iota(jnp.int32, (UNIT, UNIT), 0)
    col = lax.broadcasted_iota(jnp.int32, (UNIT, UNIT), 1)
    same = (row >> 6) == (col >> 6)
    strict = same & ((col & 63) < (row & 63))
    incl = same & ((col & 63) <= (row & 63))
    eye = jnp.where(row == col, 1.0, 0.0).astype(F32)
    ones_bd = jnp.where(same, 1.0, 0.0).astype(BF16)
    tr = lax.broadcasted_iota(jnp.int32, (CHUNK, CHUNK), 0)
    tc = lax.broadcasted_iota(jnp.int32, (CHUNK, CHUNK), 1)
    tri = jnp.where(tc <= tr, 1.0, 0.0).astype(BF16)
    lane_head = lax.broadcasted_iota(jnp.int32, (CHUNK, UNIT), 1) >> 6
    consts = (tri, ones_bd, same, strict, incl, lane_head, eye)

    sls = [slice(uu * UNIT, (uu + 1) * UNIT) for uu in range(units)]
    split = lambda ref: None if ref is None else [ref[:, sl] for sl in sls]
    outs = _wkv_units(
        split(r_ref), split(k_ref), split(v_ref), split(z_ref), split(wl_ref), split(al_ref),
        split(vl_ref), split(vf_ref), split(kk_ref), split(ka_ref), split(rk_ref), split(lg_ref), split(lb_ref),
        [s_ref.at[uu] for uu in range(units)], consts)
    for sl, out in zip(sls, outs):
        o_ref[:, sl] = out.astype(o_ref.dtype)


def wkv(rkvz, lor, v_first, k_k, k_a, r_k, ln_g, ln_b, batch, units=8):
    _, N, D = rkvz.shape
    T = N // batch
    nc = T // CHUNK
    W = units * UNIT
    has_vres = v_first is not None
    row_map = lambda b, u, c: (b * nc + c, u)

    def lead(n):
        return pl.BlockSpec((None, CHUNK, W), lambda b, u, c: (n, b * nc + c, u))

    in_specs = [lead(0), lead(1), lead(2), lead(3), lead(0), lead(1)]
    args = [rkvz, rkvz, rkvz, rkvz, lor, lor]
    if has_vres:
        in_specs += [lead(2), pl.BlockSpec((CHUNK, W), row_map)]
        args += [lor, v_first]
    par = pl.BlockSpec((1, W), lambda b, u, c: (0, u))
    in_specs += [par] * 5
    args += [k_k.reshape(1, D), k_a.reshape(1, D), r_k.reshape(1, D), ln_g.reshape(1, D), ln_b.reshape(1, D)]
    return pl.pallas_call(
        functools.partial(_wkv_body, has_vres=has_vres, units=units),
        out_shape=jax.ShapeDtypeStruct((N, D), BF16),
        grid=(batch, D // W, nc),
        in_specs=in_specs,
        out_specs=pl.BlockSpec((CHUNK, W), row_map),
        scratch_shapes=[pltpu.VMEM((units, UNIT, UNIT), F32)],
        compiler_params=pltpu.CompilerParams(
            dimension_semantics=("parallel", "parallel", "arbitrary"), vmem_limit_bytes=VMEM_LIMIT),
        name="wkv7",
    )(*args)


def _attend(s, v_aug):
    m = jnp.max(s, axis=1, keepdims=True)
    acc = _dot(jnp.exp2(s - m).astype(BF16), v_aug)
    return acc[:, :HD] * (1.0 / acc[:, HD:HD + 1])


def _nsa_body(q_ref, z_ref, g_ref, rc_ref, rs1_ref, rs2_ref, ovt_ref,
              kc_ref, vc_ref, ks_ref, vs_ref, kw_ref, vw_ref, o_ref, qa_scr, s0_scr, s1_scr, m_scr, acc_scr):
    i = pl.program_id(2)
    rows = NSA_REP * QB
    W = NSA_REP * HD

    q4 = q_ref[...]
    q4 = (q4 * rc_ref[...] + pltpu.roll(q4, W - ROPE_DIM // 2, 1) * rs1_ref[...]
          + pltpu.roll(q4, ROPE_DIM // 2, 1) * rs2_ref[...]) * (HD ** -0.5 * LOG2E)
    qs = jnp.concatenate([q4[:, r * HD:(r + 1) * HD] for r in range(NSA_REP)], axis=0)
    qb = qs.astype(BF16)
    tq_col = i * QB + (lax.broadcasted_iota(jnp.int32, (rows, 1), 0) & (QB - 1))

    ncmp = kc_ref.shape[0]
    band = WIN_BAND
    w0 = pl.multiple_of(jnp.maximum((i + 1) * QB - band, 0), QB)
    s_cmp = _dot_nt(qb, kc_ref[...])
    s_win = _dot_nt(qb, kw_ref[pl.ds(w0, band), :])

    cend = lax.broadcasted_iota(jnp.int32, (rows, ncmp), 1) * CMP_STRIDE + (CMP_BLOCK - 1)
    s_cmp = jnp.where(cend <= tq_col, s_cmp, NEG)
    e = jnp.exp2(s_cmp - jnp.max(s_cmp, axis=1, keepdims=True))
    p = e * (jnp.where(tq_col >= CMP_BLOCK - 1, 1.0, 0.0) / jnp.sum(e, axis=1, keepdims=True))
    o_cmp = _dot(p.astype(BF16), vc_ref[...])

    psum = p[0:QB]
    for r in range(1, NSA_REP):
        psum = psum + p[r * QB:(r + 1) * QB]
    ph, pm, pl_ = _split3(psum)
    ov_t = ovt_ref[...]
    n_blk = ovt_ref.shape[0]
    imp_t = _dot_nt(ov_t, ph) + _dot_nt(ov_t, pm) + _dot_nt(ov_t, pl_)

    diff = (tq_col - w0) - lax.broadcasted_iota(jnp.int32, (rows, band), 1)
    s_win = jnp.where(lax.bitcast_convert_type(diff, jnp.uint32) < jnp.uint32(WINDOW), s_win, NEG)
    o_win = _attend(s_win, vw_ref[pl.ds(w0, band), :])

    jsub = lax.broadcasted_iota(jnp.int32, (n_blk, QB), 0)
    cur = (i * QB + lax.broadcasted_iota(jnp.int32, (n_blk, QB), 1)) >> (SEL_BLOCK.bit_length() - 1)
    forced = (jsub == 0) | (jsub == cur) | (jsub == cur - 1)
    imp_t = jnp.where(forced, FORCE_BONUS, jnp.where(jsub > cur, NEG, imp_t))
    sub8 = lax.broadcasted_iota(jnp.int32, (8, QB), 0)
    grp = [imp_t[8 * a:8 * a + 8] for a in range(n_blk // 8)]
    cnt = [jnp.zeros((8, QB), F32) for _ in grp]
    for jp in range(n_blk):
        rowv = jnp.broadcast_to(imp_t[jp:jp + 1, :], (8, QB))
        for a, x in enumerate(grp):
            ge = lambda: jnp.where(rowv >= x, 1.0, 0.0)
            gt = lambda: jnp.where(rowv > x, 1.0, 0.0)
            if 8 * a > jp:
                inc = ge()
            elif 8 * a + 7 <= jp:
                inc = gt()
            else:
                inc = jnp.where(sub8 + 8 * a > jp, ge(), gt())
            cnt[a] = cnt[a] + inc
    cnt = jnp.concatenate(cnt, axis=0)
    bias_t = jnp.where(cnt < float(min(SEL_TOP_N, n_blk)), 0.0, NEG).astype(BF16)
    qi = lax.broadcasted_iota(jnp.int32, (QB, QB), 0)
    qj = lax.broadcasted_iota(jnp.int32, (QB, QB), 1)
    eye_q = jnp.where(qi == qj, 1.0, 0.0).astype(BF16)
    bias = _dot_nt(eye_q, bias_t)
    bias4 = jnp.concatenate([bias] * NSA_REP, axis=0).astype(BF16)
    pad = ks_ref.shape[1] - HD - n_blk
    parts = [qb, bias4] + ([jnp.zeros((rows, pad), BF16)] if pad else [])
    qa = jnp.concatenate(parts, axis=1)

    def tile_rows(t):
        return pl.ds(pl.multiple_of(t * SEL_TILE, SEL_TILE), SEL_TILE)

    def absorb(t, s):
        m = m_scr[...]
        m_new = jnp.maximum(m, jnp.max(s, axis=1, keepdims=True))
        pv = _dot(jnp.exp2(s - m_new).astype(BF16), vs_ref[tile_rows(t), :])
        acc_scr[...] = jnp.exp2(m - m_new) * acc_scr[...] + pv
        m_scr[...] = m_new

    def scores(t, dst):
        dst[...] = _dot_nt(qa_scr[...], ks_ref[tile_rows(t), :])

    def pair(t2, carry):
        t = 2 * t2
        scores(t + 1, s1_scr)
        absorb(t, s0_scr[...])
        scores(t + 2, s0_scr)
        absorb(t + 1, s1_scr[...])
        return carry

    t_last = (i * QB) // SEL_TILE
    odd = (t_last & 1) == 1
    qa_scr[...] = qa
    scores(0, s0_scr)
    m_scr[...] = jnp.full(m_scr.shape, NEG, F32)
    acc_scr[...] = jnp.zeros(acc_scr.shape, F32)
    lax.fori_loop(0, t_last // 2, pair, 0)

    @pl.when(odd)
    def _():
        scores(t_last, s1_scr)
        absorb(t_last - 1, s0_scr[...])

    kpos = t_last * SEL_TILE + lax.broadcasted_iota(jnp.int32, (rows, SEL_TILE), 1)
    s_diag = jnp.where(odd, s1_scr[...], s0_scr[...])
    absorb(t_last, jnp.where(kpos <= tq_col, s_diag, NEG))
    acc = acc_scr[...]
    o_sel = acc[:, :HD] * (1.0 / acc[:, HD:HD + 1])

    g = _sigmoid(g_ref[...])
    outs = []
    for r in range(NSA_REP):
        rs = slice(r * QB, (r + 1) * QB)
        outs.append(g[:, 3 * r:3 * r + 1] * o_cmp[rs] + g[:, 3 * r + 1:3 * r + 2] * o_sel[rs]
                    + g[:, 3 * r + 2:3 * r + 3] * o_win[rs])
    o4 = jnp.concatenate(outs, axis=1)
    o_ref[...] = (o4 * _silu(z_ref[...])).astype(o_ref.dtype)


def nsa_attention(qz, gl, rope_q, overlap, shared, batch):
    N = qz.shape[0]
    T = N // batch
    nq = T // QB
    W = NSA_REP * HD
    k_cmp, v_cmp, k_sel, v_sel, k_win, v_win = shared
    rc, rs1, rs2 = rope_q
    ncmp = k_cmp.shape[2]
    n_blk = T // SEL_BLOCK
    row_blk = lambda b, g, i: (b * nq + i, g)
    kv = lambda w: pl.BlockSpec((None, None, T, w), lambda b, g, i: (b, g, 0, 0))
    cm = pl.BlockSpec((None, None, ncmp, HD), lambda b, g, i: (b, g, 0, 0))
    rope = pl.BlockSpec((QB, W), lambda b, g, i: (i, 0))
    return pl.pallas_call(
        _nsa_body,
        out_shape=jax.ShapeDtypeStruct((N, NSA_HEADS * HD), BF16),
        grid=(batch, NSA_GROUPS, nq),
        in_specs=[pl.BlockSpec((QB, W), row_blk),
                  pl.BlockSpec((QB, W), lambda b, g, i: (b * nq + i, NSA_GROUPS + g)),
                  pl.BlockSpec((None, QB, 128), lambda b, g, i: (g, b * nq + i, 0)),
                  rope, rope, rope,
                  pl.BlockSpec((n_blk, ncmp), lambda b, g, i: (0, 0)),
                  cm, cm, kv(k_sel.shape[3]), kv(v_sel.shape[3]), kv(k_win.shape[3]), kv(v_win.shape[3])],
        out_specs=pl.BlockSpec((QB, W), row_blk),
        scratch_shapes=[pltpu.VMEM((NSA_REP * QB, k_sel.shape[3]), BF16),
                        pltpu.VMEM((NSA_REP * QB, SEL_TILE), F32),
                        pltpu.VMEM((NSA_REP * QB, SEL_TILE), F32),
                        pltpu.VMEM((NSA_REP * QB, 1), F32),
                        pltpu.VMEM((NSA_REP * QB, v_sel.shape[3]), F32)],
        compiler_params=pltpu.CompilerParams(
            dimension_semantics=("parallel", "parallel", "arbitrary"), vmem_limit_bytes=VMEM_LIMIT),
        name="nsa_attention",
    )(qz, qz, gl, rc, rs1, rs2, overlap, k_cmp, v_cmp, k_sel, v_sel, k_win, v_win)


def _rmsnorm(x, g):
    return x * lax.rsqrt(jnp.mean(x * x, axis=-1, keepdims=True) + NORM_EPS) * g


def _rope_tables(pos, reps):
    half = ROPE_DIM // 2
    inv = ROPE_THETA ** (-jnp.arange(half, dtype=F32) / half)
    ang = pos.astype(F32)[:, None] * inv[None, :]
    cos, sin = jnp.cos(ang), jnp.sin(ang)
    n = pos.shape[0]
    c = jnp.concatenate([cos, cos, jnp.ones((n, HD - ROPE_DIM), F32)], axis=1)
    s1 = jnp.concatenate([-sin, jnp.zeros((n, HD - half), F32)], axis=1)
    s2 = jnp.concatenate([jnp.zeros((n, half), F32), sin, jnp.zeros((n, HD - ROPE_DIM), F32)], axis=1)
    return tuple(jnp.tile(t, (1, reps)) for t in (c, s1, s2))


def _rope_glue(x, tables):
    c, s1, s2 = tables
    half = ROPE_DIM // 2
    return x * c + jnp.roll(x, -half, axis=-1) * s1 + jnp.roll(x, half, axis=-1) * s2


def _rwkv_layer(h, batch, v_first, norm_g, mu, w_rkvz, w0, w1, w2, a0, a1, a2, vres, k_k, k_a, r_k, ln_g, ln_b, w_o):
    N, D = h.shape
    T = N // batch
    u = _rmsnorm(h, norm_g)
    prev = jnp.pad(u.reshape(batch, T, D), ((0, 0), (1, 0), (0, 0)))[:, :-1].reshape(N, D)
    xx = prev - u
    mixed = (u[None] + xx[None] * mu[:, None, :]).astype(BF16)
    rkvz = matmul(mixed[:4], w_rkvz.astype(BF16))

    loras = [(w1, w2, w0), (a1, a2, a0)]
    idx = [4, 5]
    if vres is not None:
        loras.append((vres[1], vres[2], vres[0]))
        idx.append(2)
    rpad = 128
    a_w = jnp.stack([jnp.pad(l[0], ((0, 0), (0, rpad - l[0].shape[1]))) for l in loras]).astype(BF16)
    b_w = jnp.stack([jnp.pad(l[1], ((0, rpad - l[1].shape[0]), (0, 0))) for l in loras]).astype(BF16)
    bias = jnp.stack([l[2].reshape(1, D) for l in loras])
    lor = lora(mixed, idx, a_w, b_w, bias)

    o = wkv(rkvz, lor, v_first, k_k, k_a, r_k.reshape(-1), ln_g, ln_b, batch)
    h = matmul(o, w_o.astype(BF16), res=h)
    return h, rkvz[2]


def _shared_kv(h, batch, kv_norm_g, kv_w, cmp_pe, cmp_w1, cmp_w2):
    N, D = h.shape
    T = N // batch
    G = NSA_GROUPS
    hn = _rmsnorm(h, kv_norm_g).astype(BF16)
    kv = matmul(hn, kv_w.astype(BF16))
    kv = kv.reshape(batch, T, 6, G, HD).transpose(2, 0, 3, 1, 4)
    k_c, v_c, k_s, v_s, k_w, v_w = [kv[j] for j in range(6)]
    tabs = _rope_tables(jnp.arange(T), 1)
    k_s = _rope_glue(k_s, tabs)
    k_w = _rope_glue(k_w, tabs)

    ng = T // CMP_STRIDE
    half = CMP_STRIDE * HD
    x16 = jnp.stack([k_c, v_c]).reshape(2, batch * G * ng, half)
    pe = cmp_pe.reshape(2, 1, CMP_BLOCK * HD)
    a_in = jnp.concatenate([x16 + pe[:, :, :half], x16 + pe[:, :, half:]], axis=0).astype(BF16)
    w_in = jnp.concatenate([cmp_w1[:, :half], cmp_w1[:, half:]], axis=0).astype(BF16)
    hd = matmul(a_in, w_in).reshape(2, 2, batch * G, ng, -1)
    hidden = hd[0] + jnp.roll(hd[1], -1, axis=2)
    hidden = _silu(hidden).astype(BF16).reshape(2, batch * G * ng, -1)
    cmp = matmul(hidden, cmp_w2.astype(BF16)).reshape(2, batch, G, ng, HD)
    k_cmp = _rope_glue(cmp[0], _rope_tables(jnp.arange(ng) * CMP_STRIDE + CMP_BLOCK - 1, 1))
    v_cmp = cmp[1]

    n_blk = T // SEL_BLOCK
    onehot = (jnp.arange(T)[:, None] // SEL_BLOCK == jnp.arange(n_blk)[None, :]).astype(BF16)
    aug = HD + n_blk
    aug_pad = (-aug) % 128
    k_sel = jnp.concatenate([k_s.astype(BF16), jnp.broadcast_to(onehot, (batch, G, T, n_blk)),
                             jnp.zeros((batch, G, T, aug_pad), BF16)], axis=-1)
    ones_col = jnp.concatenate([jnp.ones((batch, G, T, 1), BF16), jnp.zeros((batch, G, T, HD - 1), BF16)], axis=-1)
    v_sel = jnp.concatenate([v_s.astype(BF16), ones_col], axis=-1)
    v_win = jnp.concatenate([v_w.astype(BF16), ones_col], axis=-1)
    return (k_cmp.astype(BF16), v_cmp.astype(BF16), k_sel, v_sel, k_w.astype(BF16), v_win)


def _nsa_layer(h, batch, shared, rope_q, overlap, norm_g, w_in, gate_b, w_o):
    N, D = h.shape
    G = NSA_GROUPS
    width = NSA_HEADS * HD
    ng = N_BRANCH * NSA_HEADS
    u = _rmsnorm(h, norm_g).astype(BF16)
    w_qz = jnp.concatenate([w_in[:, :width], w_in[:, width + ng:]], axis=1).astype(BF16)
    qz = matmul(u, w_qz)
    w_g = jnp.pad(w_in[:, width:width + ng], ((0, 0), (0, 128 - ng))).astype(BF16)
    gl = matmul(u, w_g)[:, :ng] + gate_b
    per = N_BRANCH * NSA_REP
    gl = jnp.pad(gl.reshape(N, G, per).transpose(1, 0, 2), ((0, 0), (0, 0), (0, 128 - per)))
    o = nsa_attention(qz, gl, rope_q, overlap, shared, batch)
    return matmul(o, w_o.astype(BF16), res=h)


def _final_norm_body(x_ref, g_ref, o_ref):
    x = x_ref[...]
    o_ref[...] = x * lax.rsqrt(jnp.mean(x * x, axis=-1, keepdims=True) + NORM_EPS) * g_ref[...]


def final_norm(h, g, tm=512):
    N, D = h.shape
    return pl.pallas_call(
        _final_norm_body,
        out_shape=jax.ShapeDtypeStruct((N, D), F32),
        grid=(N // tm,),
        in_specs=[pl.BlockSpec((tm, D), lambda i: (i, 0)), pl.BlockSpec((1, D), lambda i: (0, 0))],
        out_specs=pl.BlockSpec((tm, D), lambda i: (i, 0)),
        name="final_norm",
    )(h, g.reshape(1, D))


def kernel(x, a_norm_g, a_mu, a_w_rkvz, a_w0, a_w1, a_w2, a_a0, a_a1, a_a2, a_v0, a_v1, a_v2, a_k_k, a_k_a, a_r_k,
           a_ln_g, a_ln_b, a_w_o, kv_norm_g, kv_w, cmp_pe, cmp_w1, cmp_w2, b_norm_g, b_w_in, b_gate_b, b_w_o, final_g):
    B, T, D = x.shape
    N = B * T
    n_a = a_norm_g.shape[0]
    n_b = b_norm_g.shape[0]
    h = x.reshape(N, D)
    v_first = None
    for i in range(n_a):
        vres = None if i == 0 else (a_v0[i - 1], a_v1[i - 1], a_v2[i - 1])
        h, v = _rwkv_layer(h, B, v_first, a_norm_g[i], a_mu[i], a_w_rkvz[i], a_w0[i], a_w1[i], a_w2[i],
                           a_a0[i], a_a1[i], a_a2[i], vres, a_k_k[i], a_k_a[i], a_r_k[i],
                           a_ln_g[i], a_ln_b[i], a_w_o[i])
        if i == 0:
            v_first = v
    if n_b:
        shared = _shared_kv(h, B, kv_norm_g, kv_w, cmp_pe, cmp_w1, cmp_w2)
        rope_q = _rope_tables(jnp.arange(T), NSA_REP)
        ncmp = T // CMP_STRIDE
        n_blk = T // SEL_BLOCK
        cpos = jnp.arange(ncmp)[:, None] * CMP_STRIDE + jnp.arange(CMP_BLOCK)[None, :]
        overlap = jax.nn.one_hot(cpos // SEL_BLOCK, n_blk, dtype=F32).mean(axis=1).T.astype(BF16)
        for j in range(n_b):
            h = _nsa_layer(h, B, shared, rope_q, overlap, b_norm_g[j], b_w_in[j], b_gate_b[j], b_w_o[j])
    return final_norm(h, final_g).reshape(B, T, D)
```

```python
import functools
import math

import jax
import jax.numpy as jnp
from jax import lax
from jax.experimental import pallas as pl
from jax.experimental.pallas import tpu as pltpu

F32 = jnp.float32
BF16 = jnp.bfloat16

NORM_EPS = 1e-6
GN_EPS = 64e-5
HEAD = 64
CHUNK = 64
UNIT = 256
UNIT_HEADS = UNIT // HEAD
EXP_M05 = math.exp(-0.5)

NSA_HEADS = 16
NSA_GROUPS = 4
NSA_REP = NSA_HEADS // NSA_GROUPS
HD = 128
N_BRANCH = 3
CMP_BLOCK = 32
CMP_STRIDE = 16
SEL_BLOCK = 64
SEL_TOP_N = 16
WINDOW = 512
QB = 128
LOG2E = math.log2(math.e)
ROPE_DIM = HD // 4
ROPE_THETA = 500000.0
NEG = -1e30
FORCE_BONUS = 1e4
SEL_TILE = 512
WIN_BAND = WINDOW + 128

VMEM_LIMIT = 56 * 1024 * 1024


def _dot(a, b):
    return jnp.dot(a, b, preferred_element_type=F32)


def _dot_nt(a, b):
    return lax.dot_general(a, b, (((1,), (1,)), ((), ())), preferred_element_type=F32)


def _dot_tn(a, b):
    return lax.dot_general(a, b, (((0,), (0,)), ((), ())), preferred_element_type=F32)


def _split3(x):
    h = x.astype(BF16)
    r1 = x - h.astype(F32)
    m = r1.astype(BF16)
    l = (r1 - m.astype(F32)).astype(BF16)
    return h, m, l


def _sigmoid(x):
    return 1.0 / (1.0 + jnp.exp(-x))


def _silu(x):
    return x * _sigmoid(x)


def _mm_body(*refs, has_res):
    if has_res:
        a_ref, w_ref, r_ref, o_ref = refs
    else:
        a_ref, w_ref, o_ref = refs
    acc = _dot(a_ref[...].astype(BF16), w_ref[...].astype(BF16))
    if has_res:
        acc = acc + r_ref[...]
    o_ref[...] = acc.astype(o_ref.dtype)


def matmul(a, w, res=None, out_dtype=F32, tm=1024, tn=1024):
    squeeze = a.ndim == 2
    if squeeze:
        a, w = a[None], w[None]
        res = None if res is None else res[None]
    G, K, N = w.shape
    M = a.shape[1]
    tm = min(tm, M)
    tn = min(tn, N)
    assert M % tm == 0 and N % tn == 0, (M, N, tm, tn)
    in_specs = [pl.BlockSpec((None, tm, K), lambda g, i, j: (g, i, 0)),
                pl.BlockSpec((None, K, tn), lambda g, i, j: (g, 0, j))]
    args = [a, w]
    if res is not None:
        in_specs.append(pl.BlockSpec((None, tm, tn), lambda g, i, j: (g, i, j)))
        args.append(res)
    out = pl.pallas_call(
        functools.partial(_mm_body, has_res=res is not None),
        out_shape=jax.ShapeDtypeStruct((G, M, N), out_dtype),
        grid=(G, M // tm, N // tn),
        in_specs=in_specs,
        out_specs=pl.BlockSpec((None, tm, tn), lambda g, i, j: (g, i, j)),
        compiler_params=pltpu.CompilerParams(
            dimension_semantics=("parallel", "parallel", "arbitrary"), vmem_limit_bytes=VMEM_LIMIT),
        name="matmul",
    )(*args)
    return out[0] if squeeze else out


def _lora_body(x_ref, a_ref, b_ref, bias_ref, o_ref):
    n = pl.program_id(0)
    mid = _dot(x_ref[...], a_ref[...])
    mid = jnp.where(n == 0, jnp.tanh(mid), mid)
    o_ref[...] = _dot(mid.astype(BF16), b_ref[...]) + bias_ref[...]


def lora(mixed, idx, a_w, b_w, bias, tm=1024):
    L, D, R = a_w.shape
    N = mixed.shape[1]
    tm = min(tm, N)

    def pick(n):
        out = jnp.int32(idx[-1])
        for t in range(L - 2, -1, -1):
            out = jnp.where(n == t, jnp.int32(idx[t]), out)
        return out

    return pl.pallas_call(
        _lora_body,
        out_shape=jax.ShapeDtypeStruct((L, N, D), F32),
        grid=(L, N // tm),
        in_specs=[pl.BlockSpec((None, tm, D), lambda n, i: (pick(n), i, 0)),
                  pl.BlockSpec((None, D, R), lambda n, i: (n, 0, 0)),
                  pl.BlockSpec((None, R, D), lambda n, i: (n, 0, 0)),
                  pl.BlockSpec((None, 1, D), lambda n, i: (n, 0, 0))],
        out_specs=pl.BlockSpec((None, tm, D), lambda n, i: (n, i, 0)),
        compiler_params=pltpu.CompilerParams(
            dimension_semantics=("parallel", "arbitrary"), vmem_limit_bytes=VMEM_LIMIT),
        name="lora",
    )(mixed, a_w, b_w, bias)


def _expand(x, lane_head):
    return jnp.concatenate([jnp.where(lane_head == h, x, 0.0) for h in range(UNIT_HEADS)], axis=0)


def _compact(xe):
    out = xe[0:CHUNK]
    for h in range(1, UNIT_HEADS):
        out = out + xe[h * CHUNK:(h + 1) * CHUNK]
    return out


def _segsum(x, ones_bd):
    h, m, l = _split3(x)
    return _dot(h, ones_bd) + _dot(m, ones_bd) + _dot(l, ones_bd)


def _each(f, *lists):
    return [f(*xs) for xs in zip(*lists)]


def _wkv_units(r, k, v, z, wl, al, vl, vf, k_k, k_a, r_k, ln_g, ln_b, s_refs, consts):
    tri, ones_bd, same, strict, incl, lane_head, eye = consts
    bf = lambda x: x.astype(BF16)
    a = _each(_sigmoid, al)
    lw = _each(lambda x: -EXP_M05 * _sigmoid(x), wl)
    if vl is not None:
        v = _each(lambda v_, f_, g_: v_ + (f_ - v_) * _sigmoid(g_), v, vf, vl)
    kk = _each(lambda k_, c_: k_ * c_, k, k_k)
    n2 = _each(lambda x: _segsum(x * x, ones_bd), kk)
    kk = _each(lambda x, n_: x / jnp.maximum(jnp.sqrt(n_), 1e-12), kk, n2)
    k2 = _each(lambda k_, a_, c_: k_ * (1.0 + (a_ - 1.0) * c_), k, a, k_a)

    def cumsum(x):
        h, m, l = _split3(x)
        return _dot(tri, h) + _dot(tri, m) + _dot(tri, l)

    cum = _each(cumsum, lw)
    p_incl = _each(jnp.exp, cum)
    p_inv = _each(lambda c_: jnp.exp(-c_), cum)
    at = _each(lambda kk_, c_, l_: -kk_ * jnp.exp(c_ - l_), kk, cum, lw)
    rt = _each(lambda r_, p_: r_ * p_, r, p_incl)
    bt = _each(lambda kk_, a_, p_: kk_ * a_ * p_, kk, a, p_inv)
    kt = _each(lambda k_, p_: k_ * p_, k2, p_inv)

    ex = lambda x: _expand(x, lane_head)
    ea = _each(lambda x: bf(ex(x)), at)
    er32 = _each(ex, rt)
    er = _each(bf, er32)
    eb = _each(lambda x: bf(ex(x)), bt)
    ek = _each(lambda x: bf(ex(x)), kt)
    ev = _each(lambda x: bf(ex(x)), v)

    a_ab = _each(lambda x, y: jnp.where(strict, _dot_nt(x, y), 0.0), ea, eb)
    a_ak = _each(lambda x, y: bf(jnp.where(strict, _dot_nt(x, y), 0.0)), ea, ek)
    a_rb = _each(lambda x, y: bf(jnp.where(incl, _dot_nt(x, y), 0.0)), er, eb)
    a_rk = _each(lambda x, y: bf(jnp.where(incl, _dot_nt(x, y), 0.0)), er, ek)
    akv = _each(lambda x, y: bf(_dot(x, y)), a_ak, ev)

    tm = _each(lambda x: eye + x, a_ab)
    pw = _each(bf, a_ab)
    for _ in range(5):
        pw = _each(lambda x: bf(_dot(x, x)), pw)
        tm = _each(lambda t_, p_: t_ + _dot(p_, bf(t_)), tm, pw)
    tb = _each(bf, tm)

    u0 = _each(_dot, tb, akv)
    wt = _each(_dot, tb, ea)
    o0 = _each(lambda x, u_, y, v_: _dot(x, bf(u_)) + _dot(y, v_), a_rb, u0, a_rk, ev)
    qt = _each(lambda e_, x, w_: e_ + _dot(x, bf(w_)), er32, a_rb, wt)

    qwc = _each(lambda q_, w_: bf(jnp.concatenate([_compact(q_), _compact(w_)], axis=0)), qt, wt)
    s = [ref[...] for ref in s_refs]
    qw = _each(lambda x, s_: _dot_nt(x, bf(s_)), qwc, s)
    o = _each(lambda x, o_: x[0:CHUNK] + _compact(o_), qw, o0)
    u = _each(lambda x, u_: _compact(u_) + x[CHUNK:2 * CHUNK], qw, u0)
    upd = _each(lambda v_, u_, k_, b_: _dot_tn(bf(jnp.concatenate([v_, u_], axis=0)),
                                               bf(jnp.concatenate([k_, b_], axis=0))), v, u, kt, bt)
    for ref, s_, d_, p_ in zip(s_refs, s, upd, p_incl):
        ref[...] = (s_ + jnp.where(same, d_, 0.0)) * p_[CHUNK - 1:CHUNK, :]

    mean = _each(lambda x: _segsum(x, ones_bd) * (1.0 / HEAD), o)
    oc = _each(lambda x, m_: x - m_, o, mean)
    var = _each(lambda x: _segsum(x * x, ones_bd) * (1.0 / HEAD), oc)
    bonus = _each(lambda r_, k_, c_: _segsum(r_ * k_ * c_, ones_bd), r, k2, r_k)
    return _each(lambda oc_, var_, g_, b_, bo_, v_, z_:
                 (oc_ * lax.rsqrt(var_ + GN_EPS) * g_ + b_ + bo_ * v_) * _silu(z_),
                 oc, var, ln_g, ln_b, bonus, v, z)


def _wkv_body(*refs, has_vres, units):
    if has_vres:
        (r_ref, k_ref, v_ref, z_ref, wl_ref, al_ref, vl_ref, vf_ref,
         kk_ref, ka_ref, rk_ref, lg_ref, lb_ref, o_ref, s_ref) = refs
    else:
        (r_ref, k_ref, v_ref, z_ref, wl_ref, al_ref,
         kk_ref, ka_ref, rk_ref, lg_ref, lb_ref, o_ref, s_ref) = refs
        vl_ref = vf_ref = None

    @pl.when(pl.program_id(2) == 0)
    def _():
        s_ref[...] = jnp.zeros_like(s_ref)

    row = lax.broadcasted_iota(jnp.int32, (UNIT, UNIT), 0)
    col = lax.broadcasted_iota(jnp.int32, (UNIT, UNIT), 1)
    same = (row >> 6) == (col >> 6)
    strict = same & ((col & 63) < (row & 63))
    incl = same & ((col & 63) <= (row & 63))
    eye = jnp.where(row == col, 1.0, 0.0).astype(F32)
    ones_bd = jnp.where(same, 1.0, 0.0).astype(BF16)
    tr = lax.broadcasted_iota(jnp.int32, (CHUNK, CHUNK), 0)
    tc = lax.broadcasted_iota(jnp.int32, (CHUNK, CHUNK), 1)
    tri = jnp.where(tc <= tr, 1.0, 0.0).astype(BF16)
    lane_head = lax.broadcasted_iota(jnp.int32, (CHUNK, UNIT), 1) >> 6
    consts = (tri, ones_bd, same, strict, incl, lane_head, eye)

    sls = [slice(uu * UNIT, (uu + 1) * UNIT) for uu in range(units)]
    split = lambda ref: None if ref is None else [ref[:, sl] for sl in sls]
    outs = _wkv_units(
        split(r_ref), split(k_ref), split(v_ref), split(z_ref), split(wl_ref), split(al_ref),
        split(vl_ref), split(vf_ref), split(kk_ref), split(ka_ref), split(rk_ref), split(lg_ref), split(lb_ref),
        [s_ref.at[uu] for uu in range(units)], consts)
    for sl, out in zip(sls, outs):
        o_ref[:, sl] = out.astype(o_ref.dtype)


def wkv(rkvz, lor, v_first, k_k, k_a, r_k, ln_g, ln_b, batch, units=8):
    _, N, D = rkvz.shape
    T = N // batch
    nc = T // CHUNK
    W = units * UNIT
    has_vres = v_first is not None
    row_map = lambda b, u, c: (b * nc + c, u)

    def lead(n):
        return pl.BlockSpec((None, CHUNK, W), lambda b, u, c: (n, b * nc + c, u))

    in_specs = [lead(0), lead(1), lead(2), lead(3), lead(0), lead(1)]
    args = [rkvz, rkvz, rkvz, rkvz, lor, lor]
    if has_vres:
        in_specs += [lead(2), lead(2)]
        args += [lor, v_first]
    par = pl.BlockSpec((1, W), lambda b, u, c: (0, u))
    in_specs += [par] * 5
    args += [k_k.reshape(1, D), k_a.reshape(1, D), r_k.reshape(1, D), ln_g.reshape(1, D), ln_b.reshape(1, D)]
    return pl.pallas_call(
        functools.partial(_wkv_body, has_vres=has_vres, units=units),
        out_shape=jax.ShapeDtypeStruct((N, D), BF16),
        grid=(batch, D // W, nc),
        in_specs=in_specs,
        out_specs=pl.BlockSpec((CHUNK, W), row_map),
        scratch_shapes=[pltpu.VMEM((units, UNIT, UNIT), F32)],
        compiler_params=pltpu.CompilerParams(
            dimension_semantics=("parallel", "parallel", "arbitrary"), vmem_limit_bytes=VMEM_LIMIT),
        name="wkv7",
    )(*args)


def _attend(s, v_aug):
    m = jnp.max(s, axis=1, keepdims=True)
    acc = _dot(jnp.exp2(s - m).astype(BF16), v_aug)
    return acc[:, :HD] * (1.0 / acc[:, HD:HD + 1])


def _rope(x, c, s1, s2):
    n = x.shape[1] // HD
    if n > 1:
        c, s1, s2 = (jnp.concatenate([t] * n, axis=1) for t in (c, s1, s2))
    half = ROPE_DIM // 2
    return x * c + pltpu.roll(x, x.shape[1] - half, 1) * s1 + pltpu.roll(x, half, 1) * s2


def _nsa_body(q_ref, z_ref, g_ref, gb_ref, rc_ref, rs1_ref, rs2_ref, ovt_ref,
              kc_ref, vc_ref, ks_ref, vs_ref, kw_ref, vw_ref, o_ref, qa_scr, s0_scr, s1_scr, m_scr, acc_scr):
    i = pl.program_id(2)
    rows = NSA_REP * QB

    q4 = _rope(q_ref[...], rc_ref[...], rs1_ref[...], rs2_ref[...]) * (HD ** -0.5 * LOG2E)
    qs = jnp.concatenate([q4[:, r * HD:(r + 1) * HD] for r in range(NSA_REP)], axis=0)
    qb = qs.astype(BF16)
    tq_col = i * QB + (lax.broadcasted_iota(jnp.int32, (rows, 1), 0) & (QB - 1))

    ncmp = kc_ref.shape[0]
    band = WIN_BAND
    w0 = pl.multiple_of(jnp.maximum((i + 1) * QB - band, 0), QB)
    s_cmp = _dot_nt(qb, kc_ref[...])
    s_win = _dot_nt(qb, kw_ref[pl.ds(w0, band), :])

    cend = lax.broadcasted_iota(jnp.int32, (rows, ncmp), 1) * CMP_STRIDE + (CMP_BLOCK - 1)
    s_cmp = jnp.where(cend <= tq_col, s_cmp, NEG)
    e = jnp.exp2(s_cmp - jnp.max(s_cmp, axis=1, keepdims=True))
    p = e * (jnp.where(tq_col >= CMP_BLOCK - 1, 1.0, 0.0) / jnp.sum(e, axis=1, keepdims=True))
    o_cmp = _dot(p.astype(BF16), vc_ref[...])

    psum = p[0:QB]
    for r in range(1, NSA_REP):
        psum = psum + p[r * QB:(r + 1) * QB]
    ph, pm, pl_ = _split3(psum)
    ov_t = ovt_ref[...]
    n_blk = ovt_ref.shape[0]
    imp_t = _dot_nt(ov_t, ph) + _dot_nt(ov_t, pm) + _dot_nt(ov_t, pl_)

    diff = (tq_col - w0) - lax.broadcasted_iota(jnp.int32, (rows, band), 1)
    s_win = jnp.where(lax.bitcast_convert_type(diff, jnp.uint32) < jnp.uint32(WINDOW), s_win, NEG)
    o_win = _attend(s_win, vw_ref[pl.ds(w0, band), :])

    jsub = lax.broadcasted_iota(jnp.int32, (n_blk, QB), 0)
    cur = (i * QB + lax.broadcasted_iota(jnp.int32, (n_blk, QB), 1)) >> (SEL_BLOCK.bit_length() - 1)
    forced = (jsub == 0) | (jsub == cur) | (jsub == cur - 1)
    imp_t = jnp.where(forced, FORCE_BONUS, jnp.where(jsub > cur, NEG, imp_t))
    sub8 = lax.broadcasted_iota(jnp.int32, (8, QB), 0)
    grp = [imp_t[8 * a:8 * a + 8] for a in range(n_blk // 8)]
    cnt = [jnp.zeros((8, QB), F32) for _ in grp]
    for jp in range(n_blk):
        rowv = jnp.broadcast_to(imp_t[jp:jp + 1, :], (8, QB))
        for a, x in enumerate(grp):
            ge = lambda: jnp.where(rowv >= x, 1.0, 0.0)
            gt = lambda: jnp.where(rowv > x, 1.0, 0.0)
            if 8 * a > jp:
                inc = ge()
            elif 8 * a + 7 <= jp:
                inc = gt()
            else:
                inc = jnp.where(sub8 + 8 * a > jp, ge(), gt())
            cnt[a] = cnt[a] + inc
    cnt = jnp.concatenate(cnt, axis=0)
    bias_t = jnp.where(cnt < float(min(SEL_TOP_N, n_blk)), 0.0, NEG).astype(BF16)
    qi = lax.broadcasted_iota(jnp.int32, (QB, QB), 0)
    qj = lax.broadcasted_iota(jnp.int32, (QB, QB), 1)
    eye_q = jnp.where(qi == qj, 1.0, 0.0).astype(BF16)
    bias = _dot_nt(eye_q, bias_t)
    bias4 = jnp.concatenate([bias] * NSA_REP, axis=0).astype(BF16)
    pad = ks_ref.shape[1] - HD - n_blk
    parts = [qb, bias4] + ([jnp.zeros((rows, pad), BF16)] if pad else [])
    qa = jnp.concatenate(parts, axis=1)

    def tile_rows(t):
        return pl.ds(pl.multiple_of(t * SEL_TILE, SEL_TILE), SEL_TILE)

    def absorb(t, s):
        m = m_scr[...]
        m_new = jnp.maximum(m, jnp.max(s, axis=1, keepdims=True))
        pv = _dot(jnp.exp2(s - m_new).astype(BF16), vs_ref[tile_rows(t), :])
        acc_scr[...] = jnp.exp2(m - m_new) * acc_scr[...] + pv
        m_scr[...] = m_new

    def scores(t, dst):
        dst[...] = _dot_nt(qa_scr[...], ks_ref[tile_rows(t), :])

    def pair(t2, carry):
        t = 2 * t2
        scores(t + 1, s1_scr)
        absorb(t, s0_scr[...])
        scores(t + 2, s0_scr)
        absorb(t + 1, s1_scr[...])
        return carry

    t_last = (i * QB) // SEL_TILE
    odd = (t_last & 1) == 1
    qa_scr[...] = qa
    scores(0, s0_scr)
    m_scr[...] = jnp.full(m_scr.shape, NEG, F32)
    acc_scr[...] = jnp.zeros(acc_scr.shape, F32)
    lax.fori_loop(0, t_last // 2, pair, 0)

    @pl.when(odd)
    def _():
        scores(t_last, s1_scr)
        absorb(t_last - 1, s0_scr[...])

    kpos = t_last * SEL_TILE + lax.broadcasted_iota(jnp.int32, (rows, SEL_TILE), 1)
    s_diag = jnp.where(odd, s1_scr[...], s0_scr[...])
    absorb(t_last, jnp.where(kpos <= tq_col, s_diag, NEG))
    acc = acc_scr[...]
    o_sel = acc[:, :HD] * (1.0 / acc[:, HD:HD + 1])

    g = _sigmoid(g_ref[...] + gb_ref[...])
    outs = []
    for r in range(NSA_REP):
        rs = slice(r * QB, (r + 1) * QB)
        outs.append(g[:, 3 * r:3 * r + 1] * o_cmp[rs] + g[:, 3 * r + 1:3 * r + 2] * o_sel[rs]
                    + g[:, 3 * r + 2:3 * r + 3] * o_win[rs])
    o4 = jnp.concatenate(outs, axis=1)
    o_ref[...] = (o4 * _silu(z_ref[...])).astype(o_ref.dtype)


def nsa_attention(proj, gate_b, rope_q, overlap, shared, batch):
    N = proj.shape[0]
    T = N // batch
    nq = T // QB
    W = NSA_REP * HD
    k_cmp, v_cmp, k_sel, v_sel, k_win, v_win = shared
    rc, rs1, rs2 = rope_q
    ncmp = k_cmp.shape[2]
    n_blk = T // SEL_BLOCK
    row_blk = lambda b, g, i: (b * nq + i, g)
    kv = lambda w: pl.BlockSpec((None, None, T, w), lambda b, g, i: (b, g, 0, 0))
    cm = pl.BlockSpec((None, None, ncmp, HD), lambda b, g, i: (b, g, 0, 0))
    rope = pl.BlockSpec((QB, HD), lambda b, g, i: (i, 0))
    gate_blk0 = 2 * NSA_HEADS * HD // 128
    return pl.pallas_call(
        _nsa_body,
        out_shape=jax.ShapeDtypeStruct((N, NSA_HEADS * HD), BF16),
        grid=(batch, NSA_GROUPS, nq),
        in_specs=[pl.BlockSpec((QB, W), row_blk),
                  pl.BlockSpec((QB, W), lambda b, g, i: (b * nq + i, NSA_GROUPS + g)),
                  pl.BlockSpec((QB, 128), lambda b, g, i: (b * nq + i, gate_blk0 + g)),
                  pl.BlockSpec((1, 128), lambda b, g, i: (0, g)),
                  rope, rope, rope,
                  pl.BlockSpec((n_blk, ncmp), lambda b, g, i: (0, 0)),
                  cm, cm, kv(k_sel.shape[3]), kv(v_sel.shape[3]), kv(k_win.shape[3]), kv(v_win.shape[3])],
        out_specs=pl.BlockSpec((QB, W), row_blk),
        scratch_shapes=[pltpu.VMEM((NSA_REP * QB, k_sel.shape[3]), BF16),
                        pltpu.VMEM((NSA_REP * QB, SEL_TILE), F32),
                        pltpu.VMEM((NSA_REP * QB, SEL_TILE), F32),
                        pltpu.VMEM((NSA_REP * QB, 1), F32),
                        pltpu.VMEM((NSA_REP * QB, v_sel.shape[3]), F32)],
        compiler_params=pltpu.CompilerParams(
            dimension_semantics=("parallel", "parallel", "arbitrary"), vmem_limit_bytes=VMEM_LIMIT),
        name="nsa_attention",
    )(proj, proj, proj, gate_b, rc, rs1, rs2, overlap, k_cmp, v_cmp, k_sel, v_sel, k_win, v_win)


def _rms(x, g):
    return x * lax.rsqrt(jnp.mean(x * x, axis=-1, keepdims=True) + NORM_EPS) * g


def _norm_body(x_ref, g_ref, o_ref):
    o_ref[...] = _rms(x_ref[...], g_ref[...]).astype(o_ref.dtype)


def rmsnorm(h, g, out_dtype, tm=512):
    N, D = h.shape
    return pl.pallas_call(
        _norm_body,
        out_shape=jax.ShapeDtypeStruct((N, D), out_dtype),
        grid=(N // tm,),
        in_specs=[pl.BlockSpec((tm, D), lambda i: (i, 0)), pl.BlockSpec((1, D), lambda i: (0, 0))],
        out_specs=pl.BlockSpec((tm, D), lambda i: (i, 0)),
        compiler_params=pltpu.CompilerParams(dimension_semantics=("parallel",)),
        name="rmsnorm",
    )(h, g.reshape(1, D))


def _mix_body(x_ref, xp_ref, g_ref, mu_ref, o_ref, *, seq):
    tm = x_ref.shape[0]
    g = g_ref[...]
    u = _rms(x_ref[...], g)
    last_prev = _rms(xp_ref[...], g)[xp_ref.shape[0] - 1:, :]
    first = lax.rem(pl.program_id(0) * tm, seq) == 0
    row = lax.broadcasted_iota(jnp.int32, (tm, 1), 0)
    prev = jnp.where(row == 0, jnp.where(first, 0.0, last_prev), pltpu.roll(u, 1, 0))
    xx = prev - u
    for n in range(o_ref.shape[0]):
        o_ref[n] = (u + xx * mu_ref[n:n + 1, :]).astype(o_ref.dtype)


def rwkv_mix(h, g, mu, seq, tm=256):
    N, D = h.shape
    n_mix = mu.shape[0]
    assert seq % tm == 0
    sub = 8
    return pl.pallas_call(
        functools.partial(_mix_body, seq=seq),
        out_shape=jax.ShapeDtypeStruct((n_mix, N, D), BF16),
        grid=(N // tm,),
        in_specs=[pl.BlockSpec((tm, D), lambda i: (i, 0)),
                  pl.BlockSpec((sub, D), lambda i: (jnp.maximum(i * (tm // sub) - 1, 0), 0)),
                  pl.BlockSpec((1, D), lambda i: (0, 0)),
                  pl.BlockSpec((n_mix, D), lambda i: (0, 0))],
        out_specs=pl.BlockSpec((n_mix, tm, D), lambda i: (0, i, 0)),
        compiler_params=pltpu.CompilerParams(dimension_semantics=("parallel",), vmem_limit_bytes=VMEM_LIMIT),
        name="rwkv_mix",
    )(h, h, g.reshape(1, D), mu)


def _kvpost_body(kv_ref, rc_ref, rs1_ref, rs2_ref, pek_ref, pev_ref, cx_ref, ks_ref, vs_ref, kw_ref, vw_ref):
    tt = kv_ref.shape[0]
    aug = ks_ref.shape[-1] - HD
    tok = pl.program_id(1) * tt + lax.broadcasted_iota(jnp.int32, (tt, aug), 0)
    lane = lax.broadcasted_iota(jnp.int32, (tt, aug), 1)
    onehot = jnp.where(lane == (tok >> (SEL_BLOCK.bit_length() - 1)), 1.0, 0.0).astype(BF16)
    ones_col = jnp.where(lax.broadcasted_iota(jnp.int32, (tt, vs_ref.shape[-1] - HD), 1) == 0, 1.0, 0.0).astype(BF16)
    tabs = (rc_ref[...], rs1_ref[...], rs2_ref[...])
    G = NSA_GROUPS
    for g in range(G):
        k_c, v_c, k_s, v_s, k_w, v_w = [kv_ref[:, (j * G + g) * HD:(j * G + g + 1) * HD] for j in range(6)]
        cx_ref[0, g] = (k_c + pek_ref[0]).astype(BF16)
        cx_ref[1, g] = (v_c + pev_ref[0]).astype(BF16)
        cx_ref[2, g] = (k_c + pek_ref[1]).astype(BF16)
        cx_ref[3, g] = (v_c + pev_ref[1]).astype(BF16)
        ks_ref[g] = jnp.concatenate([_rope(k_s, *tabs).astype(BF16), onehot], axis=1)
        vs_ref[g] = jnp.concatenate([v_s.astype(BF16), ones_col], axis=1)
        kw_ref[g] = _rope(k_w, *tabs).astype(BF16)
        vw_ref[g] = jnp.concatenate([v_w.astype(BF16), ones_col], axis=1)


def kv_post(kv, rope_t, cmp_pe, batch, tt=512):
    N = kv.shape[0]
    T = N // batch
    G = NSA_GROUPS
    nt = T // tt
    n_blk = T // SEL_BLOCK
    aug = -(-n_blk // 128) * 128
    reps = tt // CMP_STRIDE
    pe = jnp.stack([jnp.tile(cmp_pe[:, :CMP_STRIDE], (1, reps, 1)), jnp.tile(cmp_pe[:, CMP_STRIDE:], (1, reps, 1))], 1)
    tab = pl.BlockSpec((tt, HD), lambda b, i: (i, 0))
    pes = pl.BlockSpec((2, tt, HD), lambda b, i: (0, 0, 0))
    out4 = lambda w: pl.BlockSpec((None, G, tt, w), lambda b, i: (b, 0, i, 0))
    sds = lambda w: jax.ShapeDtypeStruct((batch, G, T, w), BF16)
    return pl.pallas_call(
        _kvpost_body,
        out_shape=(jax.ShapeDtypeStruct((4, batch, G, T, HD), BF16), sds(HD + aug), sds(2 * HD), sds(HD), sds(2 * HD)),
        grid=(batch, nt),
        in_specs=[pl.BlockSpec((tt, kv.shape[1]), lambda b, i: (b * nt + i, 0)), tab, tab, tab, pes, pes],
        out_specs=(pl.BlockSpec((4, None, G, tt, HD), lambda b, i: (0, b, 0, i, 0)),
                   out4(HD + aug), out4(2 * HD), out4(HD), out4(2 * HD)),
        compiler_params=pltpu.CompilerParams(dimension_semantics=("parallel", "parallel"), vmem_limit_bytes=VMEM_LIMIT),
        name="kv_post",
    )(kv, *rope_t, pe[0], pe[1])


def _cmp2_body(lo_ref, hi_ref, w2_ref, rc_ref, rs1_ref, rs2_ref, o_ref):
    n = lo_ref.shape[0]
    hid = lo_ref[...] + pltpu.roll(hi_ref[...], n - 1, 0)
    out = _dot(_silu(hid).astype(BF16), w2_ref[...])
    roped = _rope(out, rc_ref[...], rs1_ref[...], rs2_ref[...])
    o_ref[...] = jnp.where(pl.program_id(0) == 0, roped, out).astype(o_ref.dtype)


def cmp_mlp2(hd, w2, rope_c, groups, ng):
    H = hd.shape[2]
    tab = pl.BlockSpec((ng, HD), lambda j, m: (0, 0))
    return pl.pallas_call(
        _cmp2_body,
        out_shape=jax.ShapeDtypeStruct((2, groups, ng, HD), BF16),
        grid=(2, groups),
        in_specs=[pl.BlockSpec((None, ng, H), lambda j, m: (j, m, 0)),
                  pl.BlockSpec((None, ng, H), lambda j, m: (2 + j, m, 0)),
                  pl.BlockSpec((None, H, HD), lambda j, m: (j, 0, 0)), tab, tab, tab],
        out_specs=pl.BlockSpec((None, None, ng, HD), lambda j, m: (j, m, 0, 0)),
        compiler_params=pltpu.CompilerParams(dimension_semantics=("parallel", "parallel")),
        name="cmp_mlp2",
    )(hd, hd, w2, *rope_c)


def _rope_tables(pos):
    half = ROPE_DIM // 2
    inv = ROPE_THETA ** (-jnp.arange(half, dtype=F32) / half)
    ang = pos.astype(F32)[:, None] * inv[None, :]
    cos, sin = jnp.cos(ang), jnp.sin(ang)
    n = pos.shape[0]
    c = jnp.concatenate([cos, cos, jnp.ones((n, HD - ROPE_DIM), F32)], axis=1)
    s1 = jnp.concatenate([-sin, jnp.zeros((n, HD - half), F32)], axis=1)
    s2 = jnp.concatenate([jnp.zeros((n, half), F32), sin, jnp.zeros((n, HD - ROPE_DIM), F32)], axis=1)
    return c, s1, s2


def _rwkv_layer(h, batch, v_first, norm_g, mu, w_rkvz, w0, w1, w2, a0, a1, a2, vres, k_k, k_a, r_k, ln_g, ln_b, w_o):
    N, D = h.shape
    mixed = rwkv_mix(h, norm_g, mu, N // batch)
    rkvz = matmul(mixed, w_rkvz.astype(BF16))

    loras = [(w1, w2, w0), (a1, a2, a0)]
    idx = [4, 5]
    if vres is not None:
        loras.append((vres[1], vres[2], vres[0]))
        idx.append(2)
    rpad = 128
    a_w = jnp.stack([jnp.pad(l[0], ((0, 0), (0, rpad - l[0].shape[1]))) for l in loras]).astype(BF16)
    b_w = jnp.stack([jnp.pad(l[1], ((0, rpad - l[1].shape[0]), (0, 0))) for l in loras]).astype(BF16)
    bias = jnp.stack([l[2].reshape(1, D) for l in loras])
    lor = lora(mixed, idx, a_w, b_w, bias)

    o = wkv(rkvz, lor, v_first, k_k, k_a, r_k.reshape(-1), ln_g, ln_b, batch)
    return matmul(o, w_o.astype(BF16), res=h), rkvz


def _shared_kv(h, batch, rope_t, kv_norm_g, kv_w, cmp_pe, cmp_w1, cmp_w2):
    N, D = h.shape
    T = N // batch
    G = NSA_GROUPS
    kv = matmul(rmsnorm(h, kv_norm_g, BF16), kv_w.astype(BF16))
    cx, k_sel, v_sel, k_win, v_win = kv_post(kv, rope_t, cmp_pe, batch)

    ng = T // CMP_STRIDE
    half = CMP_STRIDE * HD
    w1 = jnp.concatenate([cmp_w1[:, :half], cmp_w1[:, half:]], axis=0).astype(BF16)
    hd = matmul(cx.reshape(4, batch * G * ng, half), w1)
    rope_c = _rope_tables(jnp.arange(ng) * CMP_STRIDE + CMP_BLOCK - 1)
    cmp = cmp_mlp2(hd, cmp_w2.astype(BF16), rope_c, batch * G, ng).reshape(2, batch, G, ng, HD)
    return cmp[0], cmp[1], k_sel, v_sel, k_win, v_win


def _nsa_layer(h, batch, shared, rope_t, overlap, norm_g, w_in, gate_b, w_o):
    N, D = h.shape
    G = NSA_GROUPS
    width = NSA_HEADS * HD
    ng = N_BRANCH * NSA_HEADS
    per = N_BRANCH * NSA_REP
    w_g = jnp.pad(w_in[:, width:width + ng].reshape(D, G, per), ((0, 0), (0, 0), (0, 128 - per))).reshape(D, G * 128)
    w_all = jnp.concatenate([w_in[:, :width], w_in[:, width + ng:], w_g], axis=1).astype(BF16)
    b_g = jnp.pad(gate_b.reshape(G, per), ((0, 0), (0, 128 - per))).reshape(1, G * 128)
    proj = matmul(rmsnorm(h, norm_g, BF16), w_all, tn=768)
    o = nsa_attention(proj, b_g, rope_t, overlap, shared, batch)
    return matmul(o, w_o.astype(BF16), res=h)


def kernel(x, a_norm_g, a_mu, a_w_rkvz, a_w0, a_w1, a_w2, a_a0, a_a1, a_a2, a_v0, a_v1, a_v2, a_k_k, a_k_a, a_r_k,
           a_ln_g, a_ln_b, a_w_o, kv_norm_g, kv_w, cmp_pe, cmp_w1, cmp_w2, b_norm_g, b_w_in, b_gate_b, b_w_o, final_g):
    B, T, D = x.shape
    N = B * T
    n_a = a_norm_g.shape[0]
    n_b = b_norm_g.shape[0]
    h = x.reshape(N, D)
    v_first = None
    for i in range(n_a):
        vres = None if i == 0 else (a_v0[i - 1], a_v1[i - 1], a_v2[i - 1])
        h, rkvz = _rwkv_layer(h, B, v_first, a_norm_g[i], a_mu[i], a_w_rkvz[i], a_w0[i], a_w1[i], a_w2[i],
                              a_a0[i], a_a1[i], a_a2[i], vres, a_k_k[i], a_k_a[i], a_r_k[i],
                              a_ln_g[i], a_ln_b[i], a_w_o[i])
        if i == 0:
            v_first = rkvz
    if n_b:
        rope_t = _rope_tables(jnp.arange(T))
        shared = _shared_kv(h, B, rope_t, kv_norm_g, kv_w, cmp_pe, cmp_w1, cmp_w2)
        ncmp = T // CMP_STRIDE
        n_blk = T // SEL_BLOCK
        cpos = jnp.arange(ncmp)[:, None] * CMP_STRIDE + jnp.arange(CMP_BLOCK)[None, :]
        overlap = jax.nn.one_hot(cpos // SEL_BLOCK, n_blk, dtype=F32).mean(axis=1).T.astype(BF16)
        for j in range(n_b):
            h = _nsa_layer(h, B, shared, rope_t, overlap, b_norm_g[j], b_w_in[j], b_gate_b[j], b_w_o[j])
    return rmsnorm(h, final_g, F32).reshape(B, T, D)
```

```python
import functools
import math
import types

import jax
import jax.numpy as jnp
from jax import lax
from jax.experimental import pallas as pl
from jax.experimental.pallas import tpu as pltpu

F32 = jnp.float32
BF16 = jnp.bfloat16

NORM_EPS = 1e-6
GN_EPS = 64e-5
HEAD = 64
CHUNK = 64
UNIT = 256
UNIT_HEADS = UNIT // HEAD
WKV_GROUP = 4
WKV_LAG = 12
EXP_M05 = math.exp(-0.5)

NSA_HEADS = 16
NSA_GROUPS = 4
NSA_REP = NSA_HEADS // NSA_GROUPS
HD = 128
N_BRANCH = 3
CMP_BLOCK = 32
CMP_STRIDE = 16
SEL_BLOCK = 64
SEL_TOP_N = 16
WINDOW = 512
QB = 128
NSA_GROUPS_PER_STEP = 2
LOG2E = math.log2(math.e)
ROPE_DIM = HD // 4
ROPE_THETA = 500000.0
NEG = -1e30
FORCE_BONUS = 1e4
SEL_TILE = 512
WIN_BAND = WINDOW + 128

VMEM_LIMIT = 56 * 1024 * 1024


def _dot(a, b):
    return jnp.dot(a, b, preferred_element_type=F32)


def _dot_nt(a, b):
    return lax.dot_general(a, b, (((1,), (1,)), ((), ())), preferred_element_type=F32)


def _dot_tn(a, b):
    return lax.dot_general(a, b, (((0,), (0,)), ((), ())), preferred_element_type=F32)


def _split3(x):
    h = x.astype(BF16)
    r1 = x - h.astype(F32)
    m = r1.astype(BF16)
    l = (r1 - m.astype(F32)).astype(BF16)
    return h, m, l


def _sigmoid(x):
    return 1.0 / (1.0 + jnp.exp(-x))


def _silu(x):
    return x * _sigmoid(x)


def _mm_body(*refs, has_res):
    if has_res:
        a_ref, w_ref, r_ref, o_ref = refs
    else:
        a_ref, w_ref, o_ref = refs
    acc = _dot(a_ref[...].astype(BF16), w_ref[...].astype(BF16))
    if has_res:
        acc = acc + r_ref[...]
    o_ref[...] = acc.astype(o_ref.dtype)


def matmul(a, w, res=None, out_dtype=F32, tm=1024, tn=1024):
    squeeze = a.ndim == 2
    if squeeze:
        a, w = a[None], w[None]
        res = None if res is None else res[None]
    G, K, N = w.shape
    M = a.shape[1]
    tm = min(tm, M)
    tn = min(tn, N)
    assert M % tm == 0 and N % tn == 0, (M, N, tm, tn)
    in_specs = [pl.BlockSpec((None, tm, K), lambda g, i, j: (g, i, 0)),
                pl.BlockSpec((None, K, tn), lambda g, i, j: (g, 0, j))]
    args = [a, w]
    if res is not None:
        in_specs.append(pl.BlockSpec((None, tm, tn), lambda g, i, j: (g, i, j)))
        args.append(res)
    out = pl.pallas_call(
        functools.partial(_mm_body, has_res=res is not None),
        out_shape=jax.ShapeDtypeStruct((G, M, N), out_dtype),
        grid=(G, M // tm, N // tn),
        in_specs=in_specs,
        out_specs=pl.BlockSpec((None, tm, tn), lambda g, i, j: (g, i, j)),
        compiler_params=pltpu.CompilerParams(
            dimension_semantics=("parallel", "parallel", "arbitrary"), vmem_limit_bytes=VMEM_LIMIT),
        name="matmul",
    )(*args)
    return out[0] if squeeze else out


def _lora_body(x_ref, a_ref, b_ref, bias_ref, o_ref):
    n = pl.program_id(0)
    mid = _dot(x_ref[...], a_ref[...])
    mid = jnp.where(n == 0, jnp.tanh(mid), mid)
    o_ref[...] = _dot(mid.astype(BF16), b_ref[...]) + bias_ref[...]


def lora(mixed, idx, a_w, b_w, bias, tm=1024):
    L, D, R = a_w.shape
    N = mixed.shape[1]
    tm = min(tm, N)

    def pick(n):
        out = jnp.int32(idx[-1])
        for t in range(L - 2, -1, -1):
            out = jnp.where(n == t, jnp.int32(idx[t]), out)
        return out

    return pl.pallas_call(
        _lora_body,
        out_shape=jax.ShapeDtypeStruct((L, N, D), F32),
        grid=(L, N // tm),
        in_specs=[pl.BlockSpec((None, tm, D), lambda n, i: (pick(n), i, 0)),
                  pl.BlockSpec((None, D, R), lambda n, i: (n, 0, 0)),
                  pl.BlockSpec((None, R, D), lambda n, i: (n, 0, 0)),
                  pl.BlockSpec((None, 1, D), lambda n, i: (n, 0, 0))],
        out_specs=pl.BlockSpec((None, tm, D), lambda n, i: (n, i, 0)),
        compiler_params=pltpu.CompilerParams(
            dimension_semantics=("parallel", "arbitrary"), vmem_limit_bytes=VMEM_LIMIT),
        name="lora",
    )(mixed, a_w, b_w, bias)


def _expand(x, lane_head):
    return jnp.concatenate([jnp.where(lane_head == h, x, 0.0) for h in range(UNIT_HEADS)], axis=0)


def _compact(xe):
    out = xe[0:CHUNK]
    for h in range(1, UNIT_HEADS):
        out = out + xe[h * CHUNK:(h + 1) * CHUNK]
    return out


def _segsum(x, ones_bd):
    h, m, l = _split3(x)
    return _dot(h, ones_bd) + _dot(m, ones_bd) + _dot(l, ones_bd)


def _wkv_unit_stages(r, k, v, z, wl, al, vl, vf, k_k, k_a, r_k, ln_g, ln_b, s_ref, o_ref, consts):
    tri, ones_bd, same, strict, incl, lane_head, eye = consts
    bf = lambda x: x.astype(BF16)
    ex = lambda x: _expand(x, lane_head)

    a = _sigmoid(al[...])
    lw = -EXP_M05 * _sigmoid(wl[...])
    v = v[...]
    if vl is not None:
        v = v + (vf[...] - v) * _sigmoid(vl[...])
    k = k[...]
    kk = k * k_k[...]
    n2 = _segsum(kk * kk, ones_bd)
    lh, lm, ll = _split3(lw)
    cum = _dot(tri, lh) + _dot(tri, lm) + _dot(tri, ll)
    yield

    kk = kk / jnp.maximum(jnp.sqrt(n2), 1e-12)
    k2 = k * (1.0 + (a - 1.0) * k_a[...])
    p_incl = jnp.exp(cum)
    p_inv = jnp.exp(-cum)
    r = r[...]
    kt = k2 * p_inv
    bt = kk * a * p_inv
    ea = bf(ex(-kk * jnp.exp(cum - lw)))
    er32 = ex(r * p_incl)
    er = bf(er32)
    eb = bf(ex(bt))
    ek = bf(ex(kt))
    ev = bf(ex(v))
    yield

    a_ab = jnp.where(strict, _dot_nt(ea, eb), 0.0)
    yield
    a_ak = bf(jnp.where(strict, _dot_nt(ea, ek), 0.0))
    yield
    a_rb = bf(jnp.where(incl, _dot_nt(er, eb), 0.0))
    yield
    a_rk = bf(jnp.where(incl, _dot_nt(er, ek), 0.0))
    yield
    akv = bf(_dot(a_ak, ev))
    yield

    tm = eye + a_ab
    pw = bf(a_ab)
    for _ in range(5):
        pw = bf(_dot(pw, pw))
        yield
        tm = tm + _dot(pw, bf(tm))
        yield
    tb = bf(tm)

    u0 = _dot(tb, akv)
    yield
    wt = _dot(tb, ea)
    yield
    o0 = _dot(a_rb, bf(u0)) + _dot(a_rk, ev)
    yield
    qt = er32 + _dot(a_rb, bf(wt))
    yield

    s = s_ref[...]
    qw = _dot_nt(bf(jnp.concatenate([_compact(qt), _compact(wt)], axis=0)), bf(s))
    yield
    o = qw[0:CHUNK] + _compact(o0)
    u = _compact(u0) + qw[CHUNK:2 * CHUNK]
    upd = _dot_tn(bf(jnp.concatenate([v, u], axis=0)), bf(jnp.concatenate([kt, bt], axis=0)))
    yield
    s_ref[...] = (s + jnp.where(same, upd, 0.0)) * p_incl[CHUNK - 1:CHUNK, :]
    mean = _segsum(o, ones_bd) * (1.0 / HEAD)
    yield
    bonus = _segsum(r * k2 * r_k[...], ones_bd)
    yield

    oc = o - mean
    var = _segsum(oc * oc, ones_bd) * (1.0 / HEAD)
    yield

    o_ref[...] = ((oc * lax.rsqrt(var + GN_EPS) * ln_g[...] + ln_b[...] + bonus * v)
                  * _silu(z[...])).astype(o_ref.dtype)


def _run_skewed(gens, group, lag):
    done = [False] * len(gens)
    rnd = 0
    while not all(done):
        for u, g in enumerate(gens):
            if (u // group) * lag <= rnd and not done[u]:
                try:
                    next(g)
                except StopIteration:
                    done[u] = True
        rnd += 1


def _wkv_body(*refs, has_vres, units):
    if has_vres:
        (r_ref, k_ref, v_ref, z_ref, wl_ref, al_ref, vl_ref, vf_ref,
         kk_ref, ka_ref, rk_ref, lg_ref, lb_ref, o_ref, s_ref) = refs
    else:
        (r_ref, k_ref, v_ref, z_ref, wl_ref, al_ref,
         kk_ref, ka_ref, rk_ref, lg_ref, lb_ref, o_ref, s_ref) = refs
        vl_ref = vf_ref = None

    @pl.when(pl.program_id(2) == 0)
    def _():
        s_ref[...] = jnp.zeros_like(s_ref)

    row = lax.broadcasted_iota(jnp.int32, (UNIT, UNIT), 0)
    col = lax.broadcasted_iota(jnp.int32, (UNIT, UNIT), 1)
    same = (row >> 6) == (col >> 6)
    strict = same & ((col & 63) < (row & 63))
    incl = same & ((col & 63) <= (row & 63))
    eye = jnp.where(row == col, 1.0, 0.0).astype(F32)
    ones_bd = jnp.where(same, 1.0, 0.0).astype(BF16)
    tr = lax.broadcasted_iota(jnp.int32, (CHUNK, CHUNK), 0)
    tc = lax.broadcasted_iota(jnp.int32, (CHUNK, CHUNK), 1)
    tri = jnp.where(tc <= tr, 1.0, 0.0).astype(BF16)
    lane_head = lax.broadcasted_iota(jnp.int32, (CHUNK, UNIT), 1) >> 6
    consts = (tri, ones_bd, same, strict, incl, lane_head, eye)

    def view(ref, uu):
        return None if ref is None else ref.at[:, pl.ds(uu * UNIT, UNIT)]

    _run_skewed([
        _wkv_unit_stages(*(view(ref, uu) for ref in (r_ref, k_ref, v_ref, z_ref, wl_ref, al_ref, vl_ref, vf_ref,
                                                     kk_ref, ka_ref, rk_ref, lg_ref, lb_ref)),
                         s_ref.at[uu], view(o_ref, uu), consts)
        for uu in range(units)], WKV_GROUP, WKV_LAG)


def wkv(rkvz, lor, v_first, k_k, k_a, r_k, ln_g, ln_b, batch, units=8):
    _, N, D = rkvz.shape
    T = N // batch
    nc = T // CHUNK
    W = units * UNIT
    has_vres = v_first is not None
    row_map = lambda b, u, c: (b * nc + c, u)

    def lead(n):
        return pl.BlockSpec((None, CHUNK, W), lambda b, u, c: (n, b * nc + c, u))

    in_specs = [lead(0), lead(1), lead(2), lead(3), lead(0), lead(1)]
    args = [rkvz, rkvz, rkvz, rkvz, lor, lor]
    if has_vres:
        in_specs += [lead(2), lead(2)]
        args += [lor, v_first]
    par = pl.BlockSpec((1, W), lambda b, u, c: (0, u))
    in_specs += [par] * 5
    args += [k_k.reshape(1, D), k_a.reshape(1, D), r_k.reshape(1, D), ln_g.reshape(1, D), ln_b.reshape(1, D)]
    return pl.pallas_call(
        functools.partial(_wkv_body, has_vres=has_vres, units=units),
        out_shape=jax.ShapeDtypeStruct((N, D), BF16),
        grid=(batch, D // W, nc),
        in_specs=in_specs,
        out_specs=pl.BlockSpec((CHUNK, W), row_map),
        scratch_shapes=[pltpu.VMEM((units, UNIT, UNIT), F32)],
        compiler_params=pltpu.CompilerParams(
            dimension_semantics=("parallel", "parallel", "arbitrary"), vmem_limit_bytes=VMEM_LIMIT),
        name="wkv7",
    )(*args)


def _attend(s, v_aug):
    m = jnp.max(s, axis=1, keepdims=True)
    acc = _dot(jnp.exp2(s - m).astype(BF16), v_aug)
    return acc[:, :HD] * (1.0 / acc[:, HD:HD + 1])


def _rope(x, c, s1, s2):
    n = x.shape[1] // HD
    if n > 1:
        c, s1, s2 = (jnp.concatenate([t] * n, axis=1) for t in (c, s1, s2))
    half = ROPE_DIM // 2
    return x * c + pltpu.roll(x, x.shape[1] - half, 1) * s1 + pltpu.roll(x, half, 1) * s2


def _tile_rows(t):
    return pl.ds(pl.multiple_of(t * SEL_TILE, SEL_TILE), SEL_TILE)


def _nsa_scores(c, t, dst):
    dst[...] = _dot_nt(c.qa[...], c.ks[_tile_rows(t), :])


def _nsa_absorb(c, t, s):
    m = c.m[...]
    m_new = jnp.maximum(m, jnp.max(s, axis=1, keepdims=True))
    pv = _dot(jnp.exp2(s - m_new).astype(BF16), c.vs[_tile_rows(t), :])
    c.acc[...] = jnp.exp2(m - m_new) * c.acc[...] + pv
    c.m[...] = m_new


def _nsa_pair(c, t2):
    t = 2 * t2
    _nsa_scores(c, t + 1, c.s1)
    yield
    _nsa_absorb(c, t, c.s0[...])
    yield
    _nsa_scores(c, t + 2, c.s0)
    yield
    _nsa_absorb(c, t + 1, c.s1[...])
    yield


def _nsa_odd(c, t_last):
    _nsa_scores(c, t_last, c.s1)
    yield
    _nsa_absorb(c, t_last - 1, c.s0[...])
    yield


def _nsa_head(c, i, tq_col, tabs, ov_t):
    rows = NSA_REP * QB
    q4 = _rope(c.q[...], *tabs) * (HD ** -0.5 * LOG2E)
    qs = jnp.concatenate([q4[:, r * HD:(r + 1) * HD] for r in range(NSA_REP)], axis=0)
    qb = qs.astype(BF16)

    ncmp = c.kc.shape[0]
    band = WIN_BAND
    w0 = pl.multiple_of(jnp.maximum((i + 1) * QB - band, 0), QB)
    s_cmp = _dot_nt(qb, c.kc[...])
    yield
    s_win = _dot_nt(qb, c.kw[pl.ds(w0, band), :])
    yield

    cend = lax.broadcasted_iota(jnp.int32, (rows, ncmp), 1) * CMP_STRIDE + (CMP_BLOCK - 1)
    s_cmp = jnp.where(cend <= tq_col, s_cmp, NEG)
    e = jnp.exp2(s_cmp - jnp.max(s_cmp, axis=1, keepdims=True))
    p = e * (jnp.where(tq_col >= CMP_BLOCK - 1, 1.0, 0.0) / jnp.sum(e, axis=1, keepdims=True))
    c.o_cmp = _dot(p.astype(BF16), c.vc[...])
    yield

    psum = p[0:QB]
    for r in range(1, NSA_REP):
        psum = psum + p[r * QB:(r + 1) * QB]
    ph, pm, pl_ = _split3(psum)
    n_blk = ov_t.shape[0]
    imp_t = _dot_nt(ov_t, ph) + _dot_nt(ov_t, pm) + _dot_nt(ov_t, pl_)
    yield

    diff = (tq_col - w0) - lax.broadcasted_iota(jnp.int32, (rows, band), 1)
    s_win = jnp.where(lax.bitcast_convert_type(diff, jnp.uint32) < jnp.uint32(WINDOW), s_win, NEG)
    c.o_win = _attend(s_win, c.vw[pl.ds(w0, band), :])
    yield

    jsub = lax.broadcasted_iota(jnp.int32, (n_blk, QB), 0)
    cur = (i * QB + lax.broadcasted_iota(jnp.int32, (n_blk, QB), 1)) >> (SEL_BLOCK.bit_length() - 1)
    forced = (jsub == 0) | (jsub == cur) | (jsub == cur - 1)
    imp_t = jnp.where(forced, FORCE_BONUS, jnp.where(jsub > cur, NEG, imp_t))
    sub8 = lax.broadcasted_iota(jnp.int32, (8, QB), 0)
    grp = [imp_t[8 * a:8 * a + 8] for a in range(n_blk // 8)]
    cnt = [jnp.zeros((8, QB), F32) for _ in grp]
    for jp in range(n_blk):
        rowv = jnp.broadcast_to(imp_t[jp:jp + 1, :], (8, QB))
        for a, x in enumerate(grp):
            ge = lambda: jnp.where(rowv >= x, 1.0, 0.0)
            gt = lambda: jnp.where(rowv > x, 1.0, 0.0)
            if 8 * a > jp:
                inc = ge()
            elif 8 * a + 7 <= jp:
                inc = gt()
            else:
                inc = jnp.where(sub8 + 8 * a > jp, ge(), gt())
            cnt[a] = cnt[a] + inc
    cnt = jnp.concatenate(cnt, axis=0)
    bias_t = jnp.where(cnt < float(min(SEL_TOP_N, n_blk)), 0.0, NEG).astype(BF16)
    qi = lax.broadcasted_iota(jnp.int32, (QB, QB), 0)
    qj = lax.broadcasted_iota(jnp.int32, (QB, QB), 1)
    eye_q = jnp.where(qi == qj, 1.0, 0.0).astype(BF16)
    bias = _dot_nt(eye_q, bias_t)
    yield
    bias4 = jnp.concatenate([bias] * NSA_REP, axis=0).astype(BF16)
    pad = c.ks.shape[1] - HD - n_blk
    parts = [qb, bias4] + ([jnp.zeros((rows, pad), BF16)] if pad else [])
    qa = jnp.concatenate(parts, axis=1)

    c.qa[...] = qa
    c.s0[...] = _dot_nt(qa, c.ks[_tile_rows(0), :])
    c.m[...] = jnp.full(c.m.shape, NEG, F32)
    c.acc[...] = jnp.zeros(c.acc.shape, F32)
    yield


def _nsa_tail(c, t_last, odd, tq_col):
    rows = NSA_REP * QB
    kpos = t_last * SEL_TILE + lax.broadcasted_iota(jnp.int32, (rows, SEL_TILE), 1)
    s_diag = jnp.where(odd, c.s1[...], c.s0[...])
    _nsa_absorb(c, t_last, jnp.where(kpos <= tq_col, s_diag, NEG))
    yield
    acc = c.acc[...]
    o_sel = acc[:, :HD] * (1.0 / acc[:, HD:HD + 1])
    g = _sigmoid(c.g[...] + c.gb[...])
    outs = []
    for r in range(NSA_REP):
        rs = slice(r * QB, (r + 1) * QB)
        outs.append(g[:, 3 * r:3 * r + 1] * c.o_cmp[rs] + g[:, 3 * r + 1:3 * r + 2] * o_sel[rs]
                    + g[:, 3 * r + 2:3 * r + 3] * c.o_win[rs])
    o4 = jnp.concatenate(outs, axis=1)
    c.o[...] = (o4 * _silu(c.z[...])).astype(c.o.dtype)


def _nsa_body(q_ref, z_ref, g_ref, gb_ref, rc_ref, rs1_ref, rs2_ref, ovt_ref,
              kc_ref, vc_ref, ks_ref, vs_ref, kw_ref, vw_ref, o_ref, qa_scr, s0_scr, s1_scr, m_scr, acc_scr):
    i = pl.program_id(2)
    rows = NSA_REP * QB
    W = NSA_REP * HD
    n_grp = kc_ref.shape[0]
    grps = []
    for g in range(n_grp):
        lanes = pl.ds(g * W, W)
        glanes = pl.ds(g * 128, 128)
        grps.append(types.SimpleNamespace(
            q=q_ref.at[:, lanes], z=z_ref.at[:, lanes], g=g_ref.at[:, glanes], gb=gb_ref.at[:, glanes],
            kc=kc_ref.at[g], vc=vc_ref.at[g], ks=ks_ref.at[g], vs=vs_ref.at[g], kw=kw_ref.at[g], vw=vw_ref.at[g],
            o=o_ref.at[:, lanes], qa=qa_scr.at[g], s0=s0_scr.at[g], s1=s1_scr.at[g], m=m_scr.at[g], acc=acc_scr.at[g]))
    lock = lambda gens: _run_skewed(gens, n_grp, 0)

    tq_col = i * QB + (lax.broadcasted_iota(jnp.int32, (rows, 1), 0) & (QB - 1))
    tabs = (rc_ref[...], rs1_ref[...], rs2_ref[...])
    lock([_nsa_head(c, i, tq_col, tabs, ovt_ref[...]) for c in grps])

    t_last = (i * QB) // SEL_TILE
    odd = (t_last & 1) == 1

    def pair(t2, carry):
        lock([_nsa_pair(c, t2) for c in grps])
        return carry

    lax.fori_loop(0, t_last // 2, pair, 0)

    @pl.when(odd)
    def _():
        lock([_nsa_odd(c, t_last) for c in grps])

    lock([_nsa_tail(c, t_last, odd, tq_col) for c in grps])


def nsa_attention(proj, gate_b, rope_q, overlap, shared, batch):
    N = proj.shape[0]
    T = N // batch
    nq = T // QB
    W = NSA_REP * HD
    k_cmp, v_cmp, k_sel, v_sel, k_win, v_win = shared
    rc, rs1, rs2 = rope_q
    ncmp = k_cmp.shape[2]
    n_blk = T // SEL_BLOCK
    P = NSA_GROUPS_PER_STEP
    steps = NSA_GROUPS // P
    rows = NSA_REP * QB
    row_blk = lambda b, g, i: (b * nq + i, g)
    kv = lambda w: pl.BlockSpec((None, P, T, w), lambda b, g, i: (b, g, 0, 0))
    cm = pl.BlockSpec((None, P, ncmp, HD), lambda b, g, i: (b, g, 0, 0))
    rope = pl.BlockSpec((QB, HD), lambda b, g, i: (i, 0))
    gate_blk0 = 2 * NSA_HEADS * HD // (P * 128)
    return pl.pallas_call(
        _nsa_body,
        out_shape=jax.ShapeDtypeStruct((N, NSA_HEADS * HD), BF16),
        grid=(batch, steps, nq),
        in_specs=[pl.BlockSpec((QB, P * W), row_blk),
                  pl.BlockSpec((QB, P * W), lambda b, g, i: (b * nq + i, steps + g)),
                  pl.BlockSpec((QB, P * 128), lambda b, g, i: (b * nq + i, gate_blk0 + g)),
                  pl.BlockSpec((1, P * 128), lambda b, g, i: (0, g)),
                  rope, rope, rope,
                  pl.BlockSpec((n_blk, ncmp), lambda b, g, i: (0, 0)),
                  cm, cm, kv(k_sel.shape[3]), kv(v_sel.shape[3]), kv(k_win.shape[3]), kv(v_win.shape[3])],
        out_specs=pl.BlockSpec((QB, P * W), row_blk),
        scratch_shapes=[pltpu.VMEM((P, rows, k_sel.shape[3]), BF16),
                        pltpu.VMEM((P, rows, SEL_TILE), F32),
                        pltpu.VMEM((P, rows, SEL_TILE), F32),
                        pltpu.VMEM((P, rows, 1), F32),
                        pltpu.VMEM((P, rows, v_sel.shape[3]), F32)],
        compiler_params=pltpu.CompilerParams(
            dimension_semantics=("parallel", "parallel", "arbitrary"), vmem_limit_bytes=VMEM_LIMIT),
        name="nsa_attention",
    )(proj, proj, proj, gate_b, rc, rs1, rs2, overlap, k_cmp, v_cmp, k_sel, v_sel, k_win, v_win)


def _rms(x, g):
    return x * lax.rsqrt(jnp.mean(x * x, axis=-1, keepdims=True) + NORM_EPS) * g


def _norm_body(x_ref, g_ref, o_ref):
    o_ref[...] = _rms(x_ref[...], g_ref[...]).astype(o_ref.dtype)


def rmsnorm(h, g, out_dtype, tm=512):
    N, D = h.shape
    return pl.pallas_call(
        _norm_body,
        out_shape=jax.ShapeDtypeStruct((N, D), out_dtype),
        grid=(N // tm,),
        in_specs=[pl.BlockSpec((tm, D), lambda i: (i, 0)), pl.BlockSpec((1, D), lambda i: (0, 0))],
        out_specs=pl.BlockSpec((tm, D), lambda i: (i, 0)),
        compiler_params=pltpu.CompilerParams(dimension_semantics=("parallel",)),
        name="rmsnorm",
    )(h, g.reshape(1, D))


def _mix_body(x_ref, xp_ref, g_ref, mu_ref, o_ref, *, seq):
    tm = x_ref.shape[0]
    g = g_ref[...]
    u = _rms(x_ref[...], g)
    last_prev = _rms(xp_ref[...], g)[xp_ref.shape[0] - 1:, :]
    first = lax.rem(pl.program_id(0) * tm, seq) == 0
    row = lax.broadcasted_iota(jnp.int32, (tm, 1), 0)
    prev = jnp.where(row == 0, jnp.where(first, 0.0, last_prev), pltpu.roll(u, 1, 0))
    xx = prev - u
    for n in range(o_ref.shape[0]):
        o_ref[n] = (u + xx * mu_ref[n:n + 1, :]).astype(o_ref.dtype)


def rwkv_mix(h, g, mu, seq, tm=256):
    N, D = h.shape
    n_mix = mu.shape[0]
    assert seq % tm == 0
    sub = 8
    return pl.pallas_call(
        functools.partial(_mix_body, seq=seq),
        out_shape=jax.ShapeDtypeStruct((n_mix, N, D), BF16),
        grid=(N // tm,),
        in_specs=[pl.BlockSpec((tm, D), lambda i: (i, 0)),
                  pl.BlockSpec((sub, D), lambda i: (jnp.maximum(i * (tm // sub) - 1, 0), 0)),
                  pl.BlockSpec((1, D), lambda i: (0, 0)),
                  pl.BlockSpec((n_mix, D), lambda i: (0, 0))],
        out_specs=pl.BlockSpec((n_mix, tm, D), lambda i: (0, i, 0)),
        compiler_params=pltpu.CompilerParams(dimension_semantics=("parallel",), vmem_limit_bytes=VMEM_LIMIT),
        name="rwkv_mix",
    )(h, h, g.reshape(1, D), mu)


def _kvpost_body(kv_ref, rc_ref, rs1_ref, rs2_ref, pek_ref, pev_ref, cx_ref, ks_ref, vs_ref, kw_ref, vw_ref):
    tt = kv_ref.shape[0]
    aug = ks_ref.shape[-1] - HD
    tok = pl.program_id(1) * tt + lax.broadcasted_iota(jnp.int32, (tt, aug), 0)
    lane = lax.broadcasted_iota(jnp.int32, (tt, aug), 1)
    onehot = jnp.where(lane == (tok >> (SEL_BLOCK.bit_length() - 1)), 1.0, 0.0).astype(BF16)
    ones_col = jnp.where(lax.broadcasted_iota(jnp.int32, (tt, vs_ref.shape[-1] - HD), 1) == 0, 1.0, 0.0).astype(BF16)
    tabs = (rc_ref[...], rs1_ref[...], rs2_ref[...])
    G = NSA_GROUPS
    for g in range(G):
        k_c, v_c, k_s, v_s, k_w, v_w = [kv_ref[:, (j * G + g) * HD:(j * G + g + 1) * HD] for j in range(6)]
        cx_ref[0, g] = (k_c + pek_ref[0]).astype(BF16)
        cx_ref[1, g] = (v_c + pev_ref[0]).astype(BF16)
        cx_ref[2, g] = (k_c + pek_ref[1]).astype(BF16)
        cx_ref[3, g] = (v_c + pev_ref[1]).astype(BF16)
        ks_ref[g] = jnp.concatenate([_rope(k_s, *tabs).astype(BF16), onehot], axis=1)
        vs_ref[g] = jnp.concatenate([v_s.astype(BF16), ones_col], axis=1)
        kw_ref[g] = _rope(k_w, *tabs).astype(BF16)
        vw_ref[g] = jnp.concatenate([v_w.astype(BF16), ones_col], axis=1)


def kv_post(kv, rope_t, cmp_pe, batch, tt=512):
    N = kv.shape[0]
    T = N // batch
    G = NSA_GROUPS
    nt = T // tt
    n_blk = T // SEL_BLOCK
    aug = -(-n_blk // 128) * 128
    reps = tt // CMP_STRIDE
    pe = jnp.stack([jnp.tile(cmp_pe[:, :CMP_STRIDE], (1, reps, 1)), jnp.tile(cmp_pe[:, CMP_STRIDE:], (1, reps, 1))], 1)
    tab = pl.BlockSpec((tt, HD), lambda b, i: (i, 0))
    pes = pl.BlockSpec((2, tt, HD), lambda b, i: (0, 0, 0))
    out4 = lambda w: pl.BlockSpec((None, G, tt, w), lambda b, i: (b, 0, i, 0))
    sds = lambda w: jax.ShapeDtypeStruct((batch, G, T, w), BF16)
    return pl.pallas_call(
        _kvpost_body,
        out_shape=(jax.ShapeDtypeStruct((4, batch, G, T, HD), BF16), sds(HD + aug), sds(2 * HD), sds(HD), sds(2 * HD)),
        grid=(batch, nt),
        in_specs=[pl.BlockSpec((tt, kv.shape[1]), lambda b, i: (b * nt + i, 0)), tab, tab, tab, pes, pes],
        out_specs=(pl.BlockSpec((4, None, G, tt, HD), lambda b, i: (0, b, 0, i, 0)),
                   out4(HD + aug), out4(2 * HD), out4(HD), out4(2 * HD)),
        compiler_params=pltpu.CompilerParams(dimension_semantics=("parallel", "parallel"), vmem_limit_bytes=VMEM_LIMIT),
        name="kv_post",
    )(kv, *rope_t, pe[0], pe[1])


def _cmp2_body(lo_ref, hi_ref, w2_ref, rc_ref, rs1_ref, rs2_ref, o_ref):
    n = lo_ref.shape[0]
    hid = lo_ref[...] + pltpu.roll(hi_ref[...], n - 1, 0)
    out = _dot(_silu(hid).astype(BF16), w2_ref[...])
    roped = _rope(out, rc_ref[...], rs1_ref[...], rs2_ref[...])
    o_ref[...] = jnp.where(pl.program_id(0) == 0, roped, out).astype(o_ref.dtype)


def cmp_mlp2(hd, w2, rope_c, groups, ng):
    H = hd.shape[2]
    tab = pl.BlockSpec((ng, HD), lambda j, m: (0, 0))
    return pl.pallas_call(
        _cmp2_body,
        out_shape=jax.ShapeDtypeStruct((2, groups, ng, HD), BF16),
        grid=(2, groups),
        in_specs=[pl.BlockSpec((None, ng, H), lambda j, m: (j, m, 0)),
                  pl.BlockSpec((None, ng, H), lambda j, m: (2 + j, m, 0)),
                  pl.BlockSpec((None, H, HD), lambda j, m: (j, 0, 0)), tab, tab, tab],
        out_specs=pl.BlockSpec((None, None, ng, HD), lambda j, m: (j, m, 0, 0)),
        compiler_params=pltpu.CompilerParams(dimension_semantics=("parallel", "parallel")),
        name="cmp_mlp2",
    )(hd, hd, w2, *rope_c)


def _rope_tables(pos):
    half = ROPE_DIM // 2
    inv = ROPE_THETA ** (-jnp.arange(half, dtype=F32) / half)
    ang = pos.astype(F32)[:, None] * inv[None, :]
    cos, sin = jnp.cos(ang), jnp.sin(ang)
    n = pos.shape[0]
    c = jnp.concatenate([cos, cos, jnp.ones((n, HD - ROPE_DIM), F32)], axis=1)
    s1 = jnp.concatenate([-sin, jnp.zeros((n, HD - half), F32)], axis=1)
    s2 = jnp.concatenate([jnp.zeros((n, half), F32), sin, jnp.zeros((n, HD - ROPE_DIM), F32)], axis=1)
    return c, s1, s2


def _rwkv_layer(h, batch, v_first, norm_g, mu, w_rkvz, w0, w1, w2, a0, a1, a2, vres, k_k, k_a, r_k, ln_g, ln_b, w_o):
    N, D = h.shape
    mixed = rwkv_mix(h, norm_g, mu, N // batch)
    rkvz = matmul(mixed, w_rkvz.astype(BF16))

    loras = [(w1, w2, w0), (a1, a2, a0)]
    idx = [4, 5]
    if vres is not None:
        loras.append((vres[1], vres[2], vres[0]))
        idx.append(2)
    rpad = 128
    a_w = jnp.stack([jnp.pad(l[0], ((0, 0), (0, rpad - l[0].shape[1]))) for l in loras]).astype(BF16)
    b_w = jnp.stack([jnp.pad(l[1], ((0, rpad - l[1].shape[0]), (0, 0))) for l in loras]).astype(BF16)
    bias = jnp.stack([l[2].reshape(1, D) for l in loras])
    lor = lora(mixed, idx, a_w, b_w, bias)

    o = wkv(rkvz, lor, v_first, k_k, k_a, r_k.reshape(-1), ln_g, ln_b, batch)
    return matmul(o, w_o.astype(BF16), res=h), rkvz


def _shared_kv(h, batch, rope_t, kv_norm_g, kv_w, cmp_pe, cmp_w1, cmp_w2):
    N, D = h.shape
    T = N // batch
    G = NSA_GROUPS
    kv = matmul(rmsnorm(h, kv_norm_g, BF16), kv_w.astype(BF16))
    cx, k_sel, v_sel, k_win, v_win = kv_post(kv, rope_t, cmp_pe, batch)

    ng = T // CMP_STRIDE
    half = CMP_STRIDE * HD
    w1 = jnp.concatenate([cmp_w1[:, :half], cmp_w1[:, half:]], axis=0).astype(BF16)
    hd = matmul(cx.reshape(4, batch * G * ng, half), w1)
    rope_c = _rope_tables(jnp.arange(ng) * CMP_STRIDE + CMP_BLOCK - 1)
    cmp = cmp_mlp2(hd, cmp_w2.astype(BF16), rope_c, batch * G, ng).reshape(2, batch, G, ng, HD)
    return cmp[0], cmp[1], k_sel, v_sel, k_win, v_win


def _nsa_layer(h, batch, shared, rope_t, overlap, norm_g, w_in, gate_b, w_o):
    N, D = h.shape
    G = NSA_GROUPS
    width = NSA_HEADS * HD
    ng = N_BRANCH * NSA_HEADS
    per = N_BRANCH * NSA_REP
    w_g = jnp.pad(w_in[:, width:width + ng].reshape(D, G, per), ((0, 0), (0, 0), (0, 128 - per))).reshape(D, G * 128)
    w_all = jnp.concatenate([w_in[:, :width], w_in[:, width + ng:], w_g], axis=1).astype(BF16)
    b_g = jnp.pad(gate_b.reshape(G, per), ((0, 0), (0, 128 - per))).reshape(1, G * 128)
    proj = matmul(rmsnorm(h, norm_g, BF16), w_all, tn=768)
    o = nsa_attention(proj, b_g, rope_t, overlap, shared, batch)
    return matmul(o, w_o.astype(BF16), res=h)


def kernel(x, a_norm_g, a_mu, a_w_rkvz, a_w0, a_w1, a_w2, a_a0, a_a1, a_a2, a_v0, a_v1, a_v2, a_k_k, a_k_a, a_r_k,
           a_ln_g, a_ln_b, a_w_o, kv_norm_g, kv_w, cmp_pe, cmp_w1, cmp_w2, b_norm_g, b_w_in, b_gate_b, b_w_o, final_g):
    B, T, D = x.shape
    N = B * T
    n_a = a_norm_g.shape[0]
    n_b = b_norm_g.shape[0]
    h = x.reshape(N, D)
    v_first = None
    for i in range(n_a):
        vres = None if i == 0 else (a_v0[i - 1], a_v1[i - 1], a_v2[i - 1])
        h, rkvz = _rwkv_layer(h, B, v_first, a_norm_g[i], a_mu[i], a_w_rkvz[i], a_w0[i], a_w1[i], a_w2[i],
                              a_a0[i], a_a1[i], a_a2[i], vres, a_k_k[i], a_k_a[i], a_r_k[i],
                              a_ln_g[i], a_ln_b[i], a_w_o[i])
        if i == 0:
            v_first = rkvz
    if n_b:
        rope_t = _rope_tables(jnp.arange(T))
        shared = _shared_kv(h, B, rope_t, kv_norm_g, kv_w, cmp_pe, cmp_w1, cmp_w2)
        ncmp = T // CMP_STRIDE
        n_blk = T // SEL_BLOCK
        cpos = jnp.arange(ncmp)[:, None] * CMP_STRIDE + jnp.arange(CMP_BLOCK)[None, :]
        overlap = jax.nn.one_hot(cpos // SEL_BLOCK, n_blk, dtype=F32).mean(axis=1).T.astype(BF16)
        for j in range(n_b):
            h = _nsa_layer(h, B, shared, rope_t, overlap, b_norm_g[j], b_w_in[j], b_gate_b[j], b_w_o[j])
    return rmsnorm(h, final_g, F32).reshape(B, T, D)
```

```python
import functools
import math
import types

import jax
import jax.numpy as jnp
from jax import lax
from jax.experimental import pallas as pl
from jax.experimental.pallas import tpu as pltpu

F32 = jnp.float32
BF16 = jnp.bfloat16

NORM_EPS = 1e-6
GN_EPS = 64e-5
HEAD = 64
CHUNK = 64
UNIT = 256
UNIT_HEADS = UNIT // HEAD
WKV_CHUNKS = 2
WKV_GROUP = 4
WKV_LAG = 12
EXP_M05 = math.exp(-0.5)

NSA_HEADS = 16
NSA_GROUPS = 4
NSA_REP = NSA_HEADS // NSA_GROUPS
HD = 128
N_BRANCH = 3
CMP_BLOCK = 32
CMP_STRIDE = 16
SEL_BLOCK = 64
SEL_TOP_N = 16
WINDOW = 512
QB = 128
NSA_GROUPS_PER_STEP = 2
LOG2E = math.log2(math.e)
ROPE_DIM = HD // 4
ROPE_THETA = 500000.0
NEG = -1e30
FORCE_BONUS = 1e4
SEL_TILE = 512
WIN_BAND = WINDOW + 128

VMEM_LIMIT = 56 * 1024 * 1024


def _dot(a, b):
    return jnp.dot(a, b, preferred_element_type=F32)


def _dot_nt(a, b):
    return lax.dot_general(a, b, (((1,), (1,)), ((), ())), preferred_element_type=F32)


def _dot_tn(a, b):
    return lax.dot_general(a, b, (((0,), (0,)), ((), ())), preferred_element_type=F32)


def _split3(x):
    h = x.astype(BF16)
    r1 = x - h.astype(F32)
    m = r1.astype(BF16)
    l = (r1 - m.astype(F32)).astype(BF16)
    return h, m, l


def _sigmoid(x):
    return 1.0 / (1.0 + jnp.exp(-x))


def _silu(x):
    return x * _sigmoid(x)


def _mm_body(*refs, has_res):
    if has_res:
        a_ref, w_ref, r_ref, o_ref = refs
    else:
        a_ref, w_ref, o_ref = refs
    acc = _dot(a_ref[...].astype(BF16), w_ref[...].astype(BF16))
    if has_res:
        acc = acc + r_ref[...]
    o_ref[...] = acc.astype(o_ref.dtype)


def matmul(a, w, res=None, out_dtype=F32, tm=1024, tn=1024):
    squeeze = a.ndim == 2
    if squeeze:
        a, w = a[None], w[None]
        res = None if res is None else res[None]
    G, K, N = w.shape
    M = a.shape[1]
    tm = min(tm, M)
    tn = min(tn, N)
    assert M % tm == 0 and N % tn == 0, (M, N, tm, tn)
    in_specs = [pl.BlockSpec((None, tm, K), lambda g, i, j: (g, i, 0)),
                pl.BlockSpec((None, K, tn), lambda g, i, j: (g, 0, j))]
    args = [a, w]
    if res is not None:
        in_specs.append(pl.BlockSpec((None, tm, tn), lambda g, i, j: (g, i, j)))
        args.append(res)
    out = pl.pallas_call(
        functools.partial(_mm_body, has_res=res is not None),
        out_shape=jax.ShapeDtypeStruct((G, M, N), out_dtype),
        grid=(G, M // tm, N // tn),
        in_specs=in_specs,
        out_specs=pl.BlockSpec((None, tm, tn), lambda g, i, j: (g, i, j)),
        compiler_params=pltpu.CompilerParams(
            dimension_semantics=("parallel", "parallel", "arbitrary"), vmem_limit_bytes=VMEM_LIMIT),
        name="matmul",
    )(*args)
    return out[0] if squeeze else out


def _lora_body(x_ref, a_ref, b_ref, bias_ref, o_ref):
    n = pl.program_id(0)
    mid = _dot(x_ref[...], a_ref[...])
    mid = jnp.where(n == 0, jnp.tanh(mid), mid)
    o_ref[...] = _dot(mid.astype(BF16), b_ref[...]) + bias_ref[...]


def lora(mixed, idx, a_w, b_w, bias, tm=1024):
    L, D, R = a_w.shape
    N = mixed.shape[1]
    tm = min(tm, N)

    def pick(n):
        out = jnp.int32(idx[-1])
        for t in range(L - 2, -1, -1):
            out = jnp.where(n == t, jnp.int32(idx[t]), out)
        return out

    return pl.pallas_call(
        _lora_body,
        out_shape=jax.ShapeDtypeStruct((L, N, D), F32),
        grid=(L, N // tm),
        in_specs=[pl.BlockSpec((None, tm, D), lambda n, i: (pick(n), i, 0)),
                  pl.BlockSpec((None, D, R), lambda n, i: (n, 0, 0)),
                  pl.BlockSpec((None, R, D), lambda n, i: (n, 0, 0)),
                  pl.BlockSpec((None, 1, D), lambda n, i: (n, 0, 0))],
        out_specs=pl.BlockSpec((None, tm, D), lambda n, i: (n, i, 0)),
        compiler_params=pltpu.CompilerParams(
            dimension_semantics=("parallel", "arbitrary"), vmem_limit_bytes=VMEM_LIMIT),
        name="lora",
    )(mixed, a_w, b_w, bias)


def _expand(x, lane_head):
    return jnp.concatenate([jnp.where(lane_head == h, x, 0.0) for h in range(UNIT_HEADS)], axis=0)


def _compact(xe):
    out = xe[0:CHUNK]
    for h in range(1, UNIT_HEADS):
        out = out + xe[h * CHUNK:(h + 1) * CHUNK]
    return out


def _segsum(x, ones_bd):
    h = x.astype(BF16)
    m = (x - h.astype(F32)).astype(BF16)
    return _dot(h, ones_bd) + _dot(m, ones_bd)


def _wkv_unit_stages(r, k, v, z, wl, al, vl, vf, k_k, k_a, r_k, ln_g, ln_b, s_ref, o_ref, consts):
    tri, ones_bd, same, strict, incl, lane_head, eye = consts
    bf = lambda x: x.astype(BF16)
    ex = lambda x: _expand(x, lane_head)

    a = _sigmoid(al[...])
    lw = -EXP_M05 * _sigmoid(wl[...])
    v = v[...]
    if vl is not None:
        v = v + (vf[...] - v) * _sigmoid(vl[...])
    k = k[...]
    kk = k * k_k[...]
    n2 = _segsum(kk * kk, ones_bd)
    lh, lm, ll = _split3(lw)
    cum = _dot(tri, lh) + _dot(tri, lm) + _dot(tri, ll)
    yield

    kk = kk / jnp.maximum(jnp.sqrt(n2), 1e-12)
    k2 = k * (1.0 + (a - 1.0) * k_a[...])
    p_incl = jnp.exp(cum)
    p_inv = jnp.exp(-cum)
    r = r[...]
    kt = k2 * p_inv
    bt = kk * a * p_inv
    ea = bf(ex(-kk * jnp.exp(cum - lw)))
    er32 = ex(r * p_incl)
    er = bf(er32)
    eb = bf(ex(bt))
    ek = bf(ex(kt))
    ev = bf(ex(v))
    yield

    a_ab = jnp.where(strict, _dot_nt(ea, eb), 0.0)
    yield
    a_ak = bf(jnp.where(strict, _dot_nt(ea, ek), 0.0))
    yield
    a_rb = bf(jnp.where(incl, _dot_nt(er, eb), 0.0))
    yield
    a_rk = bf(jnp.where(incl, _dot_nt(er, ek), 0.0))
    yield
    akv = bf(_dot(a_ak, ev))
    yield

    tm = eye + a_ab
    pw = bf(a_ab)
    for _ in range(5):
        pw = bf(_dot(pw, pw))
        yield
        tm = tm + _dot(pw, bf(tm))
        yield
    tb = bf(tm)

    u0 = _dot(tb, akv)
    yield
    wt = _dot(tb, ea)
    yield
    o0 = _dot(a_rb, bf(u0)) + _dot(a_rk, ev)
    yield
    qt = er32 + _dot(a_rb, bf(wt))
    yield

    s = s_ref[...]
    qw = _dot_nt(bf(jnp.concatenate([_compact(qt), _compact(wt)], axis=0)), bf(s))
    yield
    o = qw[0:CHUNK] + _compact(o0)
    u = _compact(u0) + qw[CHUNK:2 * CHUNK]
    upd = _dot_tn(bf(jnp.concatenate([v, u], axis=0)), bf(jnp.concatenate([kt, bt], axis=0)))
    yield
    s_ref[...] = (s + jnp.where(same, upd, 0.0)) * p_incl[CHUNK - 1:CHUNK, :]
    mean = _segsum(o, ones_bd) * (1.0 / HEAD)
    yield
    bonus = _segsum(r * k2 * r_k[...], ones_bd)
    yield

    oc = o - mean
    var = _segsum(oc * oc, ones_bd) * (1.0 / HEAD)
    yield

    o_ref[...] = ((oc * lax.rsqrt(var + GN_EPS) * ln_g[...] + ln_b[...] + bonus * v)
                  * _silu(z[...])).astype(o_ref.dtype)


def _run_skewed(gens, group, lag):
    done = [False] * len(gens)
    rnd = 0
    while not all(done):
        for u, g in enumerate(gens):
            if (u // group) * lag <= rnd and not done[u]:
                try:
                    next(g)
                except StopIteration:
                    done[u] = True
        rnd += 1


def _wkv_body(*refs, has_vres, units):
    if has_vres:
        (r_ref, k_ref, v_ref, z_ref, wl_ref, al_ref, vl_ref, vf_ref,
         kk_ref, ka_ref, rk_ref, lg_ref, lb_ref, o_ref, s_ref) = refs
    else:
        (r_ref, k_ref, v_ref, z_ref, wl_ref, al_ref,
         kk_ref, ka_ref, rk_ref, lg_ref, lb_ref, o_ref, s_ref) = refs
        vl_ref = vf_ref = None

    @pl.when(pl.program_id(2) == 0)
    def _():
        s_ref[...] = jnp.zeros_like(s_ref)

    row = lax.broadcasted_iota(jnp.int32, (UNIT, UNIT), 0)
    col = lax.broadcasted_iota(jnp.int32, (UNIT, UNIT), 1)
    same = (row >> 6) == (col >> 6)
    strict = same & ((col & 63) < (row & 63))
    incl = same & ((col & 63) <= (row & 63))
    eye = jnp.where(row == col, 1.0, 0.0).astype(F32)
    ones_bd = jnp.where(same, 1.0, 0.0).astype(BF16)
    tr = lax.broadcasted_iota(jnp.int32, (CHUNK, CHUNK), 0)
    tc = lax.broadcasted_iota(jnp.int32, (CHUNK, CHUNK), 1)
    tri = jnp.where(tc <= tr, 1.0, 0.0).astype(BF16)
    lane_head = lax.broadcasted_iota(jnp.int32, (CHUNK, UNIT), 1) >> 6
    consts = (tri, ones_bd, same, strict, incl, lane_head, eye)

    def view(ref, cc, uu):
        if ref is None:
            return None
        rows = pl.ds(cc * CHUNK, CHUNK) if ref.shape[0] > 1 else slice(None)
        return ref.at[rows, pl.ds(uu * UNIT, UNIT)]

    _run_skewed([
        _wkv_unit_stages(*(view(ref, cc, uu) for ref in (r_ref, k_ref, v_ref, z_ref, wl_ref, al_ref, vl_ref, vf_ref,
                                                         kk_ref, ka_ref, rk_ref, lg_ref, lb_ref)),
                         s_ref.at[uu], view(o_ref, cc, uu), consts)
        for cc in range(WKV_CHUNKS) for uu in range(units)], WKV_GROUP, WKV_LAG)


def wkv(rkvz, lor, v_first, k_k, k_a, r_k, ln_g, ln_b, batch, units=8):
    _, N, D = rkvz.shape
    T = N // batch
    step_rows = WKV_CHUNKS * CHUNK
    nc = T // step_rows
    W = units * UNIT
    has_vres = v_first is not None
    row_map = lambda b, u, c: (b * nc + c, u)

    def lead(n):
        return pl.BlockSpec((None, step_rows, W), lambda b, u, c: (n, b * nc + c, u))

    in_specs = [lead(0), lead(1), lead(2), lead(3), lead(0), lead(1)]
    args = [rkvz, rkvz, rkvz, rkvz, lor, lor]
    if has_vres:
        in_specs += [lead(2), lead(2)]
        args += [lor, v_first]
    par = pl.BlockSpec((1, W), lambda b, u, c: (0, u))
    in_specs += [par] * 5
    args += [k_k.reshape(1, D), k_a.reshape(1, D), r_k.reshape(1, D), ln_g.reshape(1, D), ln_b.reshape(1, D)]
    return pl.pallas_call(
        functools.partial(_wkv_body, has_vres=has_vres, units=units),
        out_shape=jax.ShapeDtypeStruct((N, D), BF16),
        grid=(batch, D // W, nc),
        in_specs=in_specs,
        out_specs=pl.BlockSpec((step_rows, W), row_map),
        scratch_shapes=[pltpu.VMEM((units, UNIT, UNIT), F32)],
        compiler_params=pltpu.CompilerParams(
            dimension_semantics=("parallel", "parallel", "arbitrary"), vmem_limit_bytes=VMEM_LIMIT),
        name="wkv7",
    )(*args)


def _attend(s, v_aug):
    m = jnp.max(s, axis=1, keepdims=True)
    acc = _dot(jnp.exp2(s - m).astype(BF16), v_aug)
    return acc[:, :HD] * (1.0 / acc[:, HD:HD + 1])


def _rope(x, c, s1, s2):
    n = x.shape[1] // HD
    if n > 1:
        c, s1, s2 = (jnp.concatenate([t] * n, axis=1) for t in (c, s1, s2))
    half = ROPE_DIM // 2
    return x * c + pltpu.roll(x, x.shape[1] - half, 1) * s1 + pltpu.roll(x, half, 1) * s2


def _tile_rows(t):
    return pl.ds(pl.multiple_of(t * SEL_TILE, SEL_TILE), SEL_TILE)


def _nsa_scores(c, t, dst):
    dst[...] = _dot_nt(c.qa[...], c.ks[_tile_rows(t), :])


def _nsa_absorb(c, t, s):
    m = c.m[...]
    m_new = jnp.maximum(m, jnp.max(s, axis=1, keepdims=True))
    pv = _dot(jnp.exp2(s - m_new).astype(BF16), c.vs[_tile_rows(t), :])
    c.acc[...] = jnp.exp2(m - m_new) * c.acc[...] + pv
    c.m[...] = m_new


def _nsa_pair(c, t2):
    t = 2 * t2
    _nsa_scores(c, t + 1, c.s1)
    yield
    _nsa_absorb(c, t, c.s0[...])
    yield
    _nsa_scores(c, t + 2, c.s0)
    yield
    _nsa_absorb(c, t + 1, c.s1[...])
    yield


def _nsa_odd(c, t_last):
    _nsa_scores(c, t_last, c.s1)
    yield
    _nsa_absorb(c, t_last - 1, c.s0[...])
    yield


def _add_rows(s, bias):
    return jnp.concatenate([s[r * QB:(r + 1) * QB] + bias for r in range(NSA_REP)], axis=0)


def _nsa_head(c, i, tq_col, tabs, ov_t, cmp_bias, win_bias):
    rows = NSA_REP * QB
    q4 = _rope(c.q[...], *tabs) * (HD ** -0.5 * LOG2E)
    qs = jnp.concatenate([q4[:, r * HD:(r + 1) * HD] for r in range(NSA_REP)], axis=0)
    qb = qs.astype(BF16)

    band = WIN_BAND
    w0 = pl.multiple_of(jnp.maximum((i + 1) * QB - band, 0), QB)
    s_cmp = _dot_nt(qb, c.kc[...])
    yield
    s_win = _dot_nt(qb, c.kw[pl.ds(w0, band), :])
    yield

    s_cmp = _add_rows(s_cmp, cmp_bias)
    e = jnp.exp2(s_cmp - jnp.max(s_cmp, axis=1, keepdims=True))
    p = e * (jnp.where(tq_col >= CMP_BLOCK - 1, 1.0, 0.0) / jnp.sum(e, axis=1, keepdims=True))
    c.o_cmp = _dot(p.astype(BF16), c.vc[...])
    yield

    psum = p[0:QB]
    for r in range(1, NSA_REP):
        psum = psum + p[r * QB:(r + 1) * QB]
    ph, pm, pl_ = _split3(psum)
    n_blk = ov_t.shape[0]
    imp_t = _dot_nt(ov_t, ph) + _dot_nt(ov_t, pm) + _dot_nt(ov_t, pl_)
    yield

    c.o_win = _attend(_add_rows(s_win, win_bias), c.vw[pl.ds(w0, band), :])
    yield

    jsub = lax.broadcasted_iota(jnp.int32, (n_blk, QB), 0)
    cur = (i * QB + lax.broadcasted_iota(jnp.int32, (n_blk, QB), 1)) >> (SEL_BLOCK.bit_length() - 1)
    forced = (jsub == 0) | (jsub == cur) | (jsub == cur - 1)
    imp_t = jnp.where(forced, FORCE_BONUS, jnp.where(jsub > cur, NEG, imp_t))
    sub8 = lax.broadcasted_iota(jnp.int32, (8, QB), 0)
    grp = [imp_t[8 * a:8 * a + 8] for a in range(n_blk // 8)]
    cnt = [jnp.zeros((8, QB), F32) for _ in grp]
    for jp in range(n_blk):
        rowv = jnp.broadcast_to(imp_t[jp:jp + 1, :], (8, QB))
        for a, x in enumerate(grp):
            ge = lambda: jnp.where(rowv >= x, 1.0, 0.0)
            gt = lambda: jnp.where(rowv > x, 1.0, 0.0)
            if 8 * a > jp:
                inc = ge()
            elif 8 * a + 7 <= jp:
                inc = gt()
            else:
                inc = jnp.where(sub8 + 8 * a > jp, ge(), gt())
            cnt[a] = cnt[a] + inc
    cnt = jnp.concatenate(cnt, axis=0)
    bias_t = jnp.where(cnt < float(min(SEL_TOP_N, n_blk)), 0.0, NEG).astype(BF16)
    qi = lax.broadcasted_iota(jnp.int32, (QB, QB), 0)
    qj = lax.broadcasted_iota(jnp.int32, (QB, QB), 1)
    eye_q = jnp.where(qi == qj, 1.0, 0.0).astype(BF16)
    bias = _dot_nt(eye_q, bias_t)
    yield
    bias4 = jnp.concatenate([bias] * NSA_REP, axis=0).astype(BF16)
    pad = c.ks.shape[1] - HD - n_blk
    parts = [qb, bias4] + ([jnp.zeros((rows, pad), BF16)] if pad else [])
    qa = jnp.concatenate(parts, axis=1)

    c.qa[...] = qa
    c.s0[...] = _dot_nt(qa, c.ks[_tile_rows(0), :])
    c.m[...] = jnp.full(c.m.shape, NEG, F32)
    c.acc[...] = jnp.zeros(c.acc.shape, F32)
    yield


def _nsa_tail(c, t_last, odd, diag_bias):
    s_diag = jnp.where(odd, c.s1[...], c.s0[...])
    _nsa_absorb(c, t_last, _add_rows(s_diag, diag_bias))
    yield
    acc = c.acc[...]
    o_sel = acc[:, :HD] * (1.0 / acc[:, HD:HD + 1])
    g = _sigmoid(c.g[...] + c.gb[...])
    outs = []
    for r in range(NSA_REP):
        rs = slice(r * QB, (r + 1) * QB)
        outs.append(g[:, 3 * r:3 * r + 1] * c.o_cmp[rs] + g[:, 3 * r + 1:3 * r + 2] * o_sel[rs]
                    + g[:, 3 * r + 2:3 * r + 3] * c.o_win[rs])
    o4 = jnp.concatenate(outs, axis=1)
    c.o[...] = (o4 * _silu(c.z[...])).astype(c.o.dtype)


def _nsa_body(q_ref, z_ref, g_ref, gb_ref, rc_ref, rs1_ref, rs2_ref, ovt_ref,
              kc_ref, vc_ref, ks_ref, vs_ref, kw_ref, vw_ref, o_ref, qa_scr, s0_scr, s1_scr, m_scr, acc_scr):
    i = pl.program_id(2)
    rows = NSA_REP * QB
    W = NSA_REP * HD
    n_grp = kc_ref.shape[0]
    grps = []
    for g in range(n_grp):
        lanes = pl.ds(g * W, W)
        glanes = pl.ds(g * 128, 128)
        grps.append(types.SimpleNamespace(
            q=q_ref.at[:, lanes], z=z_ref.at[:, lanes], g=g_ref.at[:, glanes], gb=gb_ref.at[:, glanes],
            kc=kc_ref.at[g], vc=vc_ref.at[g], ks=ks_ref.at[g], vs=vs_ref.at[g], kw=kw_ref.at[g], vw=vw_ref.at[g],
            o=o_ref.at[:, lanes], qa=qa_scr.at[g], s0=s0_scr.at[g], s1=s1_scr.at[g], m=m_scr.at[g], acc=acc_scr.at[g]))
    lock = lambda gens: _run_skewed(gens, n_grp, 0)

    tq_col = i * QB + (lax.broadcasted_iota(jnp.int32, (rows, 1), 0) & (QB - 1))
    tabs = (rc_ref[...], rs1_ref[...], rs2_ref[...])

    t_last = (i * QB) // SEL_TILE
    odd = (t_last & 1) == 1
    tq = i * QB + lax.broadcasted_iota(jnp.int32, (QB, 1), 0)
    ncmp = kc_ref.shape[1]
    cend = lax.broadcasted_iota(jnp.int32, (QB, ncmp), 1) * CMP_STRIDE + (CMP_BLOCK - 1)
    cmp_bias = jnp.where(cend <= tq, 0.0, NEG)
    w0 = jnp.maximum((i + 1) * QB - WIN_BAND, 0)
    diff = (tq - w0) - lax.broadcasted_iota(jnp.int32, (QB, WIN_BAND), 1)
    win_bias = jnp.where(lax.bitcast_convert_type(diff, jnp.uint32) < jnp.uint32(WINDOW), 0.0, NEG)
    kpos = t_last * SEL_TILE + lax.broadcasted_iota(jnp.int32, (QB, SEL_TILE), 1)
    diag_bias = jnp.where(kpos <= tq, 0.0, NEG)

    lock([_nsa_head(c, i, tq_col, tabs, ovt_ref[...], cmp_bias, win_bias) for c in grps])

    def pair(t2, carry):
        lock([_nsa_pair(c, t2) for c in grps])
        return carry

    lax.fori_loop(0, t_last // 2, pair, 0)

    @pl.when(odd)
    def _():
        lock([_nsa_odd(c, t_last) for c in grps])

    lock([_nsa_tail(c, t_last, odd, diag_bias) for c in grps])


def nsa_attention(proj, gate_b, rope_q, overlap, shared, batch):
    N = proj.shape[0]
    T = N // batch
    nq = T // QB
    W = NSA_REP * HD
    k_cmp, v_cmp, k_sel, v_sel, k_win, v_win = shared
    rc, rs1, rs2 = rope_q
    ncmp = k_cmp.shape[2]
    n_blk = T // SEL_BLOCK
    P = NSA_GROUPS_PER_STEP
    steps = NSA_GROUPS // P
    rows = NSA_REP * QB
    row_blk = lambda b, g, i: (b * nq + i, g)
    kv = lambda w: pl.BlockSpec((None, P, T, w), lambda b, g, i: (b, g, 0, 0))
    cm = pl.BlockSpec((None, P, ncmp, HD), lambda b, g, i: (b, g, 0, 0))
    rope = pl.BlockSpec((QB, HD), lambda b, g, i: (i, 0))
    gate_blk0 = 2 * NSA_HEADS * HD // (P * 128)
    return pl.pallas_call(
        _nsa_body,
        out_shape=jax.ShapeDtypeStruct((N, NSA_HEADS * HD), BF16),
        grid=(batch, steps, nq),
        in_specs=[pl.BlockSpec((QB, P * W), row_blk),
                  pl.BlockSpec((QB, P * W), lambda b, g, i: (b * nq + i, steps + g)),
                  pl.BlockSpec((QB, P * 128), lambda b, g, i: (b * nq + i, gate_blk0 + g)),
                  pl.BlockSpec((1, P * 128), lambda b, g, i: (0, g)),
                  rope, rope, rope,
                  pl.BlockSpec((n_blk, ncmp), lambda b, g, i: (0, 0)),
                  cm, cm, kv(k_sel.shape[3]), kv(v_sel.shape[3]), kv(k_win.shape[3]), kv(v_win.shape[3])],
        out_specs=pl.BlockSpec((QB, P * W), row_blk),
        scratch_shapes=[pltpu.VMEM((P, rows, k_sel.shape[3]), BF16),
                        pltpu.VMEM((P, rows, SEL_TILE), F32),
                        pltpu.VMEM((P, rows, SEL_TILE), F32),
                        pltpu.VMEM((P, rows, 1), F32),
                        pltpu.VMEM((P, rows, v_sel.shape[3]), F32)],
        compiler_params=pltpu.CompilerParams(
            dimension_semantics=("parallel", "parallel", "arbitrary"), vmem_limit_bytes=VMEM_LIMIT),
        name="nsa_attention",
    )(proj, proj, proj, gate_b, rc, rs1, rs2, overlap, k_cmp, v_cmp, k_sel, v_sel, k_win, v_win)


def _rms(x, g):
    return x * lax.rsqrt(jnp.mean(x * x, axis=-1, keepdims=True) + NORM_EPS) * g


def _norm_body(x_ref, g_ref, o_ref):
    o_ref[...] = _rms(x_ref[...], g_ref[...]).astype(o_ref.dtype)


def rmsnorm(h, g, out_dtype, tm=512):
    N, D = h.shape
    return pl.pallas_call(
        _norm_body,
        out_shape=jax.ShapeDtypeStruct((N, D), out_dtype),
        grid=(N // tm,),
        in_specs=[pl.BlockSpec((tm, D), lambda i: (i, 0)), pl.BlockSpec((1, D), lambda i: (0, 0))],
        out_specs=pl.BlockSpec((tm, D), lambda i: (i, 0)),
        compiler_params=pltpu.CompilerParams(dimension_semantics=("parallel",)),
        name="rmsnorm",
    )(h, g.reshape(1, D))


def _mix_body(x_ref, xp_ref, g_ref, mu_ref, o_ref, *, seq):
    tm = x_ref.shape[0]
    g = g_ref[...]
    u = _rms(x_ref[...], g)
    last_prev = _rms(xp_ref[...], g)[xp_ref.shape[0] - 1:, :]
    first = lax.rem(pl.program_id(0) * tm, seq) == 0
    row = lax.broadcasted_iota(jnp.int32, (tm, 1), 0)
    prev = jnp.where(row == 0, jnp.where(first, 0.0, last_prev), pltpu.roll(u, 1, 0))
    xx = prev - u
    for n in range(o_ref.shape[0]):
        o_ref[n] = (u + xx * mu_ref[n:n + 1, :]).astype(o_ref.dtype)


def rwkv_mix(h, g, mu, seq, tm=256):
    N, D = h.shape
    n_mix = mu.shape[0]
    assert seq % tm == 0
    sub = 8
    return pl.pallas_call(
        functools.partial(_mix_body, seq=seq),
        out_shape=jax.ShapeDtypeStruct((n_mix, N, D), BF16),
        grid=(N // tm,),
        in_specs=[pl.BlockSpec((tm, D), lambda i: (i, 0)),
                  pl.BlockSpec((sub, D), lambda i: (jnp.maximum(i * (tm // sub) - 1, 0), 0)),
                  pl.BlockSpec((1, D), lambda i: (0, 0)),
                  pl.BlockSpec((n_mix, D), lambda i: (0, 0))],
        out_specs=pl.BlockSpec((n_mix, tm, D), lambda i: (0, i, 0)),
        compiler_params=pltpu.CompilerParams(dimension_semantics=("parallel",), vmem_limit_bytes=VMEM_LIMIT),
        name="rwkv_mix",
    )(h, h, g.reshape(1, D), mu)


def _kvpost_body(kv_ref, rc_ref, rs1_ref, rs2_ref, pek_ref, pev_ref, cx_ref, ks_ref, vs_ref, kw_ref, vw_ref):
    tt = kv_ref.shape[0]
    aug = ks_ref.shape[-1] - HD
    tok = pl.program_id(1) * tt + lax.broadcasted_iota(jnp.int32, (tt, aug), 0)
    lane = lax.broadcasted_iota(jnp.int32, (tt, aug), 1)
    onehot = jnp.where(lane == (tok >> (SEL_BLOCK.bit_length() - 1)), 1.0, 0.0).astype(BF16)
    ones_col = jnp.where(lax.broadcasted_iota(jnp.int32, (tt, vs_ref.shape[-1] - HD), 1) == 0, 1.0, 0.0).astype(BF16)
    tabs = (rc_ref[...], rs1_ref[...], rs2_ref[...])
    G = NSA_GROUPS
    for g in range(G):
        k_c, v_c, k_s, v_s, k_w, v_w = [kv_ref[:, (j * G + g) * HD:(j * G + g + 1) * HD] for j in range(6)]
        cx_ref[0, g] = (k_c + pek_ref[0]).astype(BF16)
        cx_ref[1, g] = (v_c + pev_ref[0]).astype(BF16)
        cx_ref[2, g] = (k_c + pek_ref[1]).astype(BF16)
        cx_ref[3, g] = (v_c + pev_ref[1]).astype(BF16)
        ks_ref[g] = jnp.concatenate([_rope(k_s, *tabs).astype(BF16), onehot], axis=1)
        vs_ref[g] = jnp.concatenate([v_s.astype(BF16), ones_col], axis=1)
        kw_ref[g] = _rope(k_w, *tabs).astype(BF16)
        vw_ref[g] = jnp.concatenate([v_w.astype(BF16), ones_col], axis=1)


def kv_post(kv, rope_t, cmp_pe, batch, tt=512):
    N = kv.shape[0]
    T = N // batch
    G = NSA_GROUPS
    nt = T // tt
    n_blk = T // SEL_BLOCK
    aug = -(-n_blk // 128) * 128
    reps = tt // CMP_STRIDE
    pe = jnp.stack([jnp.tile(cmp_pe[:, :CMP_STRIDE], (1, reps, 1)), jnp.tile(cmp_pe[:, CMP_STRIDE:], (1, reps, 1))], 1)
    tab = pl.BlockSpec((tt, HD), lambda b, i: (i, 0))
    pes = pl.BlockSpec((2, tt, HD), lambda b, i: (0, 0, 0))
    out4 = lambda w: pl.BlockSpec((None, G, tt, w), lambda b, i: (b, 0, i, 0))
    sds = lambda w: jax.ShapeDtypeStruct((batch, G, T, w), BF16)
    return pl.pallas_call(
        _kvpost_body,
        out_shape=(jax.ShapeDtypeStruct((4, batch, G, T, HD), BF16), sds(HD + aug), sds(2 * HD), sds(HD), sds(2 * HD)),
        grid=(batch, nt),
        in_specs=[pl.BlockSpec((tt, kv.shape[1]), lambda b, i: (b * nt + i, 0)), tab, tab, tab, pes, pes],
        out_specs=(pl.BlockSpec((4, None, G, tt, HD), lambda b, i: (0, b, 0, i, 0)),
                   out4(HD + aug), out4(2 * HD), out4(HD), out4(2 * HD)),
        compiler_params=pltpu.CompilerParams(dimension_semantics=("parallel", "parallel"), vmem_limit_bytes=VMEM_LIMIT),
        name="kv_post",
    )(kv, *rope_t, pe[0], pe[1])


def _cmp2_body(lo_ref, hi_ref, w2_ref, rc_ref, rs1_ref, rs2_ref, o_ref):
    n = lo_ref.shape[0]
    hid = lo_ref[...] + pltpu.roll(hi_ref[...], n - 1, 0)
    out = _dot(_silu(hid).astype(BF16), w2_ref[...])
    roped = _rope(out, rc_ref[...], rs1_ref[...], rs2_ref[...])
    o_ref[...] = jnp.where(pl.program_id(0) == 0, roped, out).astype(o_ref.dtype)


def cmp_mlp2(hd, w2, rope_c, groups, ng):
    H = hd.shape[2]
    tab = pl.BlockSpec((ng, HD), lambda j, m: (0, 0))
    return pl.pallas_call(
        _cmp2_body,
        out_shape=jax.ShapeDtypeStruct((2, groups, ng, HD), BF16),
        grid=(2, groups),
        in_specs=[pl.BlockSpec((None, ng, H), lambda j, m: (j, m, 0)),
                  pl.BlockSpec((None, ng, H), lambda j, m: (2 + j, m, 0)),
                  pl.BlockSpec((None, H, HD), lambda j, m: (j, 0, 0)), tab, tab, tab],
        out_specs=pl.BlockSpec((None, None, ng, HD), lambda j, m: (j, m, 0, 0)),
        compiler_params=pltpu.CompilerParams(dimension_semantics=("parallel", "parallel")),
        name="cmp_mlp2",
    )(hd, hd, w2, *rope_c)


def _rope_tables(pos):
    half = ROPE_DIM // 2
    inv = ROPE_THETA ** (-jnp.arange(half, dtype=F32) / half)
    ang = pos.astype(F32)[:, None] * inv[None, :]
    cos, sin = jnp.cos(ang), jnp.sin(ang)
    n = pos.shape[0]
    c = jnp.concatenate([cos, cos, jnp.ones((n, HD - ROPE_DIM), F32)], axis=1)
    s1 = jnp.concatenate([-sin, jnp.zeros((n, HD - half), F32)], axis=1)
    s2 = jnp.concatenate([jnp.zeros((n, half), F32), sin, jnp.zeros((n, HD - ROPE_DIM), F32)], axis=1)
    return c, s1, s2


def _rwkv_layer(h, batch, v_first, norm_g, mu, w_rkvz, w0, w1, w2, a0, a1, a2, vres, k_k, k_a, r_k, ln_g, ln_b, w_o):
    N, D = h.shape
    mixed = rwkv_mix(h, norm_g, mu, N // batch)
    rkvz = matmul(mixed, w_rkvz.astype(BF16))

    loras = [(w1, w2, w0), (a1, a2, a0)]
    idx = [4, 5]
    if vres is not None:
        loras.append((vres[1], vres[2], vres[0]))
        idx.append(2)
    rpad = 128
    a_w = jnp.stack([jnp.pad(l[0], ((0, 0), (0, rpad - l[0].shape[1]))) for l in loras]).astype(BF16)
    b_w = jnp.stack([jnp.pad(l[1], ((0, rpad - l[1].shape[0]), (0, 0))) for l in loras]).astype(BF16)
    bias = jnp.stack([l[2].reshape(1, D) for l in loras])
    lor = lora(mixed, idx, a_w, b_w, bias)

    o = wkv(rkvz, lor, v_first, k_k, k_a, r_k.reshape(-1), ln_g, ln_b, batch)
    return matmul(o, w_o.astype(BF16), res=h), rkvz


def _shared_kv(h, batch, rope_t, kv_norm_g, kv_w, cmp_pe, cmp_w1, cmp_w2):
    N, D = h.shape
    T = N // batch
    G = NSA_GROUPS
    kv = matmul(rmsnorm(h, kv_norm_g, BF16), kv_w.astype(BF16))
    cx, k_sel, v_sel, k_win, v_win = kv_post(kv, rope_t, cmp_pe, batch)

    ng = T // CMP_STRIDE
    half = CMP_STRIDE * HD
    w1 = jnp.concatenate([cmp_w1[:, :half], cmp_w1[:, half:]], axis=0).astype(BF16)
    hd = matmul(cx.reshape(4, batch * G * ng, half), w1)
    rope_c = _rope_tables(jnp.arange(ng) * CMP_STRIDE + CMP_BLOCK - 1)
    cmp = cmp_mlp2(hd, cmp_w2.astype(BF16), rope_c, batch * G, ng).reshape(2, batch, G, ng, HD)
    return cmp[0], cmp[1], k_sel, v_sel, k_win, v_win


def _nsa_layer(h, batch, shared, rope_t, overlap, norm_g, w_in, gate_b, w_o):
    N, D = h.shape
    G = NSA_GROUPS
    width = NSA_HEADS * HD
    ng = N_BRANCH * NSA_HEADS
    per = N_BRANCH * NSA_REP
    w_g = jnp.pad(w_in[:, width:width + ng].reshape(D, G, per), ((0, 0), (0, 0), (0, 128 - per))).reshape(D, G * 128)
    w_all = jnp.concatenate([w_in[:, :width], w_in[:, width + ng:], w_g], axis=1).astype(BF16)
    b_g = jnp.pad(gate_b.reshape(G, per), ((0, 0), (0, 128 - per))).reshape(1, G * 128)
    proj = matmul(rmsnorm(h, norm_g, BF16), w_all, tn=768)
    o = nsa_attention(proj, b_g, rope_t, overlap, shared, batch)
    return matmul(o, w_o.astype(BF16), res=h)


def kernel(x, a_norm_g, a_mu, a_w_rkvz, a_w0, a_w1, a_w2, a_a0, a_a1, a_a2, a_v0, a_v1, a_v2, a_k_k, a_k_a, a_r_k,
           a_ln_g, a_ln_b, a_w_o, kv_norm_g, kv_w, cmp_pe, cmp_w1, cmp_w2, b_norm_g, b_w_in, b_gate_b, b_w_o, final_g):
    B, T, D = x.shape
    N = B * T
    n_a = a_norm_g.shape[0]
    n_b = b_norm_g.shape[0]
    h = x.reshape(N, D)
    v_first = None
    for i in range(n_a):
        vres = None if i == 0 else (a_v0[i - 1], a_v1[i - 1], a_v2[i - 1])
        h, rkvz = _rwkv_layer(h, B, v_first, a_norm_g[i], a_mu[i], a_w_rkvz[i], a_w0[i], a_w1[i], a_w2[i],
                              a_a0[i], a_a1[i], a_a2[i], vres, a_k_k[i], a_k_a[i], a_r_k[i],
                              a_ln_g[i], a_ln_b[i], a_w_o[i])
        if i == 0:
            v_first = rkvz
    if n_b:
        rope_t = _rope_tables(jnp.arange(T))
        shared = _shared_kv(h, B, rope_t, kv_norm_g, kv_w, cmp_pe, cmp_w1, cmp_w2)
        ncmp = T // CMP_STRIDE
        n_blk = T // SEL_BLOCK
        cpos = jnp.arange(ncmp)[:, None] * CMP_STRIDE + jnp.arange(CMP_BLOCK)[None, :]
        overlap = jax.nn.one_hot(cpos // SEL_BLOCK, n_blk, dtype=F32).mean(axis=1).T.astype(BF16)
        for j in range(n_b):
            h = _nsa_layer(h, B, shared, rope_t, overlap, b_norm_g[j], b_w_in[j], b_gate_b[j], b_w_o[j])
    return rmsnorm(h, final_g, F32).reshape(B, T, D)
```

```python
import functools
import math
import types

import jax
import jax.numpy as jnp
from jax import lax
from jax.experimental import pallas as pl
from jax.experimental.pallas import tpu as pltpu

F32 = jnp.float32
BF16 = jnp.bfloat16

NORM_EPS = 1e-6
GN_EPS = 64e-5
HEAD = 64
CHUNK = 64
UNIT = 256
UNIT_HEADS = UNIT // HEAD
WKV_CHUNKS = 2
WKV_GROUP = 4
WKV_LAG = 12
EXP_M05 = math.exp(-0.5)

NSA_HEADS = 16
NSA_GROUPS = 4
NSA_REP = NSA_HEADS // NSA_GROUPS
HD = 128
N_BRANCH = 3
CMP_BLOCK = 32
CMP_STRIDE = 16
SEL_BLOCK = 64
SEL_TOP_N = 16
WINDOW = 512
QB = 128
NSA_GROUPS_PER_STEP = 2
LOG2E = math.log2(math.e)
ROPE_DIM = HD // 4
ROPE_THETA = 500000.0
NEG = -1e30
FORCE_BONUS = 1e4
SEL_TILE = 512
WIN_BAND = WINDOW + 128

VMEM_LIMIT = 56 * 1024 * 1024


def _dot(a, b):
    return jnp.dot(a, b, preferred_element_type=F32)


def _dot_nt(a, b):
    return lax.dot_general(a, b, (((1,), (1,)), ((), ())), preferred_element_type=F32)


def _dot_tn(a, b):
    return lax.dot_general(a, b, (((0,), (0,)), ((), ())), preferred_element_type=F32)


def _split3(x):
    h = x.astype(BF16)
    r1 = x - h.astype(F32)
    m = r1.astype(BF16)
    l = (r1 - m.astype(F32)).astype(BF16)
    return h, m, l


def _sigmoid(x):
    return 1.0 / (1.0 + jnp.exp(-x))


def _silu(x):
    return x * _sigmoid(x)


def _mm_body(*refs, has_res):
    if has_res:
        a_ref, w_ref, r_ref, o_ref = refs
    else:
        a_ref, w_ref, o_ref = refs
    acc = _dot(a_ref[...].astype(BF16), w_ref[...].astype(BF16))
    if has_res:
        acc = acc + r_ref[...]
    o_ref[...] = acc.astype(o_ref.dtype)


def matmul(a, w, res=None, out_dtype=F32, tm=1024, tn=1024):
    squeeze = a.ndim == 2
    if squeeze:
        a, w = a[None], w[None]
        res = None if res is None else res[None]
    G, K, N = w.shape
    M = a.shape[1]
    tm = min(tm, M)
    tn = min(tn, N)
    assert M % tm == 0 and N % tn == 0, (M, N, tm, tn)
    in_specs = [pl.BlockSpec((None, tm, K), lambda g, i, j: (g, i, 0)),
                pl.BlockSpec((None, K, tn), lambda g, i, j: (g, 0, j))]
    args = [a, w]
    if res is not None:
        in_specs.append(pl.BlockSpec((None, tm, tn), lambda g, i, j: (g, i, j)))
        args.append(res)
    out = pl.pallas_call(
        functools.partial(_mm_body, has_res=res is not None),
        out_shape=jax.ShapeDtypeStruct((G, M, N), out_dtype),
        grid=(G, M // tm, N // tn),
        in_specs=in_specs,
        out_specs=pl.BlockSpec((None, tm, tn), lambda g, i, j: (g, i, j)),
        compiler_params=pltpu.CompilerParams(
            dimension_semantics=("parallel", "parallel", "arbitrary"), vmem_limit_bytes=VMEM_LIMIT),
        name="matmul",
    )(*args)
    return out[0] if squeeze else out


def _lora_out_body(mid_ref, b_ref, bias_ref, o_ref):
    o_ref[...] = _dot(mid_ref[...], b_ref[...]) + bias_ref[...]


def lora_out(mid, b_w, bias, tm=1024):
    L, N, R = mid.shape
    D = b_w.shape[2]
    tm = min(tm, N)
    return pl.pallas_call(
        _lora_out_body,
        out_shape=jax.ShapeDtypeStruct((L, N, D), F32),
        grid=(L, N // tm),
        in_specs=[pl.BlockSpec((None, tm, R), lambda n, i: (n, i, 0)),
                  pl.BlockSpec((None, R, D), lambda n, i: (n, 0, 0)),
                  pl.BlockSpec((None, 1, D), lambda n, i: (n, 0, 0))],
        out_specs=pl.BlockSpec((None, tm, D), lambda n, i: (n, i, 0)),
        compiler_params=pltpu.CompilerParams(
            dimension_semantics=("parallel", "arbitrary"), vmem_limit_bytes=VMEM_LIMIT),
        name="lora_out",
    )(mid, b_w, bias)


def _expand(x, lane_head):
    return jnp.concatenate([jnp.where(lane_head == h, x, 0.0) for h in range(UNIT_HEADS)], axis=0)


def _compact(xe):
    out = xe[0:CHUNK]
    for h in range(1, UNIT_HEADS):
        out = out + xe[h * CHUNK:(h + 1) * CHUNK]
    return out


def _segsum(x, ones_bd):
    h = x.astype(BF16)
    m = (x - h.astype(F32)).astype(BF16)
    return _dot(h, ones_bd) + _dot(m, ones_bd)


def _wkv_unit_stages(r, k, v, z, wl, al, vl, vf, k_k, k_a, r_k, ln_g, ln_b, s_ref, o_ref, consts):
    tri, ones_bd, same, strict, incl, lane_head, eye = consts
    bf = lambda x: x.astype(BF16)
    ex = lambda x: _expand(x, lane_head)

    a = _sigmoid(al[...])
    lw = -EXP_M05 * _sigmoid(wl[...])
    v = v[...]
    if vl is not None:
        v = v + (vf[...] - v) * _sigmoid(vl[...])
    k = k[...]
    kk = k * k_k[...]
    n2 = _segsum(kk * kk, ones_bd)
    lh, lm, ll = _split3(lw)
    cum = _dot(tri, lh) + _dot(tri, lm) + _dot(tri, ll)
    yield

    kk = kk / jnp.maximum(jnp.sqrt(n2), 1e-12)
    k2 = k * (1.0 + (a - 1.0) * k_a[...])
    p_incl = jnp.exp(cum)
    p_inv = jnp.exp(-cum)
    r = r[...]
    kt = k2 * p_inv
    bt = kk * a * p_inv
    ea = bf(ex(-kk * jnp.exp(cum - lw)))
    er32 = ex(r * p_incl)
    er = bf(er32)
    eb = bf(ex(bt))
    ek = bf(ex(kt))
    ev = bf(ex(v))
    yield

    a_ab = jnp.where(strict, _dot_nt(ea, eb), 0.0)
    yield
    a_ak = bf(jnp.where(strict, _dot_nt(ea, ek), 0.0))
    yield
    a_rb = bf(jnp.where(incl, _dot_nt(er, eb), 0.0))
    yield
    a_rk = bf(jnp.where(incl, _dot_nt(er, ek), 0.0))
    yield
    akv = bf(_dot(a_ak, ev))
    yield

    tm = eye + a_ab
    pw = bf(a_ab)
    for _ in range(5):
        pw = bf(_dot(pw, pw))
        yield
        tm = tm + _dot(pw, bf(tm))
        yield
    tb = bf(tm)

    u0 = _dot(tb, akv)
    yield
    wt = _dot(tb, ea)
    yield
    o0 = _dot(a_rb, bf(u0)) + _dot(a_rk, ev)
    yield
    qt = er32 + _dot(a_rb, bf(wt))
    yield

    s = s_ref[...]
    qw = _dot_nt(bf(jnp.concatenate([_compact(qt), _compact(wt)], axis=0)), bf(s))
    yield
    o = qw[0:CHUNK] + _compact(o0)
    u = _compact(u0) + qw[CHUNK:2 * CHUNK]
    upd = _dot_tn(bf(jnp.concatenate([v, u], axis=0)), bf(jnp.concatenate([kt, bt], axis=0)))
    yield
    s_ref[...] = (s + jnp.where(same, upd, 0.0)) * p_incl[CHUNK - 1:CHUNK, :]
    mean = _segsum(o, ones_bd) * (1.0 / HEAD)
    yield
    bonus = _segsum(r * k2 * r_k[...], ones_bd)
    yield

    oc = o - mean
    var = _segsum(oc * oc, ones_bd) * (1.0 / HEAD)
    yield

    o_ref[...] = ((oc * lax.rsqrt(var + GN_EPS) * ln_g[...] + ln_b[...] + bonus * v)
                  * _silu(z[...])).astype(o_ref.dtype)


def _run_skewed(gens, group, lag):
    done = [False] * len(gens)
    rnd = 0
    while not all(done):
        for u, g in enumerate(gens):
            if (u // group) * lag <= rnd and not done[u]:
                try:
                    next(g)
                except StopIteration:
                    done[u] = True
        rnd += 1


def _wkv_body(*refs, has_vres, units):
    if has_vres:
        (r_ref, k_ref, v_ref, z_ref, wl_ref, al_ref, vl_ref, vf_ref,
         kk_ref, ka_ref, rk_ref, lg_ref, lb_ref, o_ref, s_ref) = refs
    else:
        (r_ref, k_ref, v_ref, z_ref, wl_ref, al_ref,
         kk_ref, ka_ref, rk_ref, lg_ref, lb_ref, o_ref, s_ref) = refs
        vl_ref = vf_ref = None

    @pl.when(pl.program_id(2) == 0)
    def _():
        s_ref[...] = jnp.zeros_like(s_ref)

    row = lax.broadcasted_iota(jnp.int32, (UNIT, UNIT), 0)
    col = lax.broadcasted_iota(jnp.int32, (UNIT, UNIT), 1)
    same = (row >> 6) == (col >> 6)
    strict = same & ((col & 63) < (row & 63))
    incl = same & ((col & 63) <= (row & 63))
    eye = jnp.where(row == col, 1.0, 0.0).astype(F32)
    ones_bd = jnp.where(same, 1.0, 0.0).astype(BF16)
    tr = lax.broadcasted_iota(jnp.int32, (CHUNK, CHUNK), 0)
    tc = lax.broadcasted_iota(jnp.int32, (CHUNK, CHUNK), 1)
    tri = jnp.where(tc <= tr, 1.0, 0.0).astype(BF16)
    lane_head = lax.broadcasted_iota(jnp.int32, (CHUNK, UNIT), 1) >> 6
    consts = (tri, ones_bd, same, strict, incl, lane_head, eye)

    def view(ref, cc, uu):
        if ref is None:
            return None
        rows = pl.ds(cc * CHUNK, CHUNK) if ref.shape[0] > 1 else slice(None)
        return ref.at[rows, pl.ds(uu * UNIT, UNIT)]

    _run_skewed([
        _wkv_unit_stages(*(view(ref, cc, uu) for ref in (r_ref, k_ref, v_ref, z_ref, wl_ref, al_ref, vl_ref, vf_ref,
                                                         kk_ref, ka_ref, rk_ref, lg_ref, lb_ref)),
                         s_ref.at[uu], view(o_ref, cc, uu), consts)
        for cc in range(WKV_CHUNKS) for uu in range(units)], WKV_GROUP, WKV_LAG)


def wkv(rkvz, lor, v_first, k_k, k_a, r_k, ln_g, ln_b, batch, units=8):
    _, N, D = rkvz.shape
    T = N // batch
    step_rows = WKV_CHUNKS * CHUNK
    nc = T // step_rows
    W = units * UNIT
    has_vres = v_first is not None
    row_map = lambda b, u, c: (b * nc + c, u)

    def lead(n):
        return pl.BlockSpec((None, step_rows, W), lambda b, u, c: (n, b * nc + c, u))

    in_specs = [lead(0), lead(1), lead(2), lead(3), lead(0), lead(1)]
    args = [rkvz, rkvz, rkvz, rkvz, lor, lor]
    if has_vres:
        in_specs += [lead(2), lead(2)]
        args += [lor, v_first]
    par = pl.BlockSpec((1, W), lambda b, u, c: (0, u))
    in_specs += [par] * 5
    args += [k_k.reshape(1, D), k_a.reshape(1, D), r_k.reshape(1, D), ln_g.reshape(1, D), ln_b.reshape(1, D)]
    return pl.pallas_call(
        functools.partial(_wkv_body, has_vres=has_vres, units=units),
        out_shape=jax.ShapeDtypeStruct((N, D), BF16),
        grid=(batch, D // W, nc),
        in_specs=in_specs,
        out_specs=pl.BlockSpec((step_rows, W), row_map),
        scratch_shapes=[pltpu.VMEM((units, UNIT, UNIT), F32)],
        compiler_params=pltpu.CompilerParams(
            dimension_semantics=("parallel", "parallel", "arbitrary"), vmem_limit_bytes=VMEM_LIMIT),
        name="wkv7",
    )(*args)


def _attend(s, v_aug):
    m = jnp.max(s, axis=1, keepdims=True)
    acc = _dot(jnp.exp2(s - m).astype(BF16), v_aug)
    return acc[:, :HD] * (1.0 / acc[:, HD:HD + 1])


def _rope(x, c, s1, s2):
    n = x.shape[1] // HD
    if n > 1:
        c, s1, s2 = (jnp.concatenate([t] * n, axis=1) for t in (c, s1, s2))
    half = ROPE_DIM // 2
    return x * c + pltpu.roll(x, x.shape[1] - half, 1) * s1 + pltpu.roll(x, half, 1) * s2


def _tile_rows(t):
    return pl.ds(pl.multiple_of(t * SEL_TILE, SEL_TILE), SEL_TILE)


def _nsa_scores(c, t, dst):
    dst[...] = _dot_nt(c.qa[...], c.ks[_tile_rows(t), :])


def _nsa_absorb(c, t, s):
    m = c.m[...]
    m_new = jnp.maximum(m, jnp.max(s, axis=1, keepdims=True))
    pv = _dot(jnp.exp2(s - m_new).astype(BF16), c.vs[_tile_rows(t), :])
    c.acc[...] = jnp.exp2(m - m_new) * c.acc[...] + pv
    c.m[...] = m_new


def _nsa_pair(c, t):
    _nsa_scores(c, t + 1, c.s1)
    yield
    _nsa_absorb(c, t, c.s0[...])
    yield
    _nsa_scores(c, t + 2, c.s0)
    yield
    _nsa_absorb(c, t + 1, c.s1[...])
    yield


def _nsa_first_of_odd(c):
    _nsa_scores(c, 1, c.s0)
    yield
    _nsa_absorb(c, 0, c.s1[...])
    yield


def _add_rows(s, bias):
    return jnp.concatenate([s[r * QB:(r + 1) * QB] + bias for r in range(NSA_REP)], axis=0)


def _nsa_head(c, i, tq_col, tabs, ov_t, cmp_bias, win_bias):
    rows = NSA_REP * QB
    q4 = _rope(c.q[...], *tabs) * (HD ** -0.5 * LOG2E)
    qs = jnp.concatenate([q4[:, r * HD:(r + 1) * HD] for r in range(NSA_REP)], axis=0)
    qb = qs.astype(BF16)

    band = WIN_BAND
    w0 = pl.multiple_of(jnp.maximum((i + 1) * QB - band, 0), QB)
    s_cmp = _dot_nt(qb, c.kc[...])
    yield
    s_win = _dot_nt(qb, c.kw[pl.ds(w0, band), :])
    yield

    s_cmp = _add_rows(s_cmp, cmp_bias)
    e = jnp.exp2(s_cmp - jnp.max(s_cmp, axis=1, keepdims=True))
    p = e * (jnp.where(tq_col >= CMP_BLOCK - 1, 1.0, 0.0) / jnp.sum(e, axis=1, keepdims=True))
    c.o_cmp = _dot(p.astype(BF16), c.vc[...])
    yield

    psum = p[0:QB]
    for r in range(1, NSA_REP):
        psum = psum + p[r * QB:(r + 1) * QB]
    ph, pm, pl_ = _split3(psum)
    n_blk = ov_t.shape[0]
    imp_t = _dot_nt(ov_t, ph) + _dot_nt(ov_t, pm) + _dot_nt(ov_t, pl_)
    yield

    c.o_win = _attend(_add_rows(s_win, win_bias), c.vw[pl.ds(w0, band), :])
    yield

    jsub = lax.broadcasted_iota(jnp.int32, (n_blk, QB), 0)
    cur = (i * QB + lax.broadcasted_iota(jnp.int32, (n_blk, QB), 1)) >> (SEL_BLOCK.bit_length() - 1)
    forced = (jsub == 0) | (jsub == cur) | (jsub == cur - 1)
    imp_t = jnp.where(forced, FORCE_BONUS, jnp.where(jsub > cur, NEG, imp_t))
    sub8 = lax.broadcasted_iota(jnp.int32, (8, QB), 0)
    grp = [imp_t[8 * a:8 * a + 8] for a in range(n_blk // 8)]
    cnt = [jnp.zeros((8, QB), F32) for _ in grp]
    for jp in range(n_blk):
        rowv = jnp.broadcast_to(imp_t[jp:jp + 1, :], (8, QB))
        for a, x in enumerate(grp):
            ge = lambda: jnp.where(rowv >= x, 1.0, 0.0)
            gt = lambda: jnp.where(rowv > x, 1.0, 0.0)
            if 8 * a > jp:
                inc = ge()
            elif 8 * a + 7 <= jp:
                inc = gt()
            else:
                inc = jnp.where(sub8 + 8 * a > jp, ge(), gt())
            cnt[a] = cnt[a] + inc
    cnt = jnp.concatenate(cnt, axis=0)
    bias_t = jnp.where(cnt < float(min(SEL_TOP_N, n_blk)), 0.0, NEG).astype(BF16)
    qi = lax.broadcasted_iota(jnp.int32, (QB, QB), 0)
    qj = lax.broadcasted_iota(jnp.int32, (QB, QB), 1)
    eye_q = jnp.where(qi == qj, 1.0, 0.0).astype(BF16)
    bias = _dot_nt(eye_q, bias_t)
    yield
    bias4 = jnp.concatenate([bias] * NSA_REP, axis=0).astype(BF16)
    pad = c.ks.shape[1] - HD - n_blk
    parts = [qb, bias4] + ([jnp.zeros((rows, pad), BF16)] if pad else [])
    qa = jnp.concatenate(parts, axis=1)

    c.qa[...] = qa
    s_first = _dot_nt(qa, c.ks[_tile_rows(0), :])
    c.s0[...] = s_first
    c.s1[...] = s_first
    c.m[...] = jnp.full(c.m.shape, NEG, F32)
    c.acc[...] = jnp.zeros(c.acc.shape, F32)
    yield


def _nsa_tail(c, t_last, diag_bias):
    _nsa_absorb(c, t_last, _add_rows(c.s0[...], diag_bias))
    yield
    acc = c.acc[...]
    o_sel = acc[:, :HD] * (1.0 / acc[:, HD:HD + 1])
    g = _sigmoid(c.g[...] + c.gb[...])
    outs = []
    for r in range(NSA_REP):
        rs = slice(r * QB, (r + 1) * QB)
        outs.append(g[:, 3 * r:3 * r + 1] * c.o_cmp[rs] + g[:, 3 * r + 1:3 * r + 2] * o_sel[rs]
                    + g[:, 3 * r + 2:3 * r + 3] * c.o_win[rs])
    o4 = jnp.concatenate(outs, axis=1)
    c.o[...] = (o4 * _silu(c.z[...])).astype(c.o.dtype)


def _nsa_body(q_ref, z_ref, g_ref, gb_ref, rc_ref, rs1_ref, rs2_ref, ovt_ref,
              kc_ref, vc_ref, ks_ref, vs_ref, kw_ref, vw_ref, o_ref, qa_scr, s0_scr, s1_scr, m_scr, acc_scr):
    i = pl.program_id(2)
    rows = NSA_REP * QB
    W = NSA_REP * HD
    n_grp = kc_ref.shape[0]
    grps = []
    for g in range(n_grp):
        lanes = pl.ds(g * W, W)
        glanes = pl.ds(g * 128, 128)
        grps.append(types.SimpleNamespace(
            q=q_ref.at[:, lanes], z=z_ref.at[:, lanes], g=g_ref.at[:, glanes], gb=gb_ref.at[:, glanes],
            kc=kc_ref.at[g], vc=vc_ref.at[g], ks=ks_ref.at[g], vs=vs_ref.at[g], kw=kw_ref.at[g], vw=vw_ref.at[g],
            o=o_ref.at[:, lanes], qa=qa_scr.at[g], s0=s0_scr.at[g], s1=s1_scr.at[g], m=m_scr.at[g], acc=acc_scr.at[g]))
    lock = lambda gens: _run_skewed(gens, n_grp, 0)

    tq_col = i * QB + (lax.broadcasted_iota(jnp.int32, (rows, 1), 0) & (QB - 1))
    tabs = (rc_ref[...], rs1_ref[...], rs2_ref[...])

    t_last = (i * QB) // SEL_TILE
    odd = (t_last & 1) == 1
    tq = i * QB + lax.broadcasted_iota(jnp.int32, (QB, 1), 0)
    ncmp = kc_ref.shape[1]
    cend = lax.broadcasted_iota(jnp.int32, (QB, ncmp), 1) * CMP_STRIDE + (CMP_BLOCK - 1)
    cmp_bias = jnp.where(cend <= tq, 0.0, NEG)
    w0 = jnp.maximum((i + 1) * QB - WIN_BAND, 0)
    diff = (tq - w0) - lax.broadcasted_iota(jnp.int32, (QB, WIN_BAND), 1)
    win_bias = jnp.where(lax.bitcast_convert_type(diff, jnp.uint32) < jnp.uint32(WINDOW), 0.0, NEG)
    kpos = t_last * SEL_TILE + lax.broadcasted_iota(jnp.int32, (QB, SEL_TILE), 1)
    diag_bias = jnp.where(kpos <= tq, 0.0, NEG)

    lock([_nsa_head(c, i, tq_col, tabs, ovt_ref[...], cmp_bias, win_bias) for c in grps])

    @pl.when(odd)
    def _():
        lock([_nsa_first_of_odd(c) for c in grps])

    def pair(t2, carry):
        lock([_nsa_pair(c, 2 * t2 + (t_last & 1)) for c in grps])
        return carry

    lax.fori_loop(0, t_last // 2, pair, 0)
    lock([_nsa_tail(c, t_last, diag_bias) for c in grps])


def nsa_attention(proj, gate_b, rope_q, overlap, shared, batch):
    N = proj.shape[0]
    T = N // batch
    nq = T // QB
    W = NSA_REP * HD
    k_cmp, v_cmp, k_sel, v_sel, k_win, v_win = shared
    rc, rs1, rs2 = rope_q
    ncmp = k_cmp.shape[2]
    n_blk = T // SEL_BLOCK
    P = NSA_GROUPS_PER_STEP
    steps = NSA_GROUPS // P
    rows = NSA_REP * QB
    row_blk = lambda b, g, i: (b * nq + i, g)
    kv = lambda w: pl.BlockSpec((None, P, T, w), lambda b, g, i: (b, g, 0, 0))
    cm = pl.BlockSpec((None, P, ncmp, HD), lambda b, g, i: (b, g, 0, 0))
    rope = pl.BlockSpec((QB, HD), lambda b, g, i: (i, 0))
    gate_blk0 = 2 * NSA_HEADS * HD // (P * 128)
    return pl.pallas_call(
        _nsa_body,
        out_shape=jax.ShapeDtypeStruct((N, NSA_HEADS * HD), BF16),
        grid=(batch, steps, nq),
        in_specs=[pl.BlockSpec((QB, P * W), row_blk),
                  pl.BlockSpec((QB, P * W), lambda b, g, i: (b * nq + i, steps + g)),
                  pl.BlockSpec((QB, P * 128), lambda b, g, i: (b * nq + i, gate_blk0 + g)),
                  pl.BlockSpec((1, P * 128), lambda b, g, i: (0, g)),
                  rope, rope, rope,
                  pl.BlockSpec((n_blk, ncmp), lambda b, g, i: (0, 0)),
                  cm, cm, kv(k_sel.shape[3]), kv(v_sel.shape[3]), kv(k_win.shape[3]), kv(v_win.shape[3])],
        out_specs=pl.BlockSpec((QB, P * W), row_blk),
        scratch_shapes=[pltpu.VMEM((P, rows, k_sel.shape[3]), BF16),
                        pltpu.VMEM((P, rows, SEL_TILE), F32),
                        pltpu.VMEM((P, rows, SEL_TILE), F32),
                        pltpu.VMEM((P, rows, 1), F32),
                        pltpu.VMEM((P, rows, v_sel.shape[3]), F32)],
        compiler_params=pltpu.CompilerParams(
            dimension_semantics=("parallel", "parallel", "arbitrary"), vmem_limit_bytes=VMEM_LIMIT),
        name="nsa_attention",
    )(proj, proj, proj, gate_b, rc, rs1, rs2, overlap, k_cmp, v_cmp, k_sel, v_sel, k_win, v_win)


def _rms(x, g):
    return x * lax.rsqrt(jnp.mean(x * x, axis=-1, keepdims=True) + NORM_EPS) * g


def _norm_body(x_ref, g_ref, o_ref):
    o_ref[...] = _rms(x_ref[...], g_ref[...]).astype(o_ref.dtype)


def rmsnorm(h, g, out_dtype, tm=512):
    N, D = h.shape
    return pl.pallas_call(
        _norm_body,
        out_shape=jax.ShapeDtypeStruct((N, D), out_dtype),
        grid=(N // tm,),
        in_specs=[pl.BlockSpec((tm, D), lambda i: (i, 0)), pl.BlockSpec((1, D), lambda i: (0, 0))],
        out_specs=pl.BlockSpec((tm, D), lambda i: (i, 0)),
        compiler_params=pltpu.CompilerParams(dimension_semantics=("parallel",)),
        name="rmsnorm",
    )(h, g.reshape(1, D))


def _mix_body(x_ref, xp_ref, g_ref, mu_ref, a_ref, o_ref, mid_ref, *, seq, lora_src):
    tm = x_ref.shape[0]
    g = g_ref[...]
    u = _rms(x_ref[...], g)
    last_prev = _rms(xp_ref[...], g)[xp_ref.shape[0] - 1:, :]
    first = lax.rem(pl.program_id(0) * tm, seq) == 0
    row = lax.broadcasted_iota(jnp.int32, (tm, 1), 0)
    prev = jnp.where(row == 0, jnp.where(first, 0.0, last_prev), pltpu.roll(u, 1, 0))
    xx = prev - u
    mix = lambda n: (u + xx * mu_ref[n:n + 1, :]).astype(BF16)
    for n in range(o_ref.shape[0]):
        o_ref[n] = mix(n)
    for l, n in enumerate(lora_src):
        mid = _dot(mix(n), a_ref[l])
        mid_ref[l] = (jnp.tanh(mid) if l == 0 else mid).astype(mid_ref.dtype)


def rwkv_mix(h, g, mu, a_w, lora_src, n_out, seq, tm=256):
    N, D = h.shape
    L, _, R = a_w.shape
    assert seq % tm == 0
    sub = 8
    return pl.pallas_call(
        functools.partial(_mix_body, seq=seq, lora_src=tuple(lora_src)),
        out_shape=(jax.ShapeDtypeStruct((n_out, N, D), BF16), jax.ShapeDtypeStruct((L, N, R), BF16)),
        grid=(N // tm,),
        in_specs=[pl.BlockSpec((tm, D), lambda i: (i, 0)),
                  pl.BlockSpec((sub, D), lambda i: (jnp.maximum(i * (tm // sub) - 1, 0), 0)),
                  pl.BlockSpec((1, D), lambda i: (0, 0)),
                  pl.BlockSpec(mu.shape, lambda i: (0, 0)),
                  pl.BlockSpec((L, D, R), lambda i: (0, 0, 0))],
        out_specs=(pl.BlockSpec((n_out, tm, D), lambda i: (0, i, 0)), pl.BlockSpec((L, tm, R), lambda i: (0, i, 0))),
        compiler_params=pltpu.CompilerParams(dimension_semantics=("parallel",), vmem_limit_bytes=VMEM_LIMIT),
        name="rwkv_mix",
    )(h, h, g.reshape(1, D), mu, a_w)


def _kvpost_body(kv_ref, rc_ref, rs1_ref, rs2_ref, pek_ref, pev_ref, cx_ref, ks_ref, vs_ref, kw_ref, vw_ref):
    tt = kv_ref.shape[0]
    aug = ks_ref.shape[-1] - HD
    tok = pl.program_id(1) * tt + lax.broadcasted_iota(jnp.int32, (tt, aug), 0)
    lane = lax.broadcasted_iota(jnp.int32, (tt, aug), 1)
    onehot = jnp.where(lane == (tok >> (SEL_BLOCK.bit_length() - 1)), 1.0, 0.0).astype(BF16)
    ones_col = jnp.where(lax.broadcasted_iota(jnp.int32, (tt, vs_ref.shape[-1] - HD), 1) == 0, 1.0, 0.0).astype(BF16)
    tabs = (rc_ref[...], rs1_ref[...], rs2_ref[...])
    G = NSA_GROUPS
    n16 = tt // CMP_STRIDE
    for g in range(G):
        for j, pe_ref in ((0, pek_ref), (1, pev_ref)):
            x3 = kv_ref[:, (j * G + g) * HD:(j * G + g + 1) * HD].reshape(n16, CMP_STRIDE, HD)
            x16 = jnp.concatenate([x3[:, l, :] for l in range(CMP_STRIDE)], axis=1)
            cx_ref[j, g] = (x16 + pe_ref[0:1, :]).astype(BF16)
            cx_ref[2 + j, g] = (x16 + pe_ref[1:2, :]).astype(BF16)
        k_s, v_s, k_w, v_w = [kv_ref[:, (j * G + g) * HD:(j * G + g + 1) * HD] for j in range(2, 6)]
        ks_ref[g] = jnp.concatenate([_rope(k_s, *tabs).astype(BF16), onehot], axis=1)
        vs_ref[g] = jnp.concatenate([v_s.astype(BF16), ones_col], axis=1)
        kw_ref[g] = _rope(k_w, *tabs).astype(BF16)
        vw_ref[g] = jnp.concatenate([v_w.astype(BF16), ones_col], axis=1)


def kv_post(kv, rope_t, cmp_pe, batch, tt=512):
    N = kv.shape[0]
    T = N // batch
    G = NSA_GROUPS
    nt = T // tt
    n_blk = T // SEL_BLOCK
    aug = -(-n_blk // 128) * 128
    half = CMP_STRIDE * HD
    pe = cmp_pe.reshape(2, 2, half)
    tab = pl.BlockSpec((tt, HD), lambda b, i: (i, 0))
    pes = pl.BlockSpec((2, half), lambda b, i: (0, 0))
    out4 = lambda w: pl.BlockSpec((None, G, tt, w), lambda b, i: (b, 0, i, 0))
    sds = lambda w: jax.ShapeDtypeStruct((batch, G, T, w), BF16)
    n16 = tt // CMP_STRIDE
    return pl.pallas_call(
        _kvpost_body,
        out_shape=(jax.ShapeDtypeStruct((4, batch, G, T // CMP_STRIDE, half), BF16),
                   sds(HD + aug), sds(2 * HD), sds(HD), sds(2 * HD)),
        grid=(batch, nt),
        in_specs=[pl.BlockSpec((tt, kv.shape[1]), lambda b, i: (b * nt + i, 0)), tab, tab, tab, pes, pes],
        out_specs=(pl.BlockSpec((4, None, G, n16, half), lambda b, i: (0, b, 0, i, 0)),
                   out4(HD + aug), out4(2 * HD), out4(HD), out4(2 * HD)),
        compiler_params=pltpu.CompilerParams(dimension_semantics=("parallel", "parallel"), vmem_limit_bytes=VMEM_LIMIT),
        name="kv_post",
    )(kv, *rope_t, pe[0], pe[1])


def _cmp2_body(lo_ref, hi_ref, w2_ref, rc_ref, rs1_ref, rs2_ref, o_ref):
    n = lo_ref.shape[0]
    hid = lo_ref[...] + pltpu.roll(hi_ref[...], n - 1, 0)
    out = _dot(_silu(hid).astype(BF16), w2_ref[...])
    roped = _rope(out, rc_ref[...], rs1_ref[...], rs2_ref[...])
    o_ref[...] = jnp.where(pl.program_id(0) == 0, roped, out).astype(o_ref.dtype)


def cmp_mlp2(hd, w2, rope_c, groups, ng):
    H = hd.shape[2]
    tab = pl.BlockSpec((ng, HD), lambda j, m: (0, 0))
    return pl.pallas_call(
        _cmp2_body,
        out_shape=jax.ShapeDtypeStruct((2, groups, ng, HD), BF16),
        grid=(2, groups),
        in_specs=[pl.BlockSpec((None, ng, H), lambda j, m: (j, m, 0)),
                  pl.BlockSpec((None, ng, H), lambda j, m: (2 + j, m, 0)),
                  pl.BlockSpec((None, H, HD), lambda j, m: (j, 0, 0)), tab, tab, tab],
        out_specs=pl.BlockSpec((None, None, ng, HD), lambda j, m: (j, m, 0, 0)),
        compiler_params=pltpu.CompilerParams(dimension_semantics=("parallel", "parallel")),
        name="cmp_mlp2",
    )(hd, hd, w2, *rope_c)


def _rope_tables(pos):
    half = ROPE_DIM // 2
    inv = ROPE_THETA ** (-jnp.arange(half, dtype=F32) / half)
    ang = pos.astype(F32)[:, None] * inv[None, :]
    cos, sin = jnp.cos(ang), jnp.sin(ang)
    n = pos.shape[0]
    c = jnp.concatenate([cos, cos, jnp.ones((n, HD - ROPE_DIM), F32)], axis=1)
    s1 = jnp.concatenate([-sin, jnp.zeros((n, HD - half), F32)], axis=1)
    s2 = jnp.concatenate([jnp.zeros((n, half), F32), sin, jnp.zeros((n, HD - ROPE_DIM), F32)], axis=1)
    return c, s1, s2


def _rwkv_layer(h, batch, v_first, norm_g, mu, w_rkvz, w0, w1, w2, a0, a1, a2, vres, k_k, k_a, r_k, ln_g, ln_b, w_o):
    N, D = h.shape
    loras = [(w1, w2, w0), (a1, a2, a0)]
    idx = [4, 5]
    if vres is not None:
        loras.append((vres[1], vres[2], vres[0]))
        idx.append(2)
    rpad = 128
    a_w = jnp.stack([jnp.pad(l[0], ((0, 0), (0, rpad - l[0].shape[1]))) for l in loras]).astype(BF16)
    b_w = jnp.stack([jnp.pad(l[1], ((0, rpad - l[1].shape[0]), (0, 0))) for l in loras]).astype(BF16)
    bias = jnp.stack([l[2].reshape(1, D) for l in loras])

    mixed, mid = rwkv_mix(h, norm_g, mu, a_w, idx, 4, N // batch)
    rkvz = matmul(mixed, w_rkvz.astype(BF16))
    lor = lora_out(mid, b_w, bias)

    o = wkv(rkvz, lor, v_first, k_k, k_a, r_k.reshape(-1), ln_g, ln_b, batch)
    return matmul(o, w_o.astype(BF16), res=h), rkvz


def _shared_kv(h, batch, rope_t, kv_norm_g, kv_w, cmp_pe, cmp_w1, cmp_w2):
    N, D = h.shape
    T = N // batch
    G = NSA_GROUPS
    kv = matmul(rmsnorm(h, kv_norm_g, BF16), kv_w.astype(BF16))
    cx, k_sel, v_sel, k_win, v_win = kv_post(kv, rope_t, cmp_pe, batch)

    ng = T // CMP_STRIDE
    half = CMP_STRIDE * HD
    w1 = jnp.concatenate([cmp_w1[:, :half], cmp_w1[:, half:]], axis=0).astype(BF16)
    hd = matmul(cx.reshape(4, batch * G * ng, half), w1)
    rope_c = _rope_tables(jnp.arange(ng) * CMP_STRIDE + CMP_BLOCK - 1)
    cmp = cmp_mlp2(hd, cmp_w2.astype(BF16), rope_c, batch * G, ng).reshape(2, batch, G, ng, HD)
    return cmp[0], cmp[1], k_sel, v_sel, k_win, v_win


def _nsa_layer(h, batch, shared, rope_t, overlap, norm_g, w_in, gate_b, w_o):
    N, D = h.shape
    G = NSA_GROUPS
    width = NSA_HEADS * HD
    ng = N_BRANCH * NSA_HEADS
    per = N_BRANCH * NSA_REP
    w_g = jnp.pad(w_in[:, width:width + ng].reshape(D, G, per), ((0, 0), (0, 0), (0, 128 - per))).reshape(D, G * 128)
    w_all = jnp.concatenate([w_in[:, :width], w_in[:, width + ng:], w_g], axis=1).astype(BF16)
    b_g = jnp.pad(gate_b.reshape(G, per), ((0, 0), (0, 128 - per))).reshape(1, G * 128)
    proj = matmul(rmsnorm(h, norm_g, BF16), w_all, tn=768)
    o = nsa_attention(proj, b_g, rope_t, overlap, shared, batch)
    return matmul(o, w_o.astype(BF16), res=h)


def kernel(x, a_norm_g, a_mu, a_w_rkvz, a_w0, a_w1, a_w2, a_a0, a_a1, a_a2, a_v0, a_v1, a_v2, a_k_k, a_k_a, a_r_k,
           a_ln_g, a_ln_b, a_w_o, kv_norm_g, kv_w, cmp_pe, cmp_w1, cmp_w2, b_norm_g, b_w_in, b_gate_b, b_w_o, final_g):
    B, T, D = x.shape
    N = B * T
    n_a = a_norm_g.shape[0]
    n_b = b_norm_g.shape[0]
    h = x.reshape(N, D)
    v_first = None
    for i in range(n_a):
        vres = None if i == 0 else (a_v0[i - 1], a_v1[i - 1], a_v2[i - 1])
        h, rkvz = _rwkv_layer(h, B, v_first, a_norm_g[i], a_mu[i], a_w_rkvz[i], a_w0[i], a_w1[i], a_w2[i],
                              a_a0[i], a_a1[i], a_a2[i], vres, a_k_k[i], a_k_a[i], a_r_k[i],
                              a_ln_g[i], a_ln_b[i], a_w_o[i])
        if i == 0:
            v_first = rkvz
    if n_b:
        rope_t = _rope_tables(jnp.arange(T))
        shared = _shared_kv(h, B, rope_t, kv_norm_g, kv_w, cmp_pe, cmp_w1, cmp_w2)
        ncmp = T // CMP_STRIDE
        n_blk = T // SEL_BLOCK
        cpos = jnp.arange(ncmp)[:, None] * CMP_STRIDE + jnp.arange(CMP_BLOCK)[None, :]
        overlap = jax.nn.one_hot(cpos // SEL_BLOCK, n_blk, dtype=F32).mean(axis=1).T.astype(BF16)
        for j in range(n_b):
            h = _nsa_layer(h, B, shared, rope_t, overlap, b_norm_g[j], b_w_in[j], b_gate_b[j], b_w_o[j])
    return rmsnorm(h, final_g, F32).reshape(B, T, D)
```

```python
import functools
import math
import types

import jax
import jax.numpy as jnp
from jax import lax
from jax.experimental import pallas as pl
from jax.experimental.pallas import tpu as pltpu

F32 = jnp.float32
BF16 = jnp.bfloat16

NORM_EPS = 1e-6
GN_EPS = 64e-5
HEAD = 64
CHUNK = 64
UNIT = 256
UNIT_HEADS = UNIT // HEAD
assert CHUNK == HEAD
WKV_CHUNKS = 2
WKV_GROUP = 8
WKV_LAG = 10
EXP_M05 = math.exp(-0.5)

NSA_HEADS = 16
NSA_GROUPS = 4
NSA_REP = NSA_HEADS // NSA_GROUPS
HD = 128
N_BRANCH = 3
CMP_BLOCK = 32
CMP_STRIDE = 16
SEL_BLOCK = 64
SEL_TOP_N = 16
WINDOW = 512
QB = 128
NSA_GROUPS_PER_STEP = 2
LOG2E = math.log2(math.e)
ROPE_DIM = HD // 4
ROPE_THETA = 500000.0
NEG = -1e30
FORCE_BONUS = 1e4
SEL_TILE = 512
WIN_BAND = WINDOW + 128

VMEM_LIMIT = 56 * 1024 * 1024


def _dot(a, b):
    return jnp.dot(a, b, preferred_element_type=F32)


def _dot_nt(a, b):
    return lax.dot_general(a, b, (((1,), (1,)), ((), ())), preferred_element_type=F32)


def _dot_tn(a, b):
    return lax.dot_general(a, b, (((0,), (0,)), ((), ())), preferred_element_type=F32)


def _split3(x):
    h = x.astype(BF16)
    r1 = x - h.astype(F32)
    m = r1.astype(BF16)
    l = (r1 - m.astype(F32)).astype(BF16)
    return h, m, l


def _sigmoid(x):
    return 1.0 / (1.0 + jnp.exp(-x))


def _silu(x):
    return x * _sigmoid(x)


def _mm_body(*refs, has_res):
    if has_res:
        a_ref, w_ref, r_ref, o_ref = refs
    else:
        a_ref, w_ref, o_ref = refs
    acc = _dot(a_ref[...].astype(BF16), w_ref[...].astype(BF16))
    if has_res:
        acc = acc + r_ref[...]
    o_ref[...] = acc.astype(o_ref.dtype)


def matmul(a, w, res=None, out_dtype=F32, tm=1024, tn=1024):
    squeeze = a.ndim == 2
    if squeeze:
        a, w = a[None], w[None]
        res = None if res is None else res[None]
    G, K, N = w.shape
    M = a.shape[1]
    tm = min(tm, M)
    tn = min(tn, N)
    assert M % tm == 0 and N % tn == 0, (M, N, tm, tn)
    in_specs = [pl.BlockSpec((None, tm, K), lambda g, i, j: (g, i, 0)),
                pl.BlockSpec((None, K, tn), lambda g, i, j: (g, 0, j))]
    args = [a, w]
    if res is not None:
        in_specs.append(pl.BlockSpec((None, tm, tn), lambda g, i, j: (g, i, j)))
        args.append(res)
    out = pl.pallas_call(
        functools.partial(_mm_body, has_res=res is not None),
        out_shape=jax.ShapeDtypeStruct((G, M, N), out_dtype),
        grid=(G, M // tm, N // tn),
        in_specs=in_specs,
        out_specs=pl.BlockSpec((None, tm, tn), lambda g, i, j: (g, i, j)),
        compiler_params=pltpu.CompilerParams(
            dimension_semantics=("parallel", "parallel", "arbitrary"), vmem_limit_bytes=VMEM_LIMIT),
        name="matmul",
    )(*args)
    return out[0] if squeeze else out


def _lora_out_body(mid_ref, b_ref, bias_ref, o_ref):
    o_ref[...] = _dot(mid_ref[...], b_ref[...]) + bias_ref[...]


def lora_out(mid, b_w, bias, tm=1024):
    L, N, R = mid.shape
    D = b_w.shape[2]
    tm = min(tm, N)
    return pl.pallas_call(
        _lora_out_body,
        out_shape=jax.ShapeDtypeStruct((L, N, D), F32),
        grid=(L, N // tm),
        in_specs=[pl.BlockSpec((None, tm, R), lambda n, i: (n, i, 0)),
                  pl.BlockSpec((None, R, D), lambda n, i: (n, 0, 0)),
                  pl.BlockSpec((None, 1, D), lambda n, i: (n, 0, 0))],
        out_specs=pl.BlockSpec((None, tm, D), lambda n, i: (n, i, 0)),
        compiler_params=pltpu.CompilerParams(
            dimension_semantics=("parallel", "arbitrary"), vmem_limit_bytes=VMEM_LIMIT),
        name="lora_out",
    )(mid, b_w, bias)


def _expand(x, lane_head):
    return jnp.concatenate([jnp.where(lane_head == h, x, 0.0) for h in range(UNIT_HEADS)], axis=0)


def _segsums(xs, ones_bd):
    pieces = []
    for x in xs:
        h = x.astype(BF16)
        pieces += [h, (x - h.astype(F32)).astype(BF16)]
    out = _dot(jnp.concatenate(pieces, axis=0), ones_bd)
    n = xs[0].shape[0]
    return [out[2 * i * n:(2 * i + 1) * n] + out[(2 * i + 1) * n:(2 * i + 2) * n] for i in range(len(xs))]


def _wkv_unit_stages(r, k, v, z, wl, al, vl, vf, k_k, k_a, r_k, ln_g, ln_b, s_ref, o_ref, consts, turn):
    state_written, chunk_idx = turn
    tri, ones_bd, same, strict, incl, lane_head, eye = consts
    bf = lambda x: x.astype(BF16)
    ex = lambda x: _expand(x, lane_head)

    a = _sigmoid(al[...])
    lw = -EXP_M05 * _sigmoid(wl[...])
    v = v[...]
    if vl is not None:
        v = v + (vf[...] - v) * _sigmoid(vl[...])
    k = k[...]
    r = r[...]
    kk = k * k_k[...]
    k2 = k * (1.0 + (a - 1.0) * k_a[...])
    n2, bonus = _segsums([kk * kk, r * k2 * r_k[...]], ones_bd)
    yield
    lh, lm, ll = _split3(lw)
    cum = _dot(tri, lh) + _dot(tri, lm) + _dot(tri, ll)
    yield

    kk = kk / jnp.maximum(jnp.sqrt(n2), 1e-12)
    p_incl = jnp.exp(cum)
    p_inv = jnp.exp(-cum)
    kt = k2 * p_inv
    bt = kk * a * p_inv
    at = -kk * jnp.exp(cum - lw)
    rt = r * p_incl
    eb = bf(ex(bt))
    ek = bf(ex(kt))
    ev = bf(ex(v))
    yield

    top, bot = slice(0, CHUNK), slice(CHUNK, 2 * CHUNK)
    ar = bf(jnp.concatenate([at, rt], axis=0))
    sb = _dot_nt(ar, eb)
    yield
    sk = _dot_nt(ar, ek)
    yield
    a_ab = jnp.where(strict, sb[top], 0.0)
    a_rb = bf(jnp.where(incl, sb[bot], 0.0))
    kv2 = _dot(bf(jnp.concatenate([jnp.where(strict, sk[top], 0.0), jnp.where(incl, sk[bot], 0.0)], axis=0)), ev)
    akv, rkv = kv2[top], kv2[bot]
    yield

    tm = eye + a_ab
    pw = _dot(bf(a_ab), bf(ex(a_ab)))
    yield
    for _ in range(4):
        both = _dot(bf(jnp.concatenate([pw, tm], axis=0)), bf(ex(pw)))
        pw, tm = both[top], tm + both[bot]
        yield
    tm = tm + _dot(bf(tm), bf(ex(pw)))
    yield

    assert len(state_written) == chunk_idx, "stage skew too small: state read before the previous chunk's update"
    s = s_ref[...]
    xs = _dot_nt(ar, bf(s))
    yield
    u = _dot(bf(tm), bf(ex(akv + xs[top])))
    yield
    o = xs[bot] + _dot(a_rb, bf(ex(u))) + rkv
    yield
    upd = _dot_tn(bf(jnp.concatenate([v, u], axis=0)), bf(jnp.concatenate([kt, bt], axis=0)))
    yield
    s_ref[...] = (s + jnp.where(same, upd, 0.0)) * p_incl[CHUNK - 1:CHUNK, :]
    state_written.append(chunk_idx)
    mean = _segsums([o], ones_bd)[0] * (1.0 / HEAD)
    yield

    oc = o - mean
    var = _segsums([oc * oc], ones_bd)[0] * (1.0 / HEAD)
    yield

    o_ref[...] = ((oc * lax.rsqrt(var + GN_EPS) * ln_g[...] + ln_b[...] + bonus * v)
                  * _silu(z[...])).astype(o_ref.dtype)


def _run_skewed(gens, group, lag):
    done = [False] * len(gens)
    rnd = 0
    while not all(done):
        for u, g in enumerate(gens):
            if (u // group) * lag <= rnd and not done[u]:
                try:
                    next(g)
                except StopIteration:
                    done[u] = True
        rnd += 1


def _wkv_body(*refs, has_vres, units):
    if has_vres:
        (r_ref, k_ref, v_ref, z_ref, wl_ref, al_ref, vl_ref, vf_ref,
         kk_ref, ka_ref, rk_ref, lg_ref, lb_ref, o_ref, s_ref) = refs
    else:
        (r_ref, k_ref, v_ref, z_ref, wl_ref, al_ref,
         kk_ref, ka_ref, rk_ref, lg_ref, lb_ref, o_ref, s_ref) = refs
        vl_ref = vf_ref = None

    @pl.when(pl.program_id(2) == 0)
    def _():
        s_ref[...] = jnp.zeros_like(s_ref)

    row = lax.broadcasted_iota(jnp.int32, (UNIT, UNIT), 0)
    col = lax.broadcasted_iota(jnp.int32, (UNIT, UNIT), 1)
    same = (row >> 6) == (col >> 6)
    ones_bd = jnp.where(same, 1.0, 0.0).astype(BF16)
    tr = lax.broadcasted_iota(jnp.int32, (CHUNK, CHUNK), 0)
    tc = lax.broadcasted_iota(jnp.int32, (CHUNK, CHUNK), 1)
    tri = jnp.where(tc <= tr, 1.0, 0.0).astype(BF16)
    lane = lax.broadcasted_iota(jnp.int32, (CHUNK, UNIT), 1)
    lane_head = lane >> 6
    t_row = lax.broadcasted_iota(jnp.int32, (CHUNK, UNIT), 0)
    s_lane = lane & (CHUNK - 1)
    strict = s_lane < t_row
    incl = s_lane <= t_row
    eye = jnp.where(s_lane == t_row, 1.0, 0.0).astype(F32)
    consts = (tri, ones_bd, same, strict, incl, lane_head, eye)

    def view(ref, cc, uu):
        if ref is None:
            return None
        rows = pl.ds(cc * CHUNK, CHUNK) if ref.shape[0] > 1 else slice(None)
        return ref.at[rows, pl.ds(uu * UNIT, UNIT)]

    written = [[] for _ in range(units)]
    _run_skewed([
        _wkv_unit_stages(*(view(ref, cc, uu) for ref in (r_ref, k_ref, v_ref, z_ref, wl_ref, al_ref, vl_ref, vf_ref,
                                                         kk_ref, ka_ref, rk_ref, lg_ref, lb_ref)),
                         s_ref.at[uu], view(o_ref, cc, uu), consts, (written[uu], cc))
        for cc in range(WKV_CHUNKS) for uu in range(units)], WKV_GROUP, WKV_LAG)


def wkv(rkvz, lor, v_first, k_k, k_a, r_k, ln_g, ln_b, batch, units=8):
    _, N, D = rkvz.shape
    T = N // batch
    step_rows = WKV_CHUNKS * CHUNK
    nc = T // step_rows
    W = units * UNIT
    has_vres = v_first is not None
    row_map = lambda b, u, c: (b * nc + c, u)

    def lead(n):
        return pl.BlockSpec((None, step_rows, W), lambda b, u, c: (n, b * nc + c, u))

    in_specs = [lead(0), lead(1), lead(2), lead(3), lead(0), lead(1)]
    args = [rkvz, rkvz, rkvz, rkvz, lor, lor]
    if has_vres:
        in_specs += [lead(2), lead(2)]
        args += [lor, v_first]
    par = pl.BlockSpec((1, W), lambda b, u, c: (0, u))
    in_specs += [par] * 5
    args += [k_k.reshape(1, D), k_a.reshape(1, D), r_k.reshape(1, D), ln_g.reshape(1, D), ln_b.reshape(1, D)]
    return pl.pallas_call(
        functools.partial(_wkv_body, has_vres=has_vres, units=units),
        out_shape=jax.ShapeDtypeStruct((N, D), BF16),
        grid=(batch, D // W, nc),
        in_specs=in_specs,
        out_specs=pl.BlockSpec((step_rows, W), row_map),
        scratch_shapes=[pltpu.VMEM((units, UNIT, UNIT), F32)],
        compiler_params=pltpu.CompilerParams(
            dimension_semantics=("parallel", "parallel", "arbitrary"), vmem_limit_bytes=VMEM_LIMIT),
        name="wkv7",
    )(*args)


def _attend(s, v_aug):
    m = jnp.max(s, axis=1, keepdims=True)
    acc = _dot(jnp.exp2(s - m).astype(BF16), v_aug)
    return acc[:, :HD] * (1.0 / acc[:, HD:HD + 1])


def _rope(x, c, s1, s2):
    n = x.shape[1] // HD
    if n > 1:
        c, s1, s2 = (jnp.concatenate([t] * n, axis=1) for t in (c, s1, s2))
    half = ROPE_DIM // 2
    return x * c + pltpu.roll(x, x.shape[1] - half, 1) * s1 + pltpu.roll(x, half, 1) * s2


def _tile_rows(t):
    return pl.ds(pl.multiple_of(t * SEL_TILE, SEL_TILE), SEL_TILE)


def _nsa_scores(c, t, dst):
    dst[...] = _dot_nt(c.qa[...], c.ks[_tile_rows(t), :])


def _nsa_absorb(c, t, s):
    m = c.m[...]
    m_new = jnp.maximum(m, jnp.max(s, axis=1, keepdims=True))
    pv = _dot(jnp.exp2(s - m_new).astype(BF16), c.vs[_tile_rows(t), :])
    c.acc[...] = jnp.exp2(m - m_new) * c.acc[...] + pv
    c.m[...] = m_new


def _nsa_pair(c, t):
    _nsa_scores(c, t + 1, c.s1)
    yield
    _nsa_absorb(c, t, c.s0[...])
    yield
    _nsa_scores(c, t + 2, c.s0)
    yield
    _nsa_absorb(c, t + 1, c.s1[...])
    yield


def _nsa_first_of_odd(c):
    _nsa_scores(c, 1, c.s0)
    yield
    _nsa_absorb(c, 0, c.s1[...])
    yield


def _add_rows(s, bias):
    return jnp.concatenate([s[r * QB:(r + 1) * QB] + bias for r in range(NSA_REP)], axis=0)


def _nsa_head(c, i, tq_col, tabs, ov_t, cmp_bias, win_bias):
    rows = NSA_REP * QB
    q4 = _rope(c.q[...], *tabs) * (HD ** -0.5 * LOG2E)
    qs = jnp.concatenate([q4[:, r * HD:(r + 1) * HD] for r in range(NSA_REP)], axis=0)
    qb = qs.astype(BF16)

    band = WIN_BAND
    w0 = pl.multiple_of(jnp.maximum((i + 1) * QB - band, 0), QB)
    s_cmp = _dot_nt(qb, c.kc[...])
    yield
    s_win = _dot_nt(qb, c.kw[pl.ds(w0, band), :])
    yield

    s_cmp = _add_rows(s_cmp, cmp_bias)
    e = jnp.exp2(s_cmp - jnp.max(s_cmp, axis=1, keepdims=True))
    p = e * (jnp.where(tq_col >= CMP_BLOCK - 1, 1.0, 0.0) / jnp.sum(e, axis=1, keepdims=True))
    c.o_cmp = _dot(p.astype(BF16), c.vc[...])
    yield

    psum = p[0:QB]
    for r in range(1, NSA_REP):
        psum = psum + p[r * QB:(r + 1) * QB]
    ph, pm, pl_ = _split3(psum)
    n_blk = ov_t.shape[0]
    imp_t = _dot_nt(ov_t, ph) + _dot_nt(ov_t, pm) + _dot_nt(ov_t, pl_)
    yield

    c.o_win = _attend(_add_rows(s_win, win_bias), c.vw[pl.ds(w0, band), :])
    yield

    jsub = lax.broadcasted_iota(jnp.int32, (n_blk, QB), 0)
    cur = (i * QB + lax.broadcasted_iota(jnp.int32, (n_blk, QB), 1)) >> (SEL_BLOCK.bit_length() - 1)
    forced = (jsub == 0) | (jsub == cur) | (jsub == cur - 1)
    imp_t = jnp.where(forced, FORCE_BONUS, jnp.where(jsub > cur, NEG, imp_t))
    sub8 = lax.broadcasted_iota(jnp.int32, (8, QB), 0)
    grp = [imp_t[8 * a:8 * a + 8] for a in range(n_blk // 8)]
    cnt = [jnp.zeros((8, QB), F32) for _ in grp]
    for jp in range(n_blk):
        rowv = jnp.broadcast_to(imp_t[jp:jp + 1, :], (8, QB))
        for a, x in enumerate(grp):
            ge = lambda: jnp.where(rowv >= x, 1.0, 0.0)
            gt = lambda: jnp.where(rowv > x, 1.0, 0.0)
            if 8 * a > jp:
                inc = ge()
            elif 8 * a + 7 <= jp:
                inc = gt()
            else:
                inc = jnp.where(sub8 + 8 * a > jp, ge(), gt())
            cnt[a] = cnt[a] + inc
    cnt = jnp.concatenate(cnt, axis=0)
    bias_t = jnp.where(cnt < float(min(SEL_TOP_N, n_blk)), 0.0, NEG).astype(BF16)
    qi = lax.broadcasted_iota(jnp.int32, (QB, QB), 0)
    qj = lax.broadcasted_iota(jnp.int32, (QB, QB), 1)
    eye_q = jnp.where(qi == qj, 1.0, 0.0).astype(BF16)
    bias = _dot_nt(eye_q, bias_t)
    yield
    bias4 = jnp.concatenate([bias] * NSA_REP, axis=0).astype(BF16)
    pad = c.ks.shape[1] - HD - n_blk
    parts = [qb, bias4] + ([jnp.zeros((rows, pad), BF16)] if pad else [])
    qa = jnp.concatenate(parts, axis=1)

    c.qa[...] = qa
    s_first = _dot_nt(qa, c.ks[_tile_rows(0), :])
    c.s0[...] = s_first
    c.s1[...] = s_first
    c.m[...] = jnp.full(c.m.shape, NEG, F32)
    c.acc[...] = jnp.zeros(c.acc.shape, F32)
    yield


def _nsa_tail(c, t_last, diag_bias):
    _nsa_absorb(c, t_last, _add_rows(c.s0[...], diag_bias))
    yield
    acc = c.acc[...]
    o_sel = acc[:, :HD] * (1.0 / acc[:, HD:HD + 1])
    g = _sigmoid(c.g[...] + c.gb[...])
    outs = []
    for r in range(NSA_REP):
        rs = slice(r * QB, (r + 1) * QB)
        outs.append(g[:, 3 * r:3 * r + 1] * c.o_cmp[rs] + g[:, 3 * r + 1:3 * r + 2] * o_sel[rs]
                    + g[:, 3 * r + 2:3 * r + 3] * c.o_win[rs])
    o4 = jnp.concatenate(outs, axis=1)
    c.o[...] = (o4 * _silu(c.z[...])).astype(c.o.dtype)


def _nsa_body(q_ref, z_ref, g_ref, gb_ref, rc_ref, rs1_ref, rs2_ref, ovt_ref,
              kc_ref, vc_ref, ks_ref, vs_ref, kw_ref, vw_ref, o_ref, qa_scr, s0_scr, s1_scr, m_scr, acc_scr):
    i = pl.program_id(2)
    rows = NSA_REP * QB
    W = NSA_REP * HD
    n_grp = kc_ref.shape[0]
    grps = []
    for g in range(n_grp):
        lanes = pl.ds(g * W, W)
        glanes = pl.ds(g * 128, 128)
        grps.append(types.SimpleNamespace(
            q=q_ref.at[:, lanes], z=z_ref.at[:, lanes], g=g_ref.at[:, glanes], gb=gb_ref.at[:, glanes],
            kc=kc_ref.at[g], vc=vc_ref.at[g], ks=ks_ref.at[g], vs=vs_ref.at[g], kw=kw_ref.at[g], vw=vw_ref.at[g],
            o=o_ref.at[:, lanes], qa=qa_scr.at[g], s0=s0_scr.at[g], s1=s1_scr.at[g], m=m_scr.at[g], acc=acc_scr.at[g]))
    lock = lambda gens: _run_skewed(gens, n_grp, 0)

    tq_col = i * QB + (lax.broadcasted_iota(jnp.int32, (rows, 1), 0) & (QB - 1))
    tabs = (rc_ref[...], rs1_ref[...], rs2_ref[...])

    t_last = (i * QB) // SEL_TILE
    odd = (t_last & 1) == 1
    tq = i * QB + lax.broadcasted_iota(jnp.int32, (QB, 1), 0)
    ncmp = kc_ref.shape[1]
    cend = lax.broadcasted_iota(jnp.int32, (QB, ncmp), 1) * CMP_STRIDE + (CMP_BLOCK - 1)
    cmp_bias = jnp.where(cend <= tq, 0.0, NEG)
    w0 = jnp.maximum((i + 1) * QB - WIN_BAND, 0)
    diff = (tq - w0) - lax.broadcasted_iota(jnp.int32, (QB, WIN_BAND), 1)
    win_bias = jnp.where(lax.bitcast_convert_type(diff, jnp.uint32) < jnp.uint32(WINDOW), 0.0, NEG)
    kpos = t_last * SEL_TILE + lax.broadcasted_iota(jnp.int32, (QB, SEL_TILE), 1)
    diag_bias = jnp.where(kpos <= tq, 0.0, NEG)

    lock([_nsa_head(c, i, tq_col, tabs, ovt_ref[...], cmp_bias, win_bias) for c in grps])

    @pl.when(odd)
    def _():
        lock([_nsa_first_of_odd(c) for c in grps])

    def pair(t2, carry):
        lock([_nsa_pair(c, 2 * t2 + (t_last & 1)) for c in grps])
        return carry

    lax.fori_loop(0, t_last // 2, pair, 0)
    lock([_nsa_tail(c, t_last, diag_bias) for c in grps])


def nsa_attention(proj, gate_b, rope_q, overlap, shared, batch):
    N = proj.shape[0]
    T = N // batch
    nq = T // QB
    W = NSA_REP * HD
    k_cmp, v_cmp, k_sel, v_sel, k_win, v_win = shared
    rc, rs1, rs2 = rope_q
    ncmp = k_cmp.shape[2]
    n_blk = T // SEL_BLOCK
    P = NSA_GROUPS_PER_STEP
    steps = NSA_GROUPS // P
    rows = NSA_REP * QB
    row_blk = lambda b, g, i: (b * nq + i, g)
    kv = lambda w: pl.BlockSpec((None, P, T, w), lambda b, g, i: (b, g, 0, 0))
    cm = pl.BlockSpec((None, P, ncmp, HD), lambda b, g, i: (b, g, 0, 0))
    rope = pl.BlockSpec((QB, HD), lambda b, g, i: (i, 0))
    gate_blk0 = 2 * NSA_HEADS * HD // (P * 128)
    return pl.pallas_call(
        _nsa_body,
        out_shape=jax.ShapeDtypeStruct((N, NSA_HEADS * HD), BF16),
        grid=(batch, steps, nq),
        in_specs=[pl.BlockSpec((QB, P * W), row_blk),
                  pl.BlockSpec((QB, P * W), lambda b, g, i: (b * nq + i, steps + g)),
                  pl.BlockSpec((QB, P * 128), lambda b, g, i: (b * nq + i, gate_blk0 + g)),
                  pl.BlockSpec((1, P * 128), lambda b, g, i: (0, g)),
                  rope, rope, rope,
                  pl.BlockSpec((n_blk, ncmp), lambda b, g, i: (0, 0)),
                  cm, cm, kv(k_sel.shape[3]), kv(v_sel.shape[3]), kv(k_win.shape[3]), kv(v_win.shape[3])],
        out_specs=pl.BlockSpec((QB, P * W), row_blk),
        scratch_shapes=[pltpu.VMEM((P, rows, k_sel.shape[3]), BF16),
                        pltpu.VMEM((P, rows, SEL_TILE), F32),
                        pltpu.VMEM((P, rows, SEL_TILE), F32),
                        pltpu.VMEM((P, rows, 1), F32),
                        pltpu.VMEM((P, rows, v_sel.shape[3]), F32)],
        compiler_params=pltpu.CompilerParams(
            dimension_semantics=("parallel", "parallel", "arbitrary"), vmem_limit_bytes=VMEM_LIMIT),
        name="nsa_attention",
    )(proj, proj, proj, gate_b, rc, rs1, rs2, overlap, k_cmp, v_cmp, k_sel, v_sel, k_win, v_win)


def _rms(x, g):
    return x * lax.rsqrt(jnp.mean(x * x, axis=-1, keepdims=True) + NORM_EPS) * g


def _norm_body(x_ref, g_ref, o_ref):
    o_ref[...] = _rms(x_ref[...], g_ref[...]).astype(o_ref.dtype)


def rmsnorm(h, g, out_dtype, tm=512):
    N, D = h.shape
    return pl.pallas_call(
        _norm_body,
        out_shape=jax.ShapeDtypeStruct((N, D), out_dtype),
        grid=(N // tm,),
        in_specs=[pl.BlockSpec((tm, D), lambda i: (i, 0)), pl.BlockSpec((1, D), lambda i: (0, 0))],
        out_specs=pl.BlockSpec((tm, D), lambda i: (i, 0)),
        compiler_params=pltpu.CompilerParams(dimension_semantics=("parallel",)),
        name="rmsnorm",
    )(h, g.reshape(1, D))


def _mix_body(x_ref, xp_ref, g_ref, mu_ref, a_ref, o_ref, mid_ref, *, seq, lora_src):
    tm = x_ref.shape[0]
    g = g_ref[...]
    u = _rms(x_ref[...], g)
    last_prev = _rms(xp_ref[...], g)[xp_ref.shape[0] - 1:, :]
    first = lax.rem(pl.program_id(0) * tm, seq) == 0
    row = lax.broadcasted_iota(jnp.int32, (tm, 1), 0)
    prev = jnp.where(row == 0, jnp.where(first, 0.0, last_prev), pltpu.roll(u, 1, 0))
    xx = prev - u
    mix = lambda n: (u + xx * mu_ref[n:n + 1, :]).astype(BF16)
    for n in range(o_ref.shape[0]):
        o_ref[n] = mix(n)
    for l, n in enumerate(lora_src):
        mid = _dot(mix(n), a_ref[l])
        mid_ref[l] = (jnp.tanh(mid) if l == 0 else mid).astype(mid_ref.dtype)


def rwkv_mix(h, g, mu, a_w, lora_src, n_out, seq, tm=256):
    N, D = h.shape
    L, _, R = a_w.shape
    assert seq % tm == 0
    sub = 8
    return pl.pallas_call(
        functools.partial(_mix_body, seq=seq, lora_src=tuple(lora_src)),
        out_shape=(jax.ShapeDtypeStruct((n_out, N, D), BF16), jax.ShapeDtypeStruct((L, N, R), BF16)),
        grid=(N // tm,),
        in_specs=[pl.BlockSpec((tm, D), lambda i: (i, 0)),
                  pl.BlockSpec((sub, D), lambda i: (jnp.maximum(i * (tm // sub) - 1, 0), 0)),
                  pl.BlockSpec((1, D), lambda i: (0, 0)),
                  pl.BlockSpec(mu.shape, lambda i: (0, 0)),
                  pl.BlockSpec((L, D, R), lambda i: (0, 0, 0))],
        out_specs=(pl.BlockSpec((n_out, tm, D), lambda i: (0, i, 0)), pl.BlockSpec((L, tm, R), lambda i: (0, i, 0))),
        compiler_params=pltpu.CompilerParams(dimension_semantics=("parallel",), vmem_limit_bytes=VMEM_LIMIT),
        name="rwkv_mix",
    )(h, h, g.reshape(1, D), mu, a_w)


def _kvpost_body(kv_ref, rc_ref, rs1_ref, rs2_ref, pek_ref, pev_ref, cx_ref, ks_ref, vs_ref, kw_ref, vw_ref):
    tt = kv_ref.shape[0]
    aug = ks_ref.shape[-1] - HD
    tok = pl.program_id(1) * tt + lax.broadcasted_iota(jnp.int32, (tt, aug), 0)
    lane = lax.broadcasted_iota(jnp.int32, (tt, aug), 1)
    onehot = jnp.where(lane == (tok >> (SEL_BLOCK.bit_length() - 1)), 1.0, 0.0).astype(BF16)
    ones_col = jnp.where(lax.broadcasted_iota(jnp.int32, (tt, vs_ref.shape[-1] - HD), 1) == 0, 1.0, 0.0).astype(BF16)
    tabs = (rc_ref[...], rs1_ref[...], rs2_ref[...])
    G = NSA_GROUPS
    n16 = tt // CMP_STRIDE
    for g in range(G):
        for j, pe_ref in ((0, pek_ref), (1, pev_ref)):
            x3 = kv_ref[:, (j * G + g) * HD:(j * G + g + 1) * HD].reshape(n16, CMP_STRIDE, HD)
            x16 = jnp.concatenate([x3[:, l, :] for l in range(CMP_STRIDE)], axis=1)
            cx_ref[j, g] = (x16 + pe_ref[0:1, :]).astype(BF16)
            cx_ref[2 + j, g] = (x16 + pe_ref[1:2, :]).astype(BF16)
        k_s, v_s, k_w, v_w = [kv_ref[:, (j * G + g) * HD:(j * G + g + 1) * HD] for j in range(2, 6)]
        ks_ref[g] = jnp.concatenate([_rope(k_s, *tabs).astype(BF16), onehot], axis=1)
        vs_ref[g] = jnp.concatenate([v_s.astype(BF16), ones_col], axis=1)
        kw_ref[g] = _rope(k_w, *tabs).astype(BF16)
        vw_ref[g] = jnp.concatenate([v_w.astype(BF16), ones_col], axis=1)


def kv_post(kv, rope_t, cmp_pe, batch, tt=512):
    N = kv.shape[0]
    T = N // batch
    G = NSA_GROUPS
    nt = T // tt
    n_blk = T // SEL_BLOCK
    aug = -(-n_blk // 128) * 128
    half = CMP_STRIDE * HD
    pe = cmp_pe.reshape(2, 2, half)
    tab = pl.BlockSpec((tt, HD), lambda b, i: (i, 0))
    pes = pl.BlockSpec((2, half), lambda b, i: (0, 0))
    out4 = lambda w: pl.BlockSpec((None, G, tt, w), lambda b, i: (b, 0, i, 0))
    sds = lambda w: jax.ShapeDtypeStruct((batch, G, T, w), BF16)
    n16 = tt // CMP_STRIDE
    return pl.pallas_call(
        _kvpost_body,
        out_shape=(jax.ShapeDtypeStruct((4, batch, G, T // CMP_STRIDE, half), BF16),
                   sds(HD + aug), sds(2 * HD), sds(HD), sds(2 * HD)),
        grid=(batch, nt),
        in_specs=[pl.BlockSpec((tt, kv.shape[1]), lambda b, i: (b * nt + i, 0)), tab, tab, tab, pes, pes],
        out_specs=(pl.BlockSpec((4, None, G, n16, half), lambda b, i: (0, b, 0, i, 0)),
                   out4(HD + aug), out4(2 * HD), out4(HD), out4(2 * HD)),
        compiler_params=pltpu.CompilerParams(dimension_semantics=("parallel", "parallel"), vmem_limit_bytes=VMEM_LIMIT),
        name="kv_post",
    )(kv, *rope_t, pe[0], pe[1])


def _cmp2_body(lo_ref, hi_ref, w2_ref, rc_ref, rs1_ref, rs2_ref, o_ref):
    n = lo_ref.shape[0]
    hid = lo_ref[...] + pltpu.roll(hi_ref[...], n - 1, 0)
    out = _dot(_silu(hid).astype(BF16), w2_ref[...])
    roped = _rope(out, rc_ref[...], rs1_ref[...], rs2_ref[...])
    o_ref[...] = jnp.where(pl.program_id(0) == 0, roped, out).astype(o_ref.dtype)


def cmp_mlp2(hd, w2, rope_c, groups, ng):
    H = hd.shape[2]
    tab = pl.BlockSpec((ng, HD), lambda j, m: (0, 0))
    return pl.pallas_call(
        _cmp2_body,
        out_shape=jax.ShapeDtypeStruct((2, groups, ng, HD), BF16),
        grid=(2, groups),
        in_specs=[pl.BlockSpec((None, ng, H), lambda j, m: (j, m, 0)),
                  pl.BlockSpec((None, ng, H), lambda j, m: (2 + j, m, 0)),
                  pl.BlockSpec((None, H, HD), lambda j, m: (j, 0, 0)), tab, tab, tab],
        out_specs=pl.BlockSpec((None, None, ng, HD), lambda j, m: (j, m, 0, 0)),
        compiler_params=pltpu.CompilerParams(dimension_semantics=("parallel", "parallel")),
        name="cmp_mlp2",
    )(hd, hd, w2, *rope_c)


def _rope_tables(pos):
    half = ROPE_DIM // 2
    inv = ROPE_THETA ** (-jnp.arange(half, dtype=F32) / half)
    ang = pos.astype(F32)[:, None] * inv[None, :]
    cos, sin = jnp.cos(ang), jnp.sin(ang)
    n = pos.shape[0]
    c = jnp.concatenate([cos, cos, jnp.ones((n, HD - ROPE_DIM), F32)], axis=1)
    s1 = jnp.concatenate([-sin, jnp.zeros((n, HD - half), F32)], axis=1)
    s2 = jnp.concatenate([jnp.zeros((n, half), F32), sin, jnp.zeros((n, HD - ROPE_DIM), F32)], axis=1)
    return c, s1, s2


def _rwkv_layer(h, batch, v_first, norm_g, mu, w_rkvz, w0, w1, w2, a0, a1, a2, vres, k_k, k_a, r_k, ln_g, ln_b, w_o):
    N, D = h.shape
    loras = [(w1, w2, w0), (a1, a2, a0)]
    idx = [4, 5]
    if vres is not None:
        loras.append((vres[1], vres[2], vres[0]))
        idx.append(2)
    rpad = 128
    a_w = jnp.stack([jnp.pad(l[0], ((0, 0), (0, rpad - l[0].shape[1]))) for l in loras]).astype(BF16)
    b_w = jnp.stack([jnp.pad(l[1], ((0, rpad - l[1].shape[0]), (0, 0))) for l in loras]).astype(BF16)
    bias = jnp.stack([l[2].reshape(1, D) for l in loras])

    mixed, mid = rwkv_mix(h, norm_g, mu, a_w, idx, 4, N // batch)
    rkvz = matmul(mixed, w_rkvz.astype(BF16))
    lor = lora_out(mid, b_w, bias)

    o = wkv(rkvz, lor, v_first, k_k, k_a, r_k.reshape(-1), ln_g, ln_b, batch)
    return matmul(o, w_o.astype(BF16), res=h), rkvz


def _shared_kv(h, batch, rope_t, kv_norm_g, kv_w, cmp_pe, cmp_w1, cmp_w2):
    N, D = h.shape
    T = N // batch
    G = NSA_GROUPS
    kv = matmul(rmsnorm(h, kv_norm_g, BF16), kv_w.astype(BF16))
    cx, k_sel, v_sel, k_win, v_win = kv_post(kv, rope_t, cmp_pe, batch)

    ng = T // CMP_STRIDE
    half = CMP_STRIDE * HD
    w1 = jnp.concatenate([cmp_w1[:, :half], cmp_w1[:, half:]], axis=0).astype(BF16)
    hd = matmul(cx.reshape(4, batch * G * ng, half), w1)
    rope_c = _rope_tables(jnp.arange(ng) * CMP_STRIDE + CMP_BLOCK - 1)
    cmp = cmp_mlp2(hd, cmp_w2.astype(BF16), rope_c, batch * G, ng).reshape(2, batch, G, ng, HD)
    return cmp[0], cmp[1], k_sel, v_sel, k_win, v_win


def _nsa_layer(h, batch, shared, rope_t, overlap, norm_g, w_in, gate_b, w_o):
    N, D = h.shape
    G = NSA_GROUPS
    width = NSA_HEADS * HD
    ng = N_BRANCH * NSA_HEADS
    per = N_BRANCH * NSA_REP
    w_g = jnp.pad(w_in[:, width:width + ng].reshape(D, G, per), ((0, 0), (0, 0), (0, 128 - per))).reshape(D, G * 128)
    w_all = jnp.concatenate([w_in[:, :width], w_in[:, width + ng:], w_g], axis=1).astype(BF16)
    b_g = jnp.pad(gate_b.reshape(G, per), ((0, 0), (0, 128 - per))).reshape(1, G * 128)
    proj = matmul(rmsnorm(h, norm_g, BF16), w_all, tn=768)
    o = nsa_attention(proj, b_g, rope_t, overlap, shared, batch)
    return matmul(o, w_o.astype(BF16), res=h)


def kernel(x, a_norm_g, a_mu, a_w_rkvz, a_w0, a_w1, a_w2, a_a0, a_a1, a_a2, a_v0, a_v1, a_v2, a_k_k, a_k_a, a_r_k,
           a_ln_g, a_ln_b, a_w_o, kv_norm_g, kv_w, cmp_pe, cmp_w1, cmp_w2, b_norm_g, b_w_in, b_gate_b, b_w_o, final_g):
    B, T, D = x.shape
    N = B * T
    n_a = a_norm_g.shape[0]
    n_b = b_norm_g.shape[0]
    h = x.reshape(N, D)
    v_first = None
    for i in range(n_a):
        vres = None if i == 0 else (a_v0[i - 1], a_v1[i - 1], a_v2[i - 1])
        h, rkvz = _rwkv_layer(h, B, v_first, a_norm_g[i], a_mu[i], a_w_rkvz[i], a_w0[i], a_w1[i], a_w2[i],
                              a_a0[i], a_a1[i], a_a2[i], vres, a_k_k[i], a_k_a[i], a_r_k[i],
                              a_ln_g[i], a_ln_b[i], a_w_o[i])
        if i == 0:
            v_first = rkvz
    if n_b:
        rope_t = _rope_tables(jnp.arange(T))
        shared = _shared_kv(h, B, rope_t, kv_norm_g, kv_w, cmp_pe, cmp_w1, cmp_w2)
        ncmp = T // CMP_STRIDE
        n_blk = T // SEL_BLOCK
        cpos = jnp.arange(ncmp)[:, None] * CMP_STRIDE + jnp.arange(CMP_BLOCK)[None, :]
        overlap = jax.nn.one_hot(cpos // SEL_BLOCK, n_blk, dtype=F32).mean(axis=1).T.astype(BF16)
        for j in range(n_b):
            h = _nsa_layer(h, B, shared, rope_t, overlap, b_norm_g[j], b_w_in[j], b_gate_b[j], b_w_o[j])
    return rmsnorm(h, final_g, F32).reshape(B, T, D)
```

```python
import functools
import math
import types

import jax
import jax.numpy as jnp
from jax import lax
from jax.experimental import pallas as pl
from jax.experimental.pallas import tpu as pltpu

F32 = jnp.float32
BF16 = jnp.bfloat16

NORM_EPS = 1e-6
GN_EPS = 64e-5
HEAD = 64
CHUNK = 64
UNIT = 256
UNIT_HEADS = UNIT // HEAD
assert CHUNK == HEAD
WKV_CHUNKS = 4
WKV_GROUP = 8
WKV_LAG = 10
EXP_M05 = math.exp(-0.5)

NSA_HEADS = 16
NSA_GROUPS = 4
NSA_REP = NSA_HEADS // NSA_GROUPS
HD = 128
N_BRANCH = 3
CMP_BLOCK = 32
CMP_STRIDE = 16
SEL_BLOCK = 64
SEL_TOP_N = 16
WINDOW = 512
QB = 128
NSA_GROUPS_PER_STEP = 2
LOG2E = math.log2(math.e)
ROPE_DIM = HD // 4
ROPE_THETA = 500000.0
NEG = -1e30
FORCE_BONUS = 1e4
SEL_TILE = 512
WIN_BAND = WINDOW + 128

VMEM_LIMIT = 56 * 1024 * 1024


def _dot(a, b):
    return jnp.dot(a, b, preferred_element_type=F32)


def _dot_nt(a, b):
    return lax.dot_general(a, b, (((1,), (1,)), ((), ())), preferred_element_type=F32)


def _dot_tn(a, b):
    return lax.dot_general(a, b, (((0,), (0,)), ((), ())), preferred_element_type=F32)


def _split3(x):
    h = x.astype(BF16)
    r1 = x - h.astype(F32)
    m = r1.astype(BF16)
    l = (r1 - m.astype(F32)).astype(BF16)
    return h, m, l


def _sigmoid(x):
    return 1.0 / (1.0 + jnp.exp(-x))


def _silu(x):
    return x * _sigmoid(x)


def _mm_body(*refs, has_res):
    if has_res:
        a_ref, w_ref, r_ref, o_ref = refs
    else:
        a_ref, w_ref, o_ref = refs
    acc = _dot(a_ref[...].astype(BF16), w_ref[...].astype(BF16))
    if has_res:
        acc = acc + r_ref[...]
    o_ref[...] = acc.astype(o_ref.dtype)


def matmul(a, w, res=None, out_dtype=F32, tm=None, tn=1024):
    squeeze = a.ndim == 2
    if squeeze:
        a, w = a[None], w[None]
        res = None if res is None else res[None]
    G, K, N = w.shape
    M = a.shape[1]
    if tm is None:
        tm = 2048 if res is None and M % 2048 == 0 else 1024
    tm = min(tm, M)
    tn = min(tn, N)
    assert M % tm == 0 and N % tn == 0, (M, N, tm, tn)
    in_specs = [pl.BlockSpec((None, tm, K), lambda g, i, j: (g, i, 0)),
                pl.BlockSpec((None, K, tn), lambda g, i, j: (g, 0, j))]
    args = [a, w]
    if res is not None:
        in_specs.append(pl.BlockSpec((None, tm, tn), lambda g, i, j: (g, i, j)))
        args.append(res)
    out = pl.pallas_call(
        functools.partial(_mm_body, has_res=res is not None),
        out_shape=jax.ShapeDtypeStruct((G, M, N), out_dtype),
        grid=(G, M // tm, N // tn),
        in_specs=in_specs,
        out_specs=pl.BlockSpec((None, tm, tn), lambda g, i, j: (g, i, j)),
        compiler_params=pltpu.CompilerParams(
            dimension_semantics=("parallel", "parallel", "arbitrary"), vmem_limit_bytes=VMEM_LIMIT),
        name="matmul",
    )(*args)
    return out[0] if squeeze else out


def _lora_out_body(mid_ref, b_ref, bias_ref, o_ref):
    o_ref[...] = _dot(mid_ref[...], b_ref[...]) + bias_ref[...]


def lora_out(mid, b_w, bias, tm=1024):
    L, N, R = mid.shape
    D = b_w.shape[2]
    tm = min(tm, N)
    return pl.pallas_call(
        _lora_out_body,
        out_shape=jax.ShapeDtypeStruct((L, N, D), F32),
        grid=(L, N // tm),
        in_specs=[pl.BlockSpec((None, tm, R), lambda n, i: (n, i, 0)),
                  pl.BlockSpec((None, R, D), lambda n, i: (n, 0, 0)),
                  pl.BlockSpec((None, 1, D), lambda n, i: (n, 0, 0))],
        out_specs=pl.BlockSpec((None, tm, D), lambda n, i: (n, i, 0)),
        compiler_params=pltpu.CompilerParams(
            dimension_semantics=("parallel", "arbitrary"), vmem_limit_bytes=VMEM_LIMIT),
        name="lora_out",
    )(mid, b_w, bias)


def _expand(x, lane_head):
    return jnp.concatenate([jnp.where(lane_head == h, x, 0.0) for h in range(UNIT_HEADS)], axis=0)


def _segsums(xs, ones_bd):
    pieces = []
    for x in xs:
        h = x.astype(BF16)
        pieces += [h, (x - h.astype(F32)).astype(BF16)]
    out = _dot(jnp.concatenate(pieces, axis=0), ones_bd)
    n = xs[0].shape[0]
    return [out[2 * i * n:(2 * i + 1) * n] + out[(2 * i + 1) * n:(2 * i + 2) * n] for i in range(len(xs))]


def _wkv_unit_stages(r, k, v, z, wl, al, vl, vf, k_k, k_a, r_k, ln_g, ln_b, s_ref, o_ref, consts, turn):
    state_written, chunk_idx = turn
    tri, ones_bd, same, strict, incl, lane_head, eye = consts
    bf = lambda x: x.astype(BF16)
    ex = lambda x: _expand(x, lane_head)

    a = _sigmoid(al[...])
    lw = -EXP_M05 * _sigmoid(wl[...])
    v = v[...]
    if vl is not None:
        v = v + (vf[...] - v) * _sigmoid(vl[...])
    k = k[...]
    r = r[...]
    kk = k * k_k[...]
    k2 = k * (1.0 + (a - 1.0) * k_a[...])
    n2, bonus = _segsums([kk * kk, r * k2 * r_k[...]], ones_bd)
    yield
    lh, lm, ll = _split3(lw)
    cum = _dot(tri, lh) + _dot(tri, lm) + _dot(tri, ll)
    yield

    kk = kk / jnp.maximum(jnp.sqrt(n2), 1e-12)
    p_incl = jnp.exp(cum)
    p_inv = jnp.exp(-cum)
    kt = k2 * p_inv
    bt = kk * a * p_inv
    at = -kk * jnp.exp(cum - lw)
    rt = r * p_incl
    eb = bf(ex(bt))
    ek = bf(ex(kt))
    ev = bf(ex(v))
    yield

    top, bot = slice(0, CHUNK), slice(CHUNK, 2 * CHUNK)
    ar = bf(jnp.concatenate([at, rt], axis=0))
    sb = _dot_nt(ar, eb)
    yield
    sk = _dot_nt(ar, ek)
    yield
    a_ab = jnp.where(strict, sb[top], 0.0)
    a_rb = bf(jnp.where(incl, sb[bot], 0.0))
    kv2 = _dot(bf(jnp.concatenate([jnp.where(strict, sk[top], 0.0), jnp.where(incl, sk[bot], 0.0)], axis=0)), ev)
    akv, rkv = kv2[top], kv2[bot]
    yield

    tm = eye + a_ab
    pw = _dot(bf(a_ab), bf(ex(a_ab)))
    yield
    for _ in range(4):
        both = _dot(bf(jnp.concatenate([pw, tm], axis=0)), bf(ex(pw)))
        pw, tm = both[top], tm + both[bot]
        yield
    tm = tm + _dot(bf(tm), bf(ex(pw)))
    yield

    assert len(state_written) == chunk_idx, "stage skew too small: state read before the previous chunk's update"
    s = s_ref[...]
    xs = _dot_nt(ar, bf(s))
    yield
    u = _dot(bf(tm), bf(ex(akv + xs[top])))
    yield
    o = xs[bot] + _dot(a_rb, bf(ex(u))) + rkv
    yield
    upd = _dot_tn(bf(jnp.concatenate([v, u], axis=0)), bf(jnp.concatenate([kt, bt], axis=0)))
    yield
    s_ref[...] = (s + jnp.where(same, upd, 0.0)) * p_incl[CHUNK - 1:CHUNK, :]
    state_written.append(chunk_idx)
    mean = _segsums([o], ones_bd)[0] * (1.0 / HEAD)
    yield

    oc = o - mean
    var = _segsums([oc * oc], ones_bd)[0] * (1.0 / HEAD)
    yield

    o_ref[...] = ((oc * lax.rsqrt(var + GN_EPS) * ln_g[...] + ln_b[...] + bonus * v)
                  * _silu(z[...])).astype(o_ref.dtype)


def _run_skewed(gens, group, lag):
    done = [False] * len(gens)
    rnd = 0
    while not all(done):
        for u, g in enumerate(gens):
            if (u // group) * lag <= rnd and not done[u]:
                try:
                    next(g)
                except StopIteration:
                    done[u] = True
        rnd += 1


def _wkv_body(*refs, has_vres, units):
    if has_vres:
        (r_ref, k_ref, v_ref, z_ref, wl_ref, al_ref, vl_ref, vf_ref,
         kk_ref, ka_ref, rk_ref, lg_ref, lb_ref, o_ref, s_ref) = refs
    else:
        (r_ref, k_ref, v_ref, z_ref, wl_ref, al_ref,
         kk_ref, ka_ref, rk_ref, lg_ref, lb_ref, o_ref, s_ref) = refs
        vl_ref = vf_ref = None

    @pl.when(pl.program_id(2) == 0)
    def _():
        s_ref[...] = jnp.zeros_like(s_ref)

    row = lax.broadcasted_iota(jnp.int32, (UNIT, UNIT), 0)
    col = lax.broadcasted_iota(jnp.int32, (UNIT, UNIT), 1)
    same = (row >> 6) == (col >> 6)
    ones_bd = jnp.where(same, 1.0, 0.0).astype(BF16)
    tr = lax.broadcasted_iota(jnp.int32, (CHUNK, CHUNK), 0)
    tc = lax.broadcasted_iota(jnp.int32, (CHUNK, CHUNK), 1)
    tri = jnp.where(tc <= tr, 1.0, 0.0).astype(BF16)
    lane = lax.broadcasted_iota(jnp.int32, (CHUNK, UNIT), 1)
    lane_head = lane >> 6
    t_row = lax.broadcasted_iota(jnp.int32, (CHUNK, UNIT), 0)
    s_lane = lane & (CHUNK - 1)
    strict = s_lane < t_row
    incl = s_lane <= t_row
    eye = jnp.where(s_lane == t_row, 1.0, 0.0).astype(F32)
    consts = (tri, ones_bd, same, strict, incl, lane_head, eye)

    def view(ref, cc, uu):
        if ref is None:
            return None
        rows = pl.ds(cc * CHUNK, CHUNK) if ref.shape[0] > 1 else slice(None)
        return ref.at[rows, pl.ds(uu * UNIT, UNIT)]

    written = [[] for _ in range(units)]
    _run_skewed([
        _wkv_unit_stages(*(view(ref, cc, uu) for ref in (r_ref, k_ref, v_ref, z_ref, wl_ref, al_ref, vl_ref, vf_ref,
                                                         kk_ref, ka_ref, rk_ref, lg_ref, lb_ref)),
                         s_ref.at[uu], view(o_ref, cc, uu), consts, (written[uu], cc))
        for cc in range(WKV_CHUNKS) for uu in range(units)], WKV_GROUP, WKV_LAG)


def wkv(rkvz, lor, v_first, k_k, k_a, r_k, ln_g, ln_b, batch, units=8):
    _, N, D = rkvz.shape
    T = N // batch
    step_rows = WKV_CHUNKS * CHUNK
    nc = T // step_rows
    W = units * UNIT
    has_vres = v_first is not None
    row_map = lambda b, u, c: (b * nc + c, u)

    def lead(n):
        return pl.BlockSpec((None, step_rows, W), lambda b, u, c: (n, b * nc + c, u))

    in_specs = [lead(0), lead(1), lead(2), lead(3), lead(0), lead(1)]
    args = [rkvz, rkvz, rkvz, rkvz, lor, lor]
    if has_vres:
        in_specs += [lead(2), lead(2)]
        args += [lor, v_first]
    par = pl.BlockSpec((1, W), lambda b, u, c: (0, u))
    in_specs += [par] * 5
    args += [k_k.reshape(1, D), k_a.reshape(1, D), r_k.reshape(1, D), ln_g.reshape(1, D), ln_b.reshape(1, D)]
    return pl.pallas_call(
        functools.partial(_wkv_body, has_vres=has_vres, units=units),
        out_shape=jax.ShapeDtypeStruct((N, D), BF16),
        grid=(batch, D // W, nc),
        in_specs=in_specs,
        out_specs=pl.BlockSpec((step_rows, W), row_map),
        scratch_shapes=[pltpu.VMEM((units, UNIT, UNIT), F32)],
        compiler_params=pltpu.CompilerParams(
            dimension_semantics=("parallel", "parallel", "arbitrary"), vmem_limit_bytes=VMEM_LIMIT),
        name="wkv7",
    )(*args)


def _attend(s, v_aug):
    m = jnp.max(s, axis=1, keepdims=True)
    acc = _dot(jnp.exp2(s - m).astype(BF16), v_aug)
    return acc[:, :HD] * (1.0 / acc[:, HD:HD + 1])


def _rope(x, c, s1, s2):
    n = x.shape[1] // HD
    if n > 1:
        c, s1, s2 = (jnp.concatenate([t] * n, axis=1) for t in (c, s1, s2))
    half = ROPE_DIM // 2
    return x * c + pltpu.roll(x, x.shape[1] - half, 1) * s1 + pltpu.roll(x, half, 1) * s2


def _tile_rows(t):
    return pl.ds(pl.multiple_of(t * SEL_TILE, SEL_TILE), SEL_TILE)


def _nsa_scores(c, t, dst):
    dst[...] = _dot_nt(c.qa[...], c.ks[_tile_rows(t), :])


def _nsa_absorb(c, t, s):
    m = c.m[...]
    m_new = jnp.maximum(m, jnp.max(s, axis=1, keepdims=True))
    pv = _dot(jnp.exp2(s - m_new).astype(BF16), c.vs[_tile_rows(t), :])
    c.acc[...] = jnp.exp2(m - m_new) * c.acc[...] + pv
    c.m[...] = m_new


def _nsa_pair(c, t):
    _nsa_scores(c, t + 1, c.s1)
    yield
    _nsa_absorb(c, t, c.s0[...])
    yield
    _nsa_scores(c, t + 2, c.s0)
    yield
    _nsa_absorb(c, t + 1, c.s1[...])
    yield


def _nsa_first_of_odd(c):
    _nsa_scores(c, 1, c.s0)
    yield
    _nsa_absorb(c, 0, c.s1[...])
    yield


def _add_rows(s, bias):
    return jnp.concatenate([s[r * QB:(r + 1) * QB] + bias for r in range(NSA_REP)], axis=0)


def _nsa_head(c, i, tq_col, tabs, ov_t, cmp_bias, win_bias):
    rows = NSA_REP * QB
    q4 = _rope(c.q[...], *tabs) * (HD ** -0.5 * LOG2E)
    qs = jnp.concatenate([q4[:, r * HD:(r + 1) * HD] for r in range(NSA_REP)], axis=0)
    qb = qs.astype(BF16)

    band = WIN_BAND
    w0 = pl.multiple_of(jnp.maximum((i + 1) * QB - band, 0), QB)
    s_cmp = _dot_nt(qb, c.kc[...])
    yield
    s_win = _dot_nt(qb, c.kw[pl.ds(w0, band), :])
    yield

    s_cmp = _add_rows(s_cmp, cmp_bias)
    e = jnp.exp2(s_cmp - jnp.max(s_cmp, axis=1, keepdims=True))
    p = e * (jnp.where(tq_col >= CMP_BLOCK - 1, 1.0, 0.0) / jnp.sum(e, axis=1, keepdims=True))
    c.o_cmp = _dot(p.astype(BF16), c.vc[...])
    yield

    psum = p[0:QB]
    for r in range(1, NSA_REP):
        psum = psum + p[r * QB:(r + 1) * QB]
    ph, pm, pl_ = _split3(psum)
    n_blk = ov_t.shape[0]
    imp_t = _dot_nt(ov_t, ph) + _dot_nt(ov_t, pm) + _dot_nt(ov_t, pl_)
    yield

    c.o_win = _attend(_add_rows(s_win, win_bias), c.vw[pl.ds(w0, band), :])
    yield

    jsub = lax.broadcasted_iota(jnp.int32, (n_blk, QB), 0)
    cur = (i * QB + lax.broadcasted_iota(jnp.int32, (n_blk, QB), 1)) >> (SEL_BLOCK.bit_length() - 1)
    forced = (jsub == 0) | (jsub == cur) | (jsub == cur - 1)
    imp_t = jnp.where(forced, FORCE_BONUS, jnp.where(jsub > cur, NEG, imp_t))
    sub8 = lax.broadcasted_iota(jnp.int32, (8, QB), 0)
    grp = [imp_t[8 * a:8 * a + 8] for a in range(n_blk // 8)]
    cnt = [jnp.zeros((8, QB), F32) for _ in grp]
    for jp in range(n_blk):
        rowv = jnp.broadcast_to(imp_t[jp:jp + 1, :], (8, QB))
        for a, x in enumerate(grp):
            ge = lambda: jnp.where(rowv >= x, 1.0, 0.0)
            gt = lambda: jnp.where(rowv > x, 1.0, 0.0)
            if 8 * a > jp:
                inc = ge()
            elif 8 * a + 7 <= jp:
                inc = gt()
            else:
                inc = jnp.where(sub8 + 8 * a > jp, ge(), gt())
            cnt[a] = cnt[a] + inc
    cnt = jnp.concatenate(cnt, axis=0)
    bias_t = jnp.where(cnt < float(min(SEL_TOP_N, n_blk)), 0.0, NEG).astype(BF16)
    qi = lax.broadcasted_iota(jnp.int32, (QB, QB), 0)
    qj = lax.broadcasted_iota(jnp.int32, (QB, QB), 1)
    eye_q = jnp.where(qi == qj, 1.0, 0.0).astype(BF16)
    bias = _dot_nt(eye_q, bias_t)
    yield
    bias4 = jnp.concatenate([bias] * NSA_REP, axis=0).astype(BF16)
    pad = c.ks.shape[1] - HD - n_blk
    parts = [qb, bias4] + ([jnp.zeros((rows, pad), BF16)] if pad else [])
    qa = jnp.concatenate(parts, axis=1)

    c.qa[...] = qa
    s_first = _dot_nt(qa, c.ks[_tile_rows(0), :])
    c.s0[...] = s_first
    c.s1[...] = s_first
    c.m[...] = jnp.full(c.m.shape, NEG, F32)
    c.acc[...] = jnp.zeros(c.acc.shape, F32)
    yield


def _nsa_tail(c, t_last, diag_bias):
    _nsa_absorb(c, t_last, _add_rows(c.s0[...], diag_bias))
    yield
    acc = c.acc[...]
    o_sel = acc[:, :HD] * (1.0 / acc[:, HD:HD + 1])
    g = _sigmoid(c.g[...] + c.gb[...])
    outs = []
    for r in range(NSA_REP):
        rs = slice(r * QB, (r + 1) * QB)
        outs.append(g[:, 3 * r:3 * r + 1] * c.o_cmp[rs] + g[:, 3 * r + 1:3 * r + 2] * o_sel[rs]
                    + g[:, 3 * r + 2:3 * r + 3] * c.o_win[rs])
    o4 = jnp.concatenate(outs, axis=1)
    c.o[...] = (o4 * _silu(c.z[...])).astype(c.o.dtype)


def _nsa_body(q_ref, z_ref, g_ref, gb_ref, rc_ref, rs1_ref, rs2_ref, ovt_ref,
              kc_ref, vc_ref, ks_ref, vs_ref, kw_ref, vw_ref, o_ref, qa_scr, s0_scr, s1_scr, m_scr, acc_scr):
    i = pl.program_id(2)
    rows = NSA_REP * QB
    W = NSA_REP * HD
    n_grp = kc_ref.shape[0]
    grps = []
    for g in range(n_grp):
        lanes = pl.ds(g * W, W)
        glanes = pl.ds(g * 128, 128)
        grps.append(types.SimpleNamespace(
            q=q_ref.at[:, lanes], z=z_ref.at[:, lanes], g=g_ref.at[:, glanes], gb=gb_ref.at[:, glanes],
            kc=kc_ref.at[g], vc=vc_ref.at[g], ks=ks_ref.at[g], vs=vs_ref.at[g], kw=kw_ref.at[g], vw=vw_ref.at[g],
            o=o_ref.at[:, lanes], qa=qa_scr.at[g], s0=s0_scr.at[g], s1=s1_scr.at[g], m=m_scr.at[g], acc=acc_scr.at[g]))
    lock = lambda gens: _run_skewed(gens, n_grp, 0)

    tq_col = i * QB + (lax.broadcasted_iota(jnp.int32, (rows, 1), 0) & (QB - 1))
    tabs = (rc_ref[...], rs1_ref[...], rs2_ref[...])

    t_last = (i * QB) // SEL_TILE
    odd = (t_last & 1) == 1
    tq = i * QB + lax.broadcasted_iota(jnp.int32, (QB, 1), 0)
    ncmp = kc_ref.shape[1]
    cend = lax.broadcasted_iota(jnp.int32, (QB, ncmp), 1) * CMP_STRIDE + (CMP_BLOCK - 1)
    cmp_bias = jnp.where(cend <= tq, 0.0, NEG)
    w0 = jnp.maximum((i + 1) * QB - WIN_BAND, 0)
    diff = (tq - w0) - lax.broadcasted_iota(jnp.int32, (QB, WIN_BAND), 1)
    win_bias = jnp.where(lax.bitcast_convert_type(diff, jnp.uint32) < jnp.uint32(WINDOW), 0.0, NEG)
    kpos = t_last * SEL_TILE + lax.broadcasted_iota(jnp.int32, (QB, SEL_TILE), 1)
    diag_bias = jnp.where(kpos <= tq, 0.0, NEG)

    lock([_nsa_head(c, i, tq_col, tabs, ovt_ref[...], cmp_bias, win_bias) for c in grps])

    @pl.when(odd)
    def _():
        lock([_nsa_first_of_odd(c) for c in grps])

    def pair(t2, carry):
        lock([_nsa_pair(c, 2 * t2 + (t_last & 1)) for c in grps])
        return carry

    lax.fori_loop(0, t_last // 2, pair, 0)
    lock([_nsa_tail(c, t_last, diag_bias) for c in grps])


def nsa_attention(proj, gate_b, rope_q, overlap, shared, batch):
    N = proj.shape[0]
    T = N // batch
    nq = T // QB
    W = NSA_REP * HD
    k_cmp, v_cmp, k_sel, v_sel, k_win, v_win = shared
    rc, rs1, rs2 = rope_q
    ncmp = k_cmp.shape[2]
    n_blk = T // SEL_BLOCK
    P = NSA_GROUPS_PER_STEP
    steps = NSA_GROUPS // P
    rows = NSA_REP * QB
    row_blk = lambda b, g, i: (b * nq + i, g)
    kv = lambda w: pl.BlockSpec((None, P, T, w), lambda b, g, i: (b, g, 0, 0))
    cm = pl.BlockSpec((None, P, ncmp, HD), lambda b, g, i: (b, g, 0, 0))
    rope = pl.BlockSpec((QB, HD), lambda b, g, i: (i, 0))
    gate_blk0 = 2 * NSA_HEADS * HD // (P * 128)
    return pl.pallas_call(
        _nsa_body,
        out_shape=jax.ShapeDtypeStruct((N, NSA_HEADS * HD), BF16),
        grid=(batch, steps, nq),
        in_specs=[pl.BlockSpec((QB, P * W), row_blk),
                  pl.BlockSpec((QB, P * W), lambda b, g, i: (b * nq + i, steps + g)),
                  pl.BlockSpec((QB, P * 128), lambda b, g, i: (b * nq + i, gate_blk0 + g)),
                  pl.BlockSpec((1, P * 128), lambda b, g, i: (0, g)),
                  rope, rope, rope,
                  pl.BlockSpec((n_blk, ncmp), lambda b, g, i: (0, 0)),
                  cm, cm, kv(k_sel.shape[3]), kv(v_sel.shape[3]), kv(k_win.shape[3]), kv(v_win.shape[3])],
        out_specs=pl.BlockSpec((QB, P * W), row_blk),
        scratch_shapes=[pltpu.VMEM((P, rows, k_sel.shape[3]), BF16),
                        pltpu.VMEM((P, rows, SEL_TILE), F32),
                        pltpu.VMEM((P, rows, SEL_TILE), F32),
                        pltpu.VMEM((P, rows, 1), F32),
                        pltpu.VMEM((P, rows, v_sel.shape[3]), F32)],
        compiler_params=pltpu.CompilerParams(
            dimension_semantics=("parallel", "parallel", "arbitrary"), vmem_limit_bytes=VMEM_LIMIT),
        name="nsa_attention",
    )(proj, proj, proj, gate_b, rc, rs1, rs2, overlap, k_cmp, v_cmp, k_sel, v_sel, k_win, v_win)


def _rms(x, g):
    return x * lax.rsqrt(jnp.mean(x * x, axis=-1, keepdims=True) + NORM_EPS) * g


def _norm_body(x_ref, g_ref, o_ref):
    o_ref[...] = _rms(x_ref[...], g_ref[...]).astype(o_ref.dtype)


def rmsnorm(h, g, out_dtype, tm=512):
    N, D = h.shape
    return pl.pallas_call(
        _norm_body,
        out_shape=jax.ShapeDtypeStruct((N, D), out_dtype),
        grid=(N // tm,),
        in_specs=[pl.BlockSpec((tm, D), lambda i: (i, 0)), pl.BlockSpec((1, D), lambda i: (0, 0))],
        out_specs=pl.BlockSpec((tm, D), lambda i: (i, 0)),
        compiler_params=pltpu.CompilerParams(dimension_semantics=("parallel",)),
        name="rmsnorm",
    )(h, g.reshape(1, D))


def _mix_body(x_ref, xp_ref, g_ref, mu_ref, a_ref, o_ref, mid_ref, *, seq, lora_src):
    tm = x_ref.shape[0]
    g = g_ref[...]
    u = _rms(x_ref[...], g)
    last_prev = _rms(xp_ref[...], g)[xp_ref.shape[0] - 1:, :]
    first = lax.rem(pl.program_id(0) * tm, seq) == 0
    row = lax.broadcasted_iota(jnp.int32, (tm, 1), 0)
    prev = jnp.where(row == 0, jnp.where(first, 0.0, last_prev), pltpu.roll(u, 1, 0))
    xx = prev - u
    mix = lambda n: (u + xx * mu_ref[n:n + 1, :]).astype(BF16)
    for n in range(o_ref.shape[0]):
        o_ref[n] = mix(n)
    for l, n in enumerate(lora_src):
        mid = _dot(mix(n), a_ref[l])
        mid_ref[l] = (jnp.tanh(mid) if l == 0 else mid).astype(mid_ref.dtype)


def rwkv_mix(h, g, mu, a_w, lora_src, n_out, seq, tm=256):
    N, D = h.shape
    L, _, R = a_w.shape
    assert seq % tm == 0
    sub = 8
    return pl.pallas_call(
        functools.partial(_mix_body, seq=seq, lora_src=tuple(lora_src)),
        out_shape=(jax.ShapeDtypeStruct((n_out, N, D), BF16), jax.ShapeDtypeStruct((L, N, R), BF16)),
        grid=(N // tm,),
        in_specs=[pl.BlockSpec((tm, D), lambda i: (i, 0)),
                  pl.BlockSpec((sub, D), lambda i: (jnp.maximum(i * (tm // sub) - 1, 0), 0)),
                  pl.BlockSpec((1, D), lambda i: (0, 0)),
                  pl.BlockSpec(mu.shape, lambda i: (0, 0)),
                  pl.BlockSpec((L, D, R), lambda i: (0, 0, 0))],
        out_specs=(pl.BlockSpec((n_out, tm, D), lambda i: (0, i, 0)), pl.BlockSpec((L, tm, R), lambda i: (0, i, 0))),
        compiler_params=pltpu.CompilerParams(dimension_semantics=("parallel",), vmem_limit_bytes=VMEM_LIMIT),
        name="rwkv_mix",
    )(h, h, g.reshape(1, D), mu, a_w)


def _kvpost_body(kv_ref, rc_ref, rs1_ref, rs2_ref, pek_ref, pev_ref, cx_ref, ks_ref, vs_ref, kw_ref, vw_ref):
    tt = kv_ref.shape[0]
    aug = ks_ref.shape[-1] - HD
    tok = pl.program_id(1) * tt + lax.broadcasted_iota(jnp.int32, (tt, aug), 0)
    lane = lax.broadcasted_iota(jnp.int32, (tt, aug), 1)
    onehot = jnp.where(lane == (tok >> (SEL_BLOCK.bit_length() - 1)), 1.0, 0.0).astype(BF16)
    ones_col = jnp.where(lax.broadcasted_iota(jnp.int32, (tt, vs_ref.shape[-1] - HD), 1) == 0, 1.0, 0.0).astype(BF16)
    tabs = (rc_ref[...], rs1_ref[...], rs2_ref[...])
    G = NSA_GROUPS
    n16 = tt // CMP_STRIDE
    for g in range(G):
        for j, pe_ref in ((0, pek_ref), (1, pev_ref)):
            x3 = kv_ref[:, (j * G + g) * HD:(j * G + g + 1) * HD].reshape(n16, CMP_STRIDE, HD)
            x16 = jnp.concatenate([x3[:, l, :] for l in range(CMP_STRIDE)], axis=1)
            cx_ref[j, g] = (x16 + pe_ref[0:1, :]).astype(BF16)
            cx_ref[2 + j, g] = (x16 + pe_ref[1:2, :]).astype(BF16)
        k_s, v_s, k_w, v_w = [kv_ref[:, (j * G + g) * HD:(j * G + g + 1) * HD] for j in range(2, 6)]
        ks_ref[g] = jnp.concatenate([_rope(k_s, *tabs).astype(BF16), onehot], axis=1)
        vs_ref[g] = jnp.concatenate([v_s.astype(BF16), ones_col], axis=1)
        kw_ref[g] = _rope(k_w, *tabs).astype(BF16)
        vw_ref[g] = jnp.concatenate([v_w.astype(BF16), ones_col], axis=1)


def kv_post(kv, rope_t, cmp_pe, batch, tt=512):
    N = kv.shape[0]
    T = N // batch
    G = NSA_GROUPS
    nt = T // tt
    n_blk = T // SEL_BLOCK
    aug = -(-n_blk // 128) * 128
    half = CMP_STRIDE * HD
    pe = cmp_pe.reshape(2, 2, half)
    tab = pl.BlockSpec((tt, HD), lambda b, i: (i, 0))
    pes = pl.BlockSpec((2, half), lambda b, i: (0, 0))
    out4 = lambda w: pl.BlockSpec((None, G, tt, w), lambda b, i: (b, 0, i, 0))
    sds = lambda w: jax.ShapeDtypeStruct((batch, G, T, w), BF16)
    n16 = tt // CMP_STRIDE
    return pl.pallas_call(
        _kvpost_body,
        out_shape=(jax.ShapeDtypeStruct((4, batch, G, T // CMP_STRIDE, half), BF16),
                   sds(HD + aug), sds(2 * HD), sds(HD), sds(2 * HD)),
        grid=(batch, nt),
        in_specs=[pl.BlockSpec((tt, kv.shape[1]), lambda b, i: (b * nt + i, 0)), tab, tab, tab, pes, pes],
        out_specs=(pl.BlockSpec((4, None, G, n16, half), lambda b, i: (0, b, 0, i, 0)),
                   out4(HD + aug), out4(2 * HD), out4(HD), out4(2 * HD)),
        compiler_params=pltpu.CompilerParams(dimension_semantics=("parallel", "parallel"), vmem_limit_bytes=VMEM_LIMIT),
        name="kv_post",
    )(kv, *rope_t, pe[0], pe[1])


def _cmp2_body(lo_ref, hi_ref, w2_ref, rc_ref, rs1_ref, rs2_ref, o_ref):
    n = lo_ref.shape[0]
    hid = lo_ref[...] + pltpu.roll(hi_ref[...], n - 1, 0)
    out = _dot(_silu(hid).astype(BF16), w2_ref[...])
    roped = _rope(out, rc_ref[...], rs1_ref[...], rs2_ref[...])
    o_ref[...] = jnp.where(pl.program_id(0) == 0, roped, out).astype(o_ref.dtype)


def cmp_mlp2(hd, w2, rope_c, groups, ng):
    H = hd.shape[2]
    tab = pl.BlockSpec((ng, HD), lambda j, m: (0, 0))
    return pl.pallas_call(
        _cmp2_body,
        out_shape=jax.ShapeDtypeStruct((2, groups, ng, HD), BF16),
        grid=(2, groups),
        in_specs=[pl.BlockSpec((None, ng, H), lambda j, m: (j, m, 0)),
                  pl.BlockSpec((None, ng, H), lambda j, m: (2 + j, m, 0)),
                  pl.BlockSpec((None, H, HD), lambda j, m: (j, 0, 0)), tab, tab, tab],
        out_specs=pl.BlockSpec((None, None, ng, HD), lambda j, m: (j, m, 0, 0)),
        compiler_params=pltpu.CompilerParams(dimension_semantics=("parallel", "parallel")),
        name="cmp_mlp2",
    )(hd, hd, w2, *rope_c)


def _rope_tables(pos):
    half = ROPE_DIM // 2
    inv = ROPE_THETA ** (-jnp.arange(half, dtype=F32) / half)
    ang = pos.astype(F32)[:, None] * inv[None, :]
    cos, sin = jnp.cos(ang), jnp.sin(ang)
    n = pos.shape[0]
    c = jnp.concatenate([cos, cos, jnp.ones((n, HD - ROPE_DIM), F32)], axis=1)
    s1 = jnp.concatenate([-sin, jnp.zeros((n, HD - half), F32)], axis=1)
    s2 = jnp.concatenate([jnp.zeros((n, half), F32), sin, jnp.zeros((n, HD - ROPE_DIM), F32)], axis=1)
    return c, s1, s2


def _rwkv_layer(h, batch, v_first, norm_g, mu, w_rkvz, w0, w1, w2, a0, a1, a2, vres, k_k, k_a, r_k, ln_g, ln_b, w_o):
    N, D = h.shape
    loras = [(w1, w2, w0), (a1, a2, a0)]
    idx = [4, 5]
    if vres is not None:
        loras.append((vres[1], vres[2], vres[0]))
        idx.append(2)
    rpad = 128
    a_w = jnp.stack([jnp.pad(l[0], ((0, 0), (0, rpad - l[0].shape[1]))) for l in loras]).astype(BF16)
    b_w = jnp.stack([jnp.pad(l[1], ((0, rpad - l[1].shape[0]), (0, 0))) for l in loras]).astype(BF16)
    bias = jnp.stack([l[2].reshape(1, D) for l in loras])

    mixed, mid = rwkv_mix(h, norm_g, mu, a_w, idx, 4, N // batch)
    rkvz = matmul(mixed, w_rkvz.astype(BF16))
    lor = lora_out(mid, b_w, bias)

    o = wkv(rkvz, lor, v_first, k_k, k_a, r_k.reshape(-1), ln_g, ln_b, batch)
    return matmul(o, w_o.astype(BF16), res=h), rkvz


def _shared_kv(h, batch, rope_t, kv_norm_g, kv_w, cmp_pe, cmp_w1, cmp_w2):
    N, D = h.shape
    T = N // batch
    G = NSA_GROUPS
    kv = matmul(rmsnorm(h, kv_norm_g, BF16), kv_w.astype(BF16))
    cx, k_sel, v_sel, k_win, v_win = kv_post(kv, rope_t, cmp_pe, batch)

    ng = T // CMP_STRIDE
    half = CMP_STRIDE * HD
    w1 = jnp.concatenate([cmp_w1[:, :half], cmp_w1[:, half:]], axis=0).astype(BF16)
    hd = matmul(cx.reshape(4, batch * G * ng, half), w1)
    rope_c = _rope_tables(jnp.arange(ng) * CMP_STRIDE + CMP_BLOCK - 1)
    cmp = cmp_mlp2(hd, cmp_w2.astype(BF16), rope_c, batch * G, ng).reshape(2, batch, G, ng, HD)
    return cmp[0], cmp[1], k_sel, v_sel, k_win, v_win


def _nsa_layer(h, batch, shared, rope_t, overlap, norm_g, w_in, gate_b, w_o):
    N, D = h.shape
    G = NSA_GROUPS
    width = NSA_HEADS * HD
    ng = N_BRANCH * NSA_HEADS
    per = N_BRANCH * NSA_REP
    w_g = jnp.pad(w_in[:, width:width + ng].reshape(D, G, per), ((0, 0), (0, 0), (0, 128 - per))).reshape(D, G * 128)
    w_all = jnp.concatenate([w_in[:, :width], w_in[:, width + ng:], w_g], axis=1).astype(BF16)
    b_g = jnp.pad(gate_b.reshape(G, per), ((0, 0), (0, 128 - per))).reshape(1, G * 128)
    proj = matmul(rmsnorm(h, norm_g, BF16), w_all, tn=768)
    o = nsa_attention(proj, b_g, rope_t, overlap, shared, batch)
    return matmul(o, w_o.astype(BF16), res=h)


def kernel(x, a_norm_g, a_mu, a_w_rkvz, a_w0, a_w1, a_w2, a_a0, a_a1, a_a2, a_v0, a_v1, a_v2, a_k_k, a_k_a, a_r_k,
           a_ln_g, a_ln_b, a_w_o, kv_norm_g, kv_w, cmp_pe, cmp_w1, cmp_w2, b_norm_g, b_w_in, b_gate_b, b_w_o, final_g):
    B, T, D = x.shape
    N = B * T
    n_a = a_norm_g.shape[0]
    n_b = b_norm_g.shape[0]
    h = x.reshape(N, D)
    v_first = None
    for i in range(n_a):
        vres = None if i == 0 else (a_v0[i - 1], a_v1[i - 1], a_v2[i - 1])
        h, rkvz = _rwkv_layer(h, B, v_first, a_norm_g[i], a_mu[i], a_w_rkvz[i], a_w0[i], a_w1[i], a_w2[i],
                              a_a0[i], a_a1[i], a_a2[i], vres, a_k_k[i], a_k_a[i], a_r_k[i],
                              a_ln_g[i], a_ln_b[i], a_w_o[i])
        if i == 0:
            v_first = rkvz
    if n_b:
        rope_t = _rope_tables(jnp.arange(T))
        shared = _shared_kv(h, B, rope_t, kv_norm_g, kv_w, cmp_pe, cmp_w1, cmp_w2)
        ncmp = T // CMP_STRIDE
        n_blk = T // SEL_BLOCK
        cpos = jnp.arange(ncmp)[:, None] * CMP_STRIDE + jnp.arange(CMP_BLOCK)[None, :]
        overlap = jax.nn.one_hot(cpos // SEL_BLOCK, n_blk, dtype=F32).mean(axis=1).T.astype(BF16)
        for j in range(n_b):
            h = _nsa_layer(h, B, shared, rope_t, overlap, b_norm_g[j], b_w_in[j], b_gate_b[j], b_w_o[j])
    return rmsnorm(h, final_g, F32).reshape(B, T, D)
```

```python
import functools
import math
import types

import jax
import jax.numpy as jnp
from jax import lax
from jax.experimental import pallas as pl
from jax.experimental.pallas import tpu as pltpu

F32 = jnp.float32
BF16 = jnp.bfloat16

NORM_EPS = 1e-6
GN_EPS = 64e-5
HEAD = 64
CHUNK = 64
UNIT = 256
UNIT_HEADS = UNIT // HEAD
assert CHUNK == HEAD
WKV_CHUNKS = 4
WKV_GROUP = 8
WKV_LAG = 10
EXP_M05 = math.exp(-0.5)

NSA_HEADS = 16
NSA_GROUPS = 4
NSA_REP = NSA_HEADS // NSA_GROUPS
HD = 128
N_BRANCH = 3
CMP_BLOCK = 32
CMP_STRIDE = 16
SEL_BLOCK = 64
SEL_TOP_N = 16
WINDOW = 512
QB = 128
NSA_GROUPS_PER_STEP = 2
LOG2E = math.log2(math.e)
ROPE_DIM = HD // 4
ROPE_THETA = 500000.0
NEG = -1e30
FORCE_BONUS = 1e4
SEL_TILE = 512
WIN_BAND = WINDOW + 128

VMEM_LIMIT = 56 * 1024 * 1024


def _dot(a, b):
    return jnp.dot(a, b, preferred_element_type=F32)


def _dot_nt(a, b):
    return lax.dot_general(a, b, (((1,), (1,)), ((), ())), preferred_element_type=F32)


def _dot_tn(a, b):
    return lax.dot_general(a, b, (((0,), (0,)), ((), ())), preferred_element_type=F32)


def _split3(x):
    h = x.astype(BF16)
    r1 = x - h.astype(F32)
    m = r1.astype(BF16)
    l = (r1 - m.astype(F32)).astype(BF16)
    return h, m, l


def _sigmoid(x):
    return 1.0 / (1.0 + jnp.exp(-x))


def _silu(x):
    return x * _sigmoid(x)


def _mm_body(*refs, has_res):
    if has_res:
        a_ref, w_ref, r_ref, o_ref = refs
    else:
        a_ref, w_ref, o_ref = refs
    acc = _dot(a_ref[...].astype(BF16), w_ref[...].astype(BF16))
    if has_res:
        acc = acc + r_ref[...]
    o_ref[...] = acc.astype(o_ref.dtype)


def matmul(a, w, res=None, out_dtype=F32, tm=None, tn=1024):
    squeeze = a.ndim == 2
    if squeeze:
        a, w = a[None], w[None]
        res = None if res is None else res[None]
    G, K, N = w.shape
    M = a.shape[1]
    if tm is None:
        tm = 2048 if res is None and M % 2048 == 0 else 1024
    tm = min(tm, M)
    tn = min(tn, N)
    assert M % tm == 0 and N % tn == 0, (M, N, tm, tn)
    in_specs = [pl.BlockSpec((None, tm, K), lambda g, i, j: (g, i, 0)),
                pl.BlockSpec((None, K, tn), lambda g, i, j: (g, 0, j))]
    args = [a, w]
    if res is not None:
        in_specs.append(pl.BlockSpec((None, tm, tn), lambda g, i, j: (g, i, j)))
        args.append(res)
    out = pl.pallas_call(
        functools.partial(_mm_body, has_res=res is not None),
        out_shape=jax.ShapeDtypeStruct((G, M, N), out_dtype),
        grid=(G, M // tm, N // tn),
        in_specs=in_specs,
        out_specs=pl.BlockSpec((None, tm, tn), lambda g, i, j: (g, i, j)),
        compiler_params=pltpu.CompilerParams(
            dimension_semantics=("parallel", "parallel", "arbitrary"), vmem_limit_bytes=VMEM_LIMIT),
        name="matmul",
    )(*args)
    return out[0] if squeeze else out


def _expand(x, lane_head):
    return jnp.concatenate([jnp.where(lane_head == h, x, 0.0) for h in range(UNIT_HEADS)], axis=0)


def _segsums(xs, ones_bd):
    pieces = []
    for x in xs:
        h = x.astype(BF16)
        pieces += [h, (x - h.astype(F32)).astype(BF16)]
    out = _dot(jnp.concatenate(pieces, axis=0), ones_bd)
    n = xs[0].shape[0]
    return [out[2 * i * n:(2 * i + 1) * n] + out[(2 * i + 1) * n:(2 * i + 2) * n] for i in range(len(xs))]


def _wkv_unit_stages(r, k, v, z, lora, vf, k_k, k_a, r_k, ln_g, ln_b, s_ref, o_ref, consts, turn):
    state_written, chunk_idx = turn
    tri, ones_bd, same, strict, incl, lane_head, eye = consts
    bf = lambda x: x.astype(BF16)
    ex = lambda x: _expand(x, lane_head)

    lw = -EXP_M05 * _sigmoid(lora(0))
    a = _sigmoid(lora(1))
    v = v[...]
    if vf is not None:
        v = v + (vf[...] - v) * _sigmoid(lora(2))
    yield
    k = k[...]
    r = r[...]
    kk = k * k_k[...]
    k2 = k * (1.0 + (a - 1.0) * k_a[...])
    n2, bonus = _segsums([kk * kk, r * k2 * r_k[...]], ones_bd)
    yield
    lh, lm, ll = _split3(lw)
    cum = _dot(tri, lh) + _dot(tri, lm) + _dot(tri, ll)
    yield

    kk = kk / jnp.maximum(jnp.sqrt(n2), 1e-12)
    p_incl = jnp.exp(cum)
    p_inv = jnp.exp(-cum)
    kt = k2 * p_inv
    bt = kk * a * p_inv
    at = -kk * jnp.exp(cum - lw)
    rt = r * p_incl
    eb = bf(ex(bt))
    ek = bf(ex(kt))
    ev = bf(ex(v))
    yield

    top, bot = slice(0, CHUNK), slice(CHUNK, 2 * CHUNK)
    ar = bf(jnp.concatenate([at, rt], axis=0))
    sb = _dot_nt(ar, eb)
    yield
    sk = _dot_nt(ar, ek)
    yield
    a_ab = jnp.where(strict, sb[top], 0.0)
    a_rb = bf(jnp.where(incl, sb[bot], 0.0))
    kv2 = _dot(bf(jnp.concatenate([jnp.where(strict, sk[top], 0.0), jnp.where(incl, sk[bot], 0.0)], axis=0)), ev)
    akv, rkv = kv2[top], kv2[bot]
    yield

    tm = eye + a_ab
    pw = _dot(bf(a_ab), bf(ex(a_ab)))
    yield
    for _ in range(4):
        both = _dot(bf(jnp.concatenate([pw, tm], axis=0)), bf(ex(pw)))
        pw, tm = both[top], tm + both[bot]
        yield
    tm = tm + _dot(bf(tm), bf(ex(pw)))
    yield

    assert len(state_written) == chunk_idx, "stage skew too small: state read before the previous chunk's update"
    s = s_ref[...]
    xs = _dot_nt(ar, bf(s))
    yield
    u = _dot(bf(tm), bf(ex(akv + xs[top])))
    yield
    o = xs[bot] + _dot(a_rb, bf(ex(u))) + rkv
    yield
    upd = _dot_tn(bf(jnp.concatenate([v, u], axis=0)), bf(jnp.concatenate([kt, bt], axis=0)))
    yield
    s_ref[...] = (s + jnp.where(same, upd, 0.0)) * p_incl[CHUNK - 1:CHUNK, :]
    state_written.append(chunk_idx)
    mean = _segsums([o], ones_bd)[0] * (1.0 / HEAD)
    yield

    oc = o - mean
    var = _segsums([oc * oc], ones_bd)[0] * (1.0 / HEAD)
    yield

    o_ref[...] = ((oc * lax.rsqrt(var + GN_EPS) * ln_g[...] + ln_b[...] + bonus * v)
                  * _silu(z[...])).astype(o_ref.dtype)


def _run_skewed(gens, group, lag):
    done = [False] * len(gens)
    rnd = 0
    while not all(done):
        for u, g in enumerate(gens):
            if (u // group) * lag <= rnd and not done[u]:
                try:
                    next(g)
                except StopIteration:
                    done[u] = True
        rnd += 1


def _wkv_body(*refs, has_vres, units):
    if has_vres:
        (r_ref, k_ref, v_ref, z_ref, mid_ref, bw_ref, bias_ref, vf_ref,
         kk_ref, ka_ref, rk_ref, lg_ref, lb_ref, o_ref, s_ref) = refs
    else:
        (r_ref, k_ref, v_ref, z_ref, mid_ref, bw_ref, bias_ref,
         kk_ref, ka_ref, rk_ref, lg_ref, lb_ref, o_ref, s_ref) = refs
        vf_ref = None

    @pl.when(pl.program_id(2) == 0)
    def _():
        s_ref[...] = jnp.zeros_like(s_ref)

    row = lax.broadcasted_iota(jnp.int32, (UNIT, UNIT), 0)
    col = lax.broadcasted_iota(jnp.int32, (UNIT, UNIT), 1)
    same = (row >> 6) == (col >> 6)
    ones_bd = jnp.where(same, 1.0, 0.0).astype(BF16)
    tr = lax.broadcasted_iota(jnp.int32, (CHUNK, CHUNK), 0)
    tc = lax.broadcasted_iota(jnp.int32, (CHUNK, CHUNK), 1)
    tri = jnp.where(tc <= tr, 1.0, 0.0).astype(BF16)
    lane = lax.broadcasted_iota(jnp.int32, (CHUNK, UNIT), 1)
    lane_head = lane >> 6
    t_row = lax.broadcasted_iota(jnp.int32, (CHUNK, UNIT), 0)
    s_lane = lane & (CHUNK - 1)
    strict = s_lane < t_row
    incl = s_lane <= t_row
    eye = jnp.where(s_lane == t_row, 1.0, 0.0).astype(F32)
    consts = (tri, ones_bd, same, strict, incl, lane_head, eye)

    def view(ref, cc, uu):
        if ref is None:
            return None
        rows = pl.ds(cc * CHUNK, CHUNK) if ref.shape[0] > 1 else slice(None)
        return ref.at[rows, pl.ds(uu * UNIT, UNIT)]

    def lora(cc, uu):
        lanes = pl.ds(uu * UNIT, UNIT)
        return lambda l: (_dot(mid_ref[l, pl.ds(cc * CHUNK, CHUNK), :], bw_ref[l, :, lanes]) + bias_ref[l, :, lanes])

    written = [[] for _ in range(units)]
    _run_skewed([
        _wkv_unit_stages(*(view(ref, cc, uu) for ref in (r_ref, k_ref, v_ref, z_ref)), lora(cc, uu),
                         *(view(ref, cc, uu) for ref in (vf_ref, kk_ref, ka_ref, rk_ref, lg_ref, lb_ref)),
                         s_ref.at[uu], view(o_ref, cc, uu), consts, (written[uu], cc))
        for cc in range(WKV_CHUNKS) for uu in range(units)], WKV_GROUP, WKV_LAG)


def wkv(rkvz, mid, b_w, bias, v_first, k_k, k_a, r_k, ln_g, ln_b, batch, units=8):
    _, N, D = rkvz.shape
    L, _, R = mid.shape
    T = N // batch
    step_rows = WKV_CHUNKS * CHUNK
    nc = T // step_rows
    W = units * UNIT
    has_vres = v_first is not None
    assert L == (3 if has_vres else 2)
    row_map = lambda b, u, c: (b * nc + c, u)

    def lead(n):
        return pl.BlockSpec((None, step_rows, W), lambda b, u, c: (n, b * nc + c, u))

    in_specs = [lead(0), lead(1), lead(2), lead(3),
                pl.BlockSpec((L, step_rows, R), lambda b, u, c: (0, b * nc + c, 0)),
                pl.BlockSpec((L, R, W), lambda b, u, c: (0, 0, u)),
                pl.BlockSpec((L, 1, W), lambda b, u, c: (0, 0, u))]
    args = [rkvz, rkvz, rkvz, rkvz, mid, b_w, bias]
    if has_vres:
        in_specs += [lead(2)]
        args += [v_first]
    par = pl.BlockSpec((1, W), lambda b, u, c: (0, u))
    in_specs += [par] * 5
    args += [k_k.reshape(1, D), k_a.reshape(1, D), r_k.reshape(1, D), ln_g.reshape(1, D), ln_b.reshape(1, D)]
    return pl.pallas_call(
        functools.partial(_wkv_body, has_vres=has_vres, units=units),
        out_shape=jax.ShapeDtypeStruct((N, D), BF16),
        grid=(batch, D // W, nc),
        in_specs=in_specs,
        out_specs=pl.BlockSpec((step_rows, W), row_map),
        scratch_shapes=[pltpu.VMEM((units, UNIT, UNIT), F32)],
        compiler_params=pltpu.CompilerParams(
            dimension_semantics=("parallel", "parallel", "arbitrary"), vmem_limit_bytes=VMEM_LIMIT),
        name="wkv7",
    )(*args)


def _attend(s, v_aug):
    m = jnp.max(s, axis=1, keepdims=True)
    acc = _dot(jnp.exp2(s - m).astype(BF16), v_aug)
    return acc[:, :HD] * (1.0 / acc[:, HD:HD + 1])


def _rope(x, c, s1, s2):
    n = x.shape[1] // HD
    if n > 1:
        c, s1, s2 = (jnp.concatenate([t] * n, axis=1) for t in (c, s1, s2))
    half = ROPE_DIM // 2
    return x * c + pltpu.roll(x, x.shape[1] - half, 1) * s1 + pltpu.roll(x, half, 1) * s2


def _tile_rows(t):
    return pl.ds(pl.multiple_of(t * SEL_TILE, SEL_TILE), SEL_TILE)


def _nsa_scores(c, t, dst):
    dst[...] = _dot_nt(c.qa[...], c.ks[_tile_rows(t), :])


def _nsa_absorb(c, t, s):
    m = c.m[...]
    m_new = jnp.maximum(m, jnp.max(s, axis=1, keepdims=True))
    pv = _dot(jnp.exp2(s - m_new).astype(BF16), c.vs[_tile_rows(t), :])
    c.acc[...] = jnp.exp2(m - m_new) * c.acc[...] + pv
    c.m[...] = m_new


def _nsa_pair(c, t):
    _nsa_scores(c, t + 1, c.s1)
    yield
    _nsa_absorb(c, t, c.s0[...])
    yield
    _nsa_scores(c, t + 2, c.s0)
    yield
    _nsa_absorb(c, t + 1, c.s1[...])
    yield


def _nsa_first_of_odd(c):
    _nsa_scores(c, 1, c.s0)
    yield
    _nsa_absorb(c, 0, c.s1[...])
    yield


def _add_rows(s, bias):
    return jnp.concatenate([s[r * QB:(r + 1) * QB] + bias for r in range(NSA_REP)], axis=0)


def _nsa_head(c, i, tq_col, tabs, ov_t, cmp_bias, win_bias):
    rows = NSA_REP * QB
    q4 = _rope(c.q[...], *tabs) * (HD ** -0.5 * LOG2E)
    qs = jnp.concatenate([q4[:, r * HD:(r + 1) * HD] for r in range(NSA_REP)], axis=0)
    qb = qs.astype(BF16)

    band = WIN_BAND
    w0 = pl.multiple_of(jnp.maximum((i + 1) * QB - band, 0), QB)
    s_cmp = _dot_nt(qb, c.kc[...])
    yield
    s_win = _dot_nt(qb, c.kw[pl.ds(w0, band), :])
    yield

    s_cmp = _add_rows(s_cmp, cmp_bias)
    e = jnp.exp2(s_cmp - jnp.max(s_cmp, axis=1, keepdims=True))
    p = e * (jnp.where(tq_col >= CMP_BLOCK - 1, 1.0, 0.0) / jnp.sum(e, axis=1, keepdims=True))
    c.o_cmp = _dot(p.astype(BF16), c.vc[...])
    yield

    psum = p[0:QB]
    for r in range(1, NSA_REP):
        psum = psum + p[r * QB:(r + 1) * QB]
    ph, pm, pl_ = _split3(psum)
    n_blk = ov_t.shape[0]
    imp_t = _dot_nt(ov_t, ph) + _dot_nt(ov_t, pm) + _dot_nt(ov_t, pl_)
    yield

    c.o_win = _attend(_add_rows(s_win, win_bias), c.vw[pl.ds(w0, band), :])
    yield

    jsub = lax.broadcasted_iota(jnp.int32, (n_blk, QB), 0)
    cur = (i * QB + lax.broadcasted_iota(jnp.int32, (n_blk, QB), 1)) >> (SEL_BLOCK.bit_length() - 1)
    forced = (jsub == 0) | (jsub == cur) | (jsub == cur - 1)
    imp_t = jnp.where(forced, FORCE_BONUS, jnp.where(jsub > cur, NEG, imp_t))
    sub8 = lax.broadcasted_iota(jnp.int32, (8, QB), 0)
    grp = [imp_t[8 * a:8 * a + 8] for a in range(n_blk // 8)]
    cnt = [jnp.zeros((8, QB), F32) for _ in grp]
    for jp in range(n_blk):
        rowv = jnp.broadcast_to(imp_t[jp:jp + 1, :], (8, QB))
        for a, x in enumerate(grp):
            ge = lambda: jnp.where(rowv >= x, 1.0, 0.0)
            gt = lambda: jnp.where(rowv > x, 1.0, 0.0)
            if 8 * a > jp:
                inc = ge()
            elif 8 * a + 7 <= jp:
                inc = gt()
            else:
                inc = jnp.where(sub8 + 8 * a > jp, ge(), gt())
            cnt[a] = cnt[a] + inc
    cnt = jnp.concatenate(cnt, axis=0)
    bias_t = jnp.where(cnt < float(min(SEL_TOP_N, n_blk)), 0.0, NEG).astype(BF16)
    qi = lax.broadcasted_iota(jnp.int32, (QB, QB), 0)
    qj = lax.broadcasted_iota(jnp.int32, (QB, QB), 1)
    eye_q = jnp.where(qi == qj, 1.0, 0.0).astype(BF16)
    bias = _dot_nt(eye_q, bias_t)
    yield
    bias4 = jnp.concatenate([bias] * NSA_REP, axis=0).astype(BF16)
    pad = c.ks.shape[1] - HD - n_blk
    parts = [qb, bias4] + ([jnp.zeros((rows, pad), BF16)] if pad else [])
    qa = jnp.concatenate(parts, axis=1)

    c.qa[...] = qa
    s_first = _dot_nt(qa, c.ks[_tile_rows(0), :])
    c.s0[...] = s_first
    c.s1[...] = s_first
    c.m[...] = jnp.full(c.m.shape, NEG, F32)
    c.acc[...] = jnp.zeros(c.acc.shape, F32)
    yield


def _nsa_tail(c, t_last, diag_bias):
    _nsa_absorb(c, t_last, _add_rows(c.s0[...], diag_bias))
    yield
    acc = c.acc[...]
    o_sel = acc[:, :HD] * (1.0 / acc[:, HD:HD + 1])
    g = _sigmoid(c.g[...] + c.gb[...])
    outs = []
    for r in range(NSA_REP):
        rs = slice(r * QB, (r + 1) * QB)
        outs.append(g[:, 3 * r:3 * r + 1] * c.o_cmp[rs] + g[:, 3 * r + 1:3 * r + 2] * o_sel[rs]
                    + g[:, 3 * r + 2:3 * r + 3] * c.o_win[rs])
    o4 = jnp.concatenate(outs, axis=1)
    c.o[...] = (o4 * _silu(c.z[...])).astype(c.o.dtype)


def _nsa_body(q_ref, z_ref, g_ref, gb_ref, rc_ref, rs1_ref, rs2_ref, ovt_ref,
              kc_ref, vc_ref, ks_ref, vs_ref, kw_ref, vw_ref, o_ref, qa_scr, s0_scr, s1_scr, m_scr, acc_scr):
    i = pl.program_id(2)
    rows = NSA_REP * QB
    W = NSA_REP * HD
    n_grp = kc_ref.shape[0]
    grps = []
    for g in range(n_grp):
        lanes = pl.ds(g * W, W)
        glanes = pl.ds(g * 128, 128)
        grps.append(types.SimpleNamespace(
            q=q_ref.at[:, lanes], z=z_ref.at[:, lanes], g=g_ref.at[:, glanes], gb=gb_ref.at[:, glanes],
            kc=kc_ref.at[g], vc=vc_ref.at[g], ks=ks_ref.at[g], vs=vs_ref.at[g], kw=kw_ref.at[g], vw=vw_ref.at[g],
            o=o_ref.at[:, lanes], qa=qa_scr.at[g], s0=s0_scr.at[g], s1=s1_scr.at[g], m=m_scr.at[g], acc=acc_scr.at[g]))
    lock = lambda gens: _run_skewed(gens, n_grp, 0)

    tq_col = i * QB + (lax.broadcasted_iota(jnp.int32, (rows, 1), 0) & (QB - 1))
    tabs = (rc_ref[...], rs1_ref[...], rs2_ref[...])

    t_last = (i * QB) // SEL_TILE
    odd = (t_last & 1) == 1
    tq = i * QB + lax.broadcasted_iota(jnp.int32, (QB, 1), 0)
    ncmp = kc_ref.shape[1]
    cend = lax.broadcasted_iota(jnp.int32, (QB, ncmp), 1) * CMP_STRIDE + (CMP_BLOCK - 1)
    cmp_bias = jnp.where(cend <= tq, 0.0, NEG)
    w0 = jnp.maximum((i + 1) * QB - WIN_BAND, 0)
    diff = (tq - w0) - lax.broadcasted_iota(jnp.int32, (QB, WIN_BAND), 1)
    win_bias = jnp.where(lax.bitcast_convert_type(diff, jnp.uint32) < jnp.uint32(WINDOW), 0.0, NEG)
    kpos = t_last * SEL_TILE + lax.broadcasted_iota(jnp.int32, (QB, SEL_TILE), 1)
    diag_bias = jnp.where(kpos <= tq, 0.0, NEG)

    lock([_nsa_head(c, i, tq_col, tabs, ovt_ref[...], cmp_bias, win_bias) for c in grps])

    @pl.when(odd)
    def _():
        lock([_nsa_first_of_odd(c) for c in grps])

    def pair(t2, carry):
        lock([_nsa_pair(c, 2 * t2 + (t_last & 1)) for c in grps])
        return carry

    lax.fori_loop(0, t_last // 2, pair, 0)
    lock([_nsa_tail(c, t_last, diag_bias) for c in grps])


def nsa_attention(proj, gate_b, rope_q, overlap, shared, batch):
    N = proj.shape[0]
    T = N // batch
    nq = T // QB
    W = NSA_REP * HD
    k_cmp, v_cmp, k_sel, v_sel, k_win, v_win = shared
    rc, rs1, rs2 = rope_q
    ncmp = k_cmp.shape[2]
    n_blk = T // SEL_BLOCK
    P = NSA_GROUPS_PER_STEP
    steps = NSA_GROUPS // P
    rows = NSA_REP * QB
    row_blk = lambda b, g, i: (b * nq + i, g)
    kv = lambda w: pl.BlockSpec((None, P, T, w), lambda b, g, i: (b, g, 0, 0))
    cm = pl.BlockSpec((None, P, ncmp, HD), lambda b, g, i: (b, g, 0, 0))
    rope = pl.BlockSpec((QB, HD), lambda b, g, i: (i, 0))
    gate_blk0 = 2 * NSA_HEADS * HD // (P * 128)
    return pl.pallas_call(
        _nsa_body,
        out_shape=jax.ShapeDtypeStruct((N, NSA_HEADS * HD), BF16),
        grid=(batch, steps, nq),
        in_specs=[pl.BlockSpec((QB, P * W), row_blk),
                  pl.BlockSpec((QB, P * W), lambda b, g, i: (b * nq + i, steps + g)),
                  pl.BlockSpec((QB, P * 128), lambda b, g, i: (b * nq + i, gate_blk0 + g)),
                  pl.BlockSpec((1, P * 128), lambda b, g, i: (0, g)),
                  rope, rope, rope,
                  pl.BlockSpec((n_blk, ncmp), lambda b, g, i: (0, 0)),
                  cm, cm, kv(k_sel.shape[3]), kv(v_sel.shape[3]), kv(k_win.shape[3]), kv(v_win.shape[3])],
        out_specs=pl.BlockSpec((QB, P * W), row_blk),
        scratch_shapes=[pltpu.VMEM((P, rows, k_sel.shape[3]), BF16),
                        pltpu.VMEM((P, rows, SEL_TILE), F32),
                        pltpu.VMEM((P, rows, SEL_TILE), F32),
                        pltpu.VMEM((P, rows, 1), F32),
                        pltpu.VMEM((P, rows, v_sel.shape[3]), F32)],
        compiler_params=pltpu.CompilerParams(
            dimension_semantics=("parallel", "parallel", "arbitrary"), vmem_limit_bytes=VMEM_LIMIT),
        name="nsa_attention",
    )(proj, proj, proj, gate_b, rc, rs1, rs2, overlap, k_cmp, v_cmp, k_sel, v_sel, k_win, v_win)


def _rms(x, g):
    return x * lax.rsqrt(jnp.mean(x * x, axis=-1, keepdims=True) + NORM_EPS) * g


def _norm_body(x_ref, g_ref, o_ref):
    o_ref[...] = _rms(x_ref[...], g_ref[...]).astype(o_ref.dtype)


def rmsnorm(h, g, out_dtype, tm=512):
    N, D = h.shape
    return pl.pallas_call(
        _norm_body,
        out_shape=jax.ShapeDtypeStruct((N, D), out_dtype),
        grid=(N // tm,),
        in_specs=[pl.BlockSpec((tm, D), lambda i: (i, 0)), pl.BlockSpec((1, D), lambda i: (0, 0))],
        out_specs=pl.BlockSpec((tm, D), lambda i: (i, 0)),
        compiler_params=pltpu.CompilerParams(dimension_semantics=("parallel",)),
        name="rmsnorm",
    )(h, g.reshape(1, D))


def _mix_body(x_ref, xp_ref, g_ref, mu_ref, a_ref, o_ref, mid_ref, *, seq, lora_src):
    tm = x_ref.shape[0]
    g = g_ref[...]
    u = _rms(x_ref[...], g)
    last_prev = _rms(xp_ref[...], g)[xp_ref.shape[0] - 1:, :]
    first = lax.rem(pl.program_id(0) * tm, seq) == 0
    row = lax.broadcasted_iota(jnp.int32, (tm, 1), 0)
    prev = jnp.where(row == 0, jnp.where(first, 0.0, last_prev), pltpu.roll(u, 1, 0))
    xx = prev - u
    mix = lambda n: (u + xx * mu_ref[n:n + 1, :]).astype(BF16)
    for n in range(o_ref.shape[0]):
        o_ref[n] = mix(n)
    for l, n in enumerate(lora_src):
        mid = _dot(mix(n), a_ref[l])
        mid_ref[l] = (jnp.tanh(mid) if l == 0 else mid).astype(mid_ref.dtype)


def rwkv_mix(h, g, mu, a_w, lora_src, n_out, seq, tm=256):
    N, D = h.shape
    L, _, R = a_w.shape
    assert seq % tm == 0
    sub = 8
    return pl.pallas_call(
        functools.partial(_mix_body, seq=seq, lora_src=tuple(lora_src)),
        out_shape=(jax.ShapeDtypeStruct((n_out, N, D), BF16), jax.ShapeDtypeStruct((L, N, R), BF16)),
        grid=(N // tm,),
        in_specs=[pl.BlockSpec((tm, D), lambda i: (i, 0)),
                  pl.BlockSpec((sub, D), lambda i: (jnp.maximum(i * (tm // sub) - 1, 0), 0)),
                  pl.BlockSpec((1, D), lambda i: (0, 0)),
                  pl.BlockSpec(mu.shape, lambda i: (0, 0)),
                  pl.BlockSpec((L, D, R), lambda i: (0, 0, 0))],
        out_specs=(pl.BlockSpec((n_out, tm, D), lambda i: (0, i, 0)), pl.BlockSpec((L, tm, R), lambda i: (0, i, 0))),
        compiler_params=pltpu.CompilerParams(dimension_semantics=("parallel",), vmem_limit_bytes=VMEM_LIMIT),
        name="rwkv_mix",
    )(h, h, g.reshape(1, D), mu, a_w)


def _kvpost_body(kv_ref, rc_ref, rs1_ref, rs2_ref, pek_ref, pev_ref, cx_ref, ks_ref, vs_ref, kw_ref, vw_ref):
    tt = kv_ref.shape[0]
    aug = ks_ref.shape[-1] - HD
    tok = pl.program_id(1) * tt + lax.broadcasted_iota(jnp.int32, (tt, aug), 0)
    lane = lax.broadcasted_iota(jnp.int32, (tt, aug), 1)
    onehot = jnp.where(lane == (tok >> (SEL_BLOCK.bit_length() - 1)), 1.0, 0.0).astype(BF16)
    ones_col = jnp.where(lax.broadcasted_iota(jnp.int32, (tt, vs_ref.shape[-1] - HD), 1) == 0, 1.0, 0.0).astype(BF16)
    tabs = (rc_ref[...], rs1_ref[...], rs2_ref[...])
    G = NSA_GROUPS
    n16 = tt // CMP_STRIDE
    for g in range(G):
        for j, pe_ref in ((0, pek_ref), (1, pev_ref)):
            x3 = kv_ref[:, (j * G + g) * HD:(j * G + g + 1) * HD].reshape(n16, CMP_STRIDE, HD)
            x16 = jnp.concatenate([x3[:, l, :] for l in range(CMP_STRIDE)], axis=1)
            cx_ref[j, g] = (x16 + pe_ref[0:1, :]).astype(BF16)
            cx_ref[2 + j, g] = (x16 + pe_ref[1:2, :]).astype(BF16)
        k_s, v_s, k_w, v_w = [kv_ref[:, (j * G + g) * HD:(j * G + g + 1) * HD] for j in range(2, 6)]
        ks_ref[g] = jnp.concatenate([_rope(k_s, *tabs).astype(BF16), onehot], axis=1)
        vs_ref[g] = jnp.concatenate([v_s.astype(BF16), ones_col], axis=1)
        kw_ref[g] = _rope(k_w, *tabs).astype(BF16)
        vw_ref[g] = jnp.concatenate([v_w.astype(BF16), ones_col], axis=1)


def kv_post(kv, rope_t, cmp_pe, batch, tt=512):
    N = kv.shape[0]
    T = N // batch
    G = NSA_GROUPS
    nt = T // tt
    n_blk = T // SEL_BLOCK
    aug = -(-n_blk // 128) * 128
    half = CMP_STRIDE * HD
    pe = cmp_pe.reshape(2, 2, half)
    tab = pl.BlockSpec((tt, HD), lambda b, i: (i, 0))
    pes = pl.BlockSpec((2, half), lambda b, i: (0, 0))
    out4 = lambda w: pl.BlockSpec((None, G, tt, w), lambda b, i: (b, 0, i, 0))
    sds = lambda w: jax.ShapeDtypeStruct((batch, G, T, w), BF16)
    n16 = tt // CMP_STRIDE
    return pl.pallas_call(
        _kvpost_body,
        out_shape=(jax.ShapeDtypeStruct((4, batch, G, T // CMP_STRIDE, half), BF16),
                   sds(HD + aug), sds(2 * HD), sds(HD), sds(2 * HD)),
        grid=(batch, nt),
        in_specs=[pl.BlockSpec((tt, kv.shape[1]), lambda b, i: (b * nt + i, 0)), tab, tab, tab, pes, pes],
        out_specs=(pl.BlockSpec((4, None, G, n16, half), lambda b, i: (0, b, 0, i, 0)),
                   out4(HD + aug), out4(2 * HD), out4(HD), out4(2 * HD)),
        compiler_params=pltpu.CompilerParams(dimension_semantics=("parallel", "parallel"), vmem_limit_bytes=VMEM_LIMIT),
        name="kv_post",
    )(kv, *rope_t, pe[0], pe[1])


def _cmp2_body(lo_ref, hi_ref, w2_ref, rc_ref, rs1_ref, rs2_ref, o_ref):
    n = lo_ref.shape[0]
    hid = lo_ref[...] + pltpu.roll(hi_ref[...], n - 1, 0)
    out = _dot(_silu(hid).astype(BF16), w2_ref[...])
    roped = _rope(out, rc_ref[...], rs1_ref[...], rs2_ref[...])
    o_ref[...] = jnp.where(pl.program_id(0) == 0, roped, out).astype(o_ref.dtype)


def cmp_mlp2(hd, w2, rope_c, groups, ng):
    H = hd.shape[2]
    tab = pl.BlockSpec((ng, HD), lambda j, m: (0, 0))
    return pl.pallas_call(
        _cmp2_body,
        out_shape=jax.ShapeDtypeStruct((2, groups, ng, HD), BF16),
        grid=(2, groups),
        in_specs=[pl.BlockSpec((None, ng, H), lambda j, m: (j, m, 0)),
                  pl.BlockSpec((None, ng, H), lambda j, m: (2 + j, m, 0)),
                  pl.BlockSpec((None, H, HD), lambda j, m: (j, 0, 0)), tab, tab, tab],
        out_specs=pl.BlockSpec((None, None, ng, HD), lambda j, m: (j, m, 0, 0)),
        compiler_params=pltpu.CompilerParams(dimension_semantics=("parallel", "parallel")),
        name="cmp_mlp2",
    )(hd, hd, w2, *rope_c)


def _rope_tables(pos):
    half = ROPE_DIM // 2
    inv = ROPE_THETA ** (-jnp.arange(half, dtype=F32) / half)
    ang = pos.astype(F32)[:, None] * inv[None, :]
    cos, sin = jnp.cos(ang), jnp.sin(ang)
    n = pos.shape[0]
    c = jnp.concatenate([cos, cos, jnp.ones((n, HD - ROPE_DIM), F32)], axis=1)
    s1 = jnp.concatenate([-sin, jnp.zeros((n, HD - half), F32)], axis=1)
    s2 = jnp.concatenate([jnp.zeros((n, half), F32), sin, jnp.zeros((n, HD - ROPE_DIM), F32)], axis=1)
    return c, s1, s2


def _rwkv_layer(h, batch, v_first, norm_g, mu, w_rkvz, w0, w1, w2, a0, a1, a2, vres, k_k, k_a, r_k, ln_g, ln_b, w_o):
    N, D = h.shape
    loras = [(w1, w2, w0), (a1, a2, a0)]
    idx = [4, 5]
    if vres is not None:
        loras.append((vres[1], vres[2], vres[0]))
        idx.append(2)
    rpad = 128
    a_w = jnp.stack([jnp.pad(l[0], ((0, 0), (0, rpad - l[0].shape[1]))) for l in loras]).astype(BF16)
    b_w = jnp.stack([jnp.pad(l[1], ((0, rpad - l[1].shape[0]), (0, 0))) for l in loras]).astype(BF16)
    bias = jnp.stack([l[2].reshape(1, D) for l in loras])

    mixed, mid = rwkv_mix(h, norm_g, mu, a_w, idx, 4, N // batch)
    rkvz = matmul(mixed, w_rkvz.astype(BF16))

    o = wkv(rkvz, mid, b_w, bias, v_first, k_k, k_a, r_k.reshape(-1), ln_g, ln_b, batch)
    return matmul(o, w_o.astype(BF16), res=h), rkvz


def _shared_kv(h, batch, rope_t, kv_norm_g, kv_w, cmp_pe, cmp_w1, cmp_w2):
    N, D = h.shape
    T = N // batch
    G = NSA_GROUPS
    kv = matmul(rmsnorm(h, kv_norm_g, BF16), kv_w.astype(BF16))
    cx, k_sel, v_sel, k_win, v_win = kv_post(kv, rope_t, cmp_pe, batch)

    ng = T // CMP_STRIDE
    half = CMP_STRIDE * HD
    w1 = jnp.concatenate([cmp_w1[:, :half], cmp_w1[:, half:]], axis=0).astype(BF16)
    hd = matmul(cx.reshape(4, batch * G * ng, half), w1)
    rope_c = _rope_tables(jnp.arange(ng) * CMP_STRIDE + CMP_BLOCK - 1)
    cmp = cmp_mlp2(hd, cmp_w2.astype(BF16), rope_c, batch * G, ng).reshape(2, batch, G, ng, HD)
    return cmp[0], cmp[1], k_sel, v_sel, k_win, v_win


def _nsa_layer(h, batch, shared, rope_t, overlap, norm_g, w_in, gate_b, w_o):
    N, D = h.shape
    G = NSA_GROUPS
    width = NSA_HEADS * HD
    ng = N_BRANCH * NSA_HEADS
    per = N_BRANCH * NSA_REP
    w_g = jnp.pad(w_in[:, width:width + ng].reshape(D, G, per), ((0, 0), (0, 0), (0, 128 - per))).reshape(D, G * 128)
    w_all = jnp.concatenate([w_in[:, :width], w_in[:, width + ng:], w_g], axis=1).astype(BF16)
    b_g = jnp.pad(gate_b.reshape(G, per), ((0, 0), (0, 128 - per))).reshape(1, G * 128)
    proj = matmul(rmsnorm(h, norm_g, BF16), w_all, tn=768)
    o = nsa_attention(proj, b_g, rope_t, overlap, shared, batch)
    return matmul(o, w_o.astype(BF16), res=h)


def kernel(x, a_norm_g, a_mu, a_w_rkvz, a_w0, a_w1, a_w2, a_a0, a_a1, a_a2, a_v0, a_v1, a_v2, a_k_k, a_k_a, a_r_k,
           a_ln_g, a_ln_b, a_w_o, kv_norm_g, kv_w, cmp_pe, cmp_w1, cmp_w2, b_norm_g, b_w_in, b_gate_b, b_w_o, final_g):
    B, T, D = x.shape
    N = B * T
    n_a = a_norm_g.shape[0]
    n_b = b_norm_g.shape[0]
    h = x.reshape(N, D)
    v_first = None
    for i in range(n_a):
        vres = None if i == 0 else (a_v0[i - 1], a_v1[i - 1], a_v2[i - 1])
        h, rkvz = _rwkv_layer(h, B, v_first, a_norm_g[i], a_mu[i], a_w_rkvz[i], a_w0[i], a_w1[i], a_w2[i],
                              a_a0[i], a_a1[i], a_a2[i], vres, a_k_k[i], a_k_a[i], a_r_k[i],
                              a_ln_g[i], a_ln_b[i], a_w_o[i])
        if i == 0:
            v_first = rkvz
    if n_b:
        rope_t = _rope_tables(jnp.arange(T))
        shared = _shared_kv(h, B, rope_t, kv_norm_g, kv_w, cmp_pe, cmp_w1, cmp_w2)
        ncmp = T // CMP_STRIDE
        n_blk = T // SEL_BLOCK
        cpos = jnp.arange(ncmp)[:, None] * CMP_STRIDE + jnp.arange(CMP_BLOCK)[None, :]
        overlap = jax.nn.one_hot(cpos // SEL_BLOCK, n_blk, dtype=F32).mean(axis=1).T.astype(BF16)
        for j in range(n_b):
            h = _nsa_layer(h, B, shared, rope_t, overlap, b_norm_g[j], b_w_in[j], b_gate_b[j], b_w_o[j])
    return rmsnorm(h, final_g, F32).reshape(B, T, D)
```

```python
import functools
import math
import types

import jax
import jax.numpy as jnp
from jax import lax
from jax.experimental import pallas as pl
from jax.experimental.pallas import tpu as pltpu

F32 = jnp.float32
BF16 = jnp.bfloat16

NORM_EPS = 1e-6
GN_EPS = 64e-5
HEAD = 64
CHUNK = 64
UNIT = 256
UNIT_HEADS = UNIT // HEAD
assert CHUNK == HEAD
WKV_CHUNKS = 4
WKV_GROUP = 8
WKV_LAG = 10
EXP_M05 = math.exp(-0.5)

NSA_HEADS = 16
NSA_GROUPS = 4
NSA_REP = NSA_HEADS // NSA_GROUPS
HD = 128
N_BRANCH = 3
CMP_BLOCK = 32
CMP_STRIDE = 16
SEL_BLOCK = 64
SEL_TOP_N = 16
WINDOW = 512
QB = 128
NSA_GROUPS_PER_STEP = 2
LOG2E = math.log2(math.e)
ROPE_DIM = HD // 4
ROPE_THETA = 500000.0
NEG = -1e30
FORCE_BONUS = 1e4
SEL_TILE = 512
WIN_BAND = WINDOW + 128

VMEM_LIMIT = 56 * 1024 * 1024


def _dot(a, b):
    return jnp.dot(a, b, preferred_element_type=F32)


def _dot_nt(a, b):
    return lax.dot_general(a, b, (((1,), (1,)), ((), ())), preferred_element_type=F32)


def _dot_tn(a, b):
    return lax.dot_general(a, b, (((0,), (0,)), ((), ())), preferred_element_type=F32)


def _split3(x):
    h = x.astype(BF16)
    r1 = x - h.astype(F32)
    m = r1.astype(BF16)
    l = (r1 - m.astype(F32)).astype(BF16)
    return h, m, l


def _sigmoid(x):
    return 1.0 / (1.0 + jnp.exp(-x))


def _silu(x):
    return x * _sigmoid(x)


def _mm_body(*refs, has_res):
    if has_res:
        a_ref, w_ref, r_ref, o_ref = refs
    else:
        a_ref, w_ref, o_ref = refs
    acc = _dot(a_ref[...].astype(BF16), w_ref[...].astype(BF16))
    if has_res:
        acc = acc + r_ref[...]
    o_ref[...] = acc.astype(o_ref.dtype)


def matmul(a, w, res=None, out_dtype=F32, tm=None, tn=1024):
    squeeze = a.ndim == 2
    if squeeze:
        a, w = a[None], w[None]
        res = None if res is None else res[None]
    G, K, N = w.shape
    M = a.shape[1]
    if tm is None:
        tm = 2048 if res is None and M % 2048 == 0 else 1024
    tm = min(tm, M)
    tn = min(tn, N)
    assert M % tm == 0 and N % tn == 0, (M, N, tm, tn)
    in_specs = [pl.BlockSpec((None, tm, K), lambda g, i, j: (g, i, 0)),
                pl.BlockSpec((None, K, tn), lambda g, i, j: (g, 0, j))]
    args = [a, w]
    if res is not None:
        in_specs.append(pl.BlockSpec((None, tm, tn), lambda g, i, j: (g, i, j)))
        args.append(res)
    out = pl.pallas_call(
        functools.partial(_mm_body, has_res=res is not None),
        out_shape=jax.ShapeDtypeStruct((G, M, N), out_dtype),
        grid=(G, M // tm, N // tn),
        in_specs=in_specs,
        out_specs=pl.BlockSpec((None, tm, tn), lambda g, i, j: (g, i, j)),
        compiler_params=pltpu.CompilerParams(
            dimension_semantics=("parallel", "parallel", "arbitrary"), vmem_limit_bytes=VMEM_LIMIT),
        name="matmul",
    )(*args)
    return out[0] if squeeze else out


def _out_proj_body(a_ref, w_ref, r_ref, g_ref, *o_refs, keep_h):
    h = _dot(a_ref[...], w_ref[...]) + r_ref[...]
    outs = list(o_refs)
    if keep_h:
        outs.pop(0)[...] = h
    y = h * lax.rsqrt(jnp.mean(h * h, axis=-1, keepdims=True) + NORM_EPS)
    for n, o_ref in enumerate(outs):
        o_ref[...] = (y * g_ref[n:n + 1, :]).astype(o_ref.dtype)


def out_proj(a, w, res, gains, final, tm=512):
    if not gains:
        return matmul(a, w, res=res), []
    M, K = a.shape
    N = w.shape[1]
    tm = min(tm, M)
    row = pl.BlockSpec((tm, N), lambda i: (i, 0))
    dt = F32 if final else BF16
    out_shape = [jax.ShapeDtypeStruct((M, N), dt) for _ in gains]
    if not final:
        out_shape.insert(0, jax.ShapeDtypeStruct((M, N), F32))
    outs = pl.pallas_call(
        functools.partial(_out_proj_body, keep_h=not final),
        out_shape=tuple(out_shape),
        grid=(M // tm,),
        in_specs=[pl.BlockSpec((tm, K), lambda i: (i, 0)), pl.BlockSpec((K, N), lambda i: (0, 0)), row,
                  pl.BlockSpec((len(gains), N), lambda i: (0, 0))],
        out_specs=tuple(row for _ in out_shape),
        compiler_params=pltpu.CompilerParams(dimension_semantics=("parallel",), vmem_limit_bytes=VMEM_LIMIT),
        name="out_proj",
    )(a, w, res, jnp.stack(gains))
    return (None, list(outs)) if final else (outs[0], list(outs[1:]))


def _expand(x, lane_head):
    return jnp.concatenate([jnp.where(lane_head == h, x, 0.0) for h in range(UNIT_HEADS)], axis=0)


def _segsums(xs, ones_bd):
    pieces = []
    for x in xs:
        h = x.astype(BF16)
        pieces += [h, (x - h.astype(F32)).astype(BF16)]
    out = _dot(jnp.concatenate(pieces, axis=0), ones_bd)
    n = xs[0].shape[0]
    return [out[2 * i * n:(2 * i + 1) * n] + out[(2 * i + 1) * n:(2 * i + 2) * n] for i in range(len(xs))]


def _wkv_unit_stages(r, k, v, z, lora, vf, k_k, k_a, r_k, ln_g, ln_b, s_ref, o_ref, consts, turn):
    state_written, chunk_idx = turn
    tri, ones_bd, same, strict, incl, lane_head, eye = consts
    bf = lambda x: x.astype(BF16)
    ex = lambda x: _expand(x, lane_head)

    lw = -EXP_M05 * _sigmoid(lora(0))
    a = _sigmoid(lora(1))
    v = v[...]
    if vf is not None:
        v = v + (vf[...] - v) * _sigmoid(lora(2))
    yield
    k = k[...]
    r = r[...]
    kk = k * k_k[...]
    k2 = k * (1.0 + (a - 1.0) * k_a[...])
    n2, bonus = _segsums([kk * kk, r * k2 * r_k[...]], ones_bd)
    yield
    lh, lm, ll = _split3(lw)
    cum = _dot(tri, lh) + _dot(tri, lm) + _dot(tri, ll)
    yield

    kk = kk / jnp.maximum(jnp.sqrt(n2), 1e-12)
    p_incl = jnp.exp(cum)
    p_inv = jnp.exp(-cum)
    kt = k2 * p_inv
    bt = kk * a * p_inv
    at = -kk * jnp.exp(cum - lw)
    rt = r * p_incl
    eb = bf(ex(bt))
    ek = bf(ex(kt))
    ev = bf(ex(v))
    yield

    top, bot = slice(0, CHUNK), slice(CHUNK, 2 * CHUNK)
    ar = bf(jnp.concatenate([at, rt], axis=0))
    sb = _dot_nt(ar, eb)
    yield
    sk = _dot_nt(ar, ek)
    yield
    a_ab = jnp.where(strict, sb[top], 0.0)
    a_rb = bf(jnp.where(incl, sb[bot], 0.0))
    kv2 = _dot(bf(jnp.concatenate([jnp.where(strict, sk[top], 0.0), jnp.where(incl, sk[bot], 0.0)], axis=0)), ev)
    akv, rkv = kv2[top], kv2[bot]
    yield

    tm = eye + a_ab
    pw = _dot(bf(a_ab), bf(ex(a_ab)))
    yield
    for _ in range(4):
        both = _dot(bf(jnp.concatenate([pw, tm], axis=0)), bf(ex(pw)))
        pw, tm = both[top], tm + both[bot]
        yield
    tm = tm + _dot(bf(tm), bf(ex(pw)))
    yield

    assert len(state_written) == chunk_idx, "stage skew too small: state read before the previous chunk's update"
    s = s_ref[...]
    xs = _dot_nt(ar, bf(s))
    yield
    u = _dot(bf(tm), bf(ex(akv + xs[top])))
    yield
    o = xs[bot] + _dot(a_rb, bf(ex(u))) + rkv
    yield
    upd = _dot_tn(bf(jnp.concatenate([v, u], axis=0)), bf(jnp.concatenate([kt, bt], axis=0)))
    yield
    s_ref[...] = (s + jnp.where(same, upd, 0.0)) * p_incl[CHUNK - 1:CHUNK, :]
    state_written.append(chunk_idx)
    mean = _segsums([o], ones_bd)[0] * (1.0 / HEAD)
    yield

    oc = o - mean
    var = _segsums([oc * oc], ones_bd)[0] * (1.0 / HEAD)
    yield

    o_ref[...] = ((oc * lax.rsqrt(var + GN_EPS) * ln_g[...] + ln_b[...] + bonus * v)
                  * _silu(z[...])).astype(o_ref.dtype)


def _run_skewed(gens, group, lag):
    done = [False] * len(gens)
    rnd = 0
    while not all(done):
        for u, g in enumerate(gens):
            if (u // group) * lag <= rnd and not done[u]:
                try:
                    next(g)
                except StopIteration:
                    done[u] = True
        rnd += 1


def _wkv_body(*refs, has_vres, units):
    if has_vres:
        (r_ref, k_ref, v_ref, z_ref, mid_ref, bw_ref, bias_ref, vf_ref,
         kk_ref, ka_ref, rk_ref, lg_ref, lb_ref, o_ref, s_ref) = refs
    else:
        (r_ref, k_ref, v_ref, z_ref, mid_ref, bw_ref, bias_ref,
         kk_ref, ka_ref, rk_ref, lg_ref, lb_ref, o_ref, s_ref) = refs
        vf_ref = None

    @pl.when(pl.program_id(2) == 0)
    def _():
        s_ref[...] = jnp.zeros_like(s_ref)

    row = lax.broadcasted_iota(jnp.int32, (UNIT, UNIT), 0)
    col = lax.broadcasted_iota(jnp.int32, (UNIT, UNIT), 1)
    same = (row >> 6) == (col >> 6)
    ones_bd = jnp.where(same, 1.0, 0.0).astype(BF16)
    tr = lax.broadcasted_iota(jnp.int32, (CHUNK, CHUNK), 0)
    tc = lax.broadcasted_iota(jnp.int32, (CHUNK, CHUNK), 1)
    tri = jnp.where(tc <= tr, 1.0, 0.0).astype(BF16)
    lane = lax.broadcasted_iota(jnp.int32, (CHUNK, UNIT), 1)
    lane_head = lane >> 6
    t_row = lax.broadcasted_iota(jnp.int32, (CHUNK, UNIT), 0)
    s_lane = lane & (CHUNK - 1)
    strict = s_lane < t_row
    incl = s_lane <= t_row
    eye = jnp.where(s_lane == t_row, 1.0, 0.0).astype(F32)
    consts = (tri, ones_bd, same, strict, incl, lane_head, eye)

    def view(ref, cc, uu):
        if ref is None:
            return None
        rows = pl.ds(cc * CHUNK, CHUNK) if ref.shape[0] > 1 else slice(None)
        return ref.at[rows, pl.ds(uu * UNIT, UNIT)]

    def lora(cc, uu):
        lanes = pl.ds(uu * UNIT, UNIT)
        return lambda l: (_dot(mid_ref[l, pl.ds(cc * CHUNK, CHUNK), :], bw_ref[l, :, lanes]) + bias_ref[l, :, lanes])

    written = [[] for _ in range(units)]
    _run_skewed([
        _wkv_unit_stages(*(view(ref, cc, uu) for ref in (r_ref, k_ref, v_ref, z_ref)), lora(cc, uu),
                         *(view(ref, cc, uu) for ref in (vf_ref, kk_ref, ka_ref, rk_ref, lg_ref, lb_ref)),
                         s_ref.at[uu], view(o_ref, cc, uu), consts, (written[uu], cc))
        for cc in range(WKV_CHUNKS) for uu in range(units)], WKV_GROUP, WKV_LAG)


def wkv(rkvz, mid, b_w, bias, v_first, k_k, k_a, r_k, ln_g, ln_b, batch, units=8):
    _, N, D = rkvz.shape
    L, _, R = mid.shape
    T = N // batch
    step_rows = WKV_CHUNKS * CHUNK
    nc = T // step_rows
    W = units * UNIT
    has_vres = v_first is not None
    assert L == (3 if has_vres else 2)
    row_map = lambda b, u, c: (b * nc + c, u)

    def lead(n):
        return pl.BlockSpec((None, step_rows, W), lambda b, u, c: (n, b * nc + c, u))

    in_specs = [lead(0), lead(1), lead(2), lead(3),
                pl.BlockSpec((L, step_rows, R), lambda b, u, c: (0, b * nc + c, 0)),
                pl.BlockSpec((L, R, W), lambda b, u, c: (0, 0, u)),
                pl.BlockSpec((L, 1, W), lambda b, u, c: (0, 0, u))]
    args = [rkvz, rkvz, rkvz, rkvz, mid, b_w, bias]
    if has_vres:
        in_specs += [lead(2)]
        args += [v_first]
    par = pl.BlockSpec((1, W), lambda b, u, c: (0, u))
    in_specs += [par] * 5
    args += [k_k.reshape(1, D), k_a.reshape(1, D), r_k.reshape(1, D), ln_g.reshape(1, D), ln_b.reshape(1, D)]
    return pl.pallas_call(
        functools.partial(_wkv_body, has_vres=has_vres, units=units),
        out_shape=jax.ShapeDtypeStruct((N, D), BF16),
        grid=(batch, D // W, nc),
        in_specs=in_specs,
        out_specs=pl.BlockSpec((step_rows, W), row_map),
        scratch_shapes=[pltpu.VMEM((units, UNIT, UNIT), F32)],
        compiler_params=pltpu.CompilerParams(
            dimension_semantics=("parallel", "parallel", "arbitrary"), vmem_limit_bytes=VMEM_LIMIT),
        name="wkv7",
    )(*args)


def _attend(s, v_aug):
    m = jnp.max(s, axis=1, keepdims=True)
    acc = _dot(jnp.exp2(s - m).astype(BF16), v_aug)
    return acc[:, :HD] * (1.0 / acc[:, HD:HD + 1])


def _rope(x, c, s1, s2):
    n = x.shape[1] // HD
    if n > 1:
        c, s1, s2 = (jnp.concatenate([t] * n, axis=1) for t in (c, s1, s2))
    half = ROPE_DIM // 2
    return x * c + pltpu.roll(x, x.shape[1] - half, 1) * s1 + pltpu.roll(x, half, 1) * s2


def _tile_rows(t):
    return pl.ds(pl.multiple_of(t * SEL_TILE, SEL_TILE), SEL_TILE)


def _nsa_scores(c, t, dst):
    dst[...] = _dot_nt(c.qa[...], c.ks[_tile_rows(t), :])


def _nsa_absorb(c, t, s):
    m = c.m[...]
    m_new = jnp.maximum(m, jnp.max(s, axis=1, keepdims=True))
    pv = _dot(jnp.exp2(s - m_new).astype(BF16), c.vs[_tile_rows(t), :])
    c.acc[...] = jnp.exp2(m - m_new) * c.acc[...] + pv
    c.m[...] = m_new


def _nsa_pair(c, t):
    _nsa_scores(c, t + 1, c.s1)
    yield
    _nsa_absorb(c, t, c.s0[...])
    yield
    _nsa_scores(c, t + 2, c.s0)
    yield
    _nsa_absorb(c, t + 1, c.s1[...])
    yield


def _nsa_first_of_odd(c):
    _nsa_scores(c, 1, c.s0)
    yield
    _nsa_absorb(c, 0, c.s1[...])
    yield


def _add_rows(s, bias):
    return jnp.concatenate([s[r * QB:(r + 1) * QB] + bias for r in range(NSA_REP)], axis=0)


def _nsa_head(c, i, tq_col, tabs, ov_t, cmp_bias, win_bias):
    rows = NSA_REP * QB
    q4 = _rope(c.q[...], *tabs) * (HD ** -0.5 * LOG2E)
    qs = jnp.concatenate([q4[:, r * HD:(r + 1) * HD] for r in range(NSA_REP)], axis=0)
    qb = qs.astype(BF16)

    band = WIN_BAND
    w0 = pl.multiple_of(jnp.maximum((i + 1) * QB - band, 0), QB)
    s_cmp = _dot_nt(qb, c.kc[...])
    yield
    s_win = _dot_nt(qb, c.kw[pl.ds(w0, band), :])
    yield

    s_cmp = _add_rows(s_cmp, cmp_bias)
    e = jnp.exp2(s_cmp - jnp.max(s_cmp, axis=1, keepdims=True))
    p = e * (jnp.where(tq_col >= CMP_BLOCK - 1, 1.0, 0.0) / jnp.sum(e, axis=1, keepdims=True))
    c.o_cmp = _dot(p.astype(BF16), c.vc[...])
    yield

    psum = p[0:QB]
    for r in range(1, NSA_REP):
        psum = psum + p[r * QB:(r + 1) * QB]
    ph, pm, pl_ = _split3(psum)
    n_blk = ov_t.shape[0]
    imp_t = _dot_nt(ov_t, ph) + _dot_nt(ov_t, pm) + _dot_nt(ov_t, pl_)
    yield

    c.o_win = _attend(_add_rows(s_win, win_bias), c.vw[pl.ds(w0, band), :])
    yield

    jsub = lax.broadcasted_iota(jnp.int32, (n_blk, QB), 0)
    cur = (i * QB + lax.broadcasted_iota(jnp.int32, (n_blk, QB), 1)) >> (SEL_BLOCK.bit_length() - 1)
    forced = (jsub == 0) | (jsub == cur) | (jsub == cur - 1)
    imp_t = jnp.where(forced, FORCE_BONUS, jnp.where(jsub > cur, NEG, imp_t))
    sub8 = lax.broadcasted_iota(jnp.int32, (8, QB), 0)
    grp = [imp_t[8 * a:8 * a + 8] for a in range(n_blk // 8)]
    cnt = [jnp.zeros((8, QB), F32) for _ in grp]
    for jp in range(n_blk):
        rowv = jnp.broadcast_to(imp_t[jp:jp + 1, :], (8, QB))
        for a, x in enumerate(grp):
            ge = lambda: jnp.where(rowv >= x, 1.0, 0.0)
            gt = lambda: jnp.where(rowv > x, 1.0, 0.0)
            if 8 * a > jp:
                inc = ge()
            elif 8 * a + 7 <= jp:
                inc = gt()
            else:
                inc = jnp.where(sub8 + 8 * a > jp, ge(), gt())
            cnt[a] = cnt[a] + inc
    cnt = jnp.concatenate(cnt, axis=0)
    bias_t = jnp.where(cnt < float(min(SEL_TOP_N, n_blk)), 0.0, NEG).astype(BF16)
    qi = lax.broadcasted_iota(jnp.int32, (QB, QB), 0)
    qj = lax.broadcasted_iota(jnp.int32, (QB, QB), 1)
    eye_q = jnp.where(qi == qj, 1.0, 0.0).astype(BF16)
    bias = _dot_nt(eye_q, bias_t)
    yield
    bias4 = jnp.concatenate([bias] * NSA_REP, axis=0).astype(BF16)
    pad = c.ks.shape[1] - HD - n_blk
    parts = [qb, bias4] + ([jnp.zeros((rows, pad), BF16)] if pad else [])
    qa = jnp.concatenate(parts, axis=1)

    c.qa[...] = qa
    s_first = _dot_nt(qa, c.ks[_tile_rows(0), :])
    c.s0[...] = s_first
    c.s1[...] = s_first
    c.m[...] = jnp.full(c.m.shape, NEG, F32)
    c.acc[...] = jnp.zeros(c.acc.shape, F32)
    yield


def _nsa_tail(c, t_last, diag_bias):
    _nsa_absorb(c, t_last, _add_rows(c.s0[...], diag_bias))
    yield
    acc = c.acc[...]
    o_sel = acc[:, :HD] * (1.0 / acc[:, HD:HD + 1])
    g = _sigmoid(c.g[...] + c.gb[...])
    outs = []
    for r in range(NSA_REP):
        rs = slice(r * QB, (r + 1) * QB)
        outs.append(g[:, 3 * r:3 * r + 1] * c.o_cmp[rs] + g[:, 3 * r + 1:3 * r + 2] * o_sel[rs]
                    + g[:, 3 * r + 2:3 * r + 3] * c.o_win[rs])
    o4 = jnp.concatenate(outs, axis=1)
    c.o[...] = (o4 * _silu(c.z[...])).astype(c.o.dtype)


def _nsa_body(q_ref, z_ref, g_ref, gb_ref, rc_ref, rs1_ref, rs2_ref, ovt_ref,
              kc_ref, vc_ref, ks_ref, vs_ref, kw_ref, vw_ref, o_ref, qa_scr, s0_scr, s1_scr, m_scr, acc_scr):
    i = pl.program_id(2)
    rows = NSA_REP * QB
    W = NSA_REP * HD
    n_grp = kc_ref.shape[0]
    grps = []
    for g in range(n_grp):
        lanes = pl.ds(g * W, W)
        glanes = pl.ds(g * 128, 128)
        grps.append(types.SimpleNamespace(
            q=q_ref.at[:, lanes], z=z_ref.at[:, lanes], g=g_ref.at[:, glanes], gb=gb_ref.at[:, glanes],
            kc=kc_ref.at[g], vc=vc_ref.at[g], ks=ks_ref.at[g], vs=vs_ref.at[g], kw=kw_ref.at[g], vw=vw_ref.at[g],
            o=o_ref.at[:, lanes], qa=qa_scr.at[g], s0=s0_scr.at[g], s1=s1_scr.at[g], m=m_scr.at[g], acc=acc_scr.at[g]))
    lock = lambda gens: _run_skewed(gens, n_grp, 0)

    tq_col = i * QB + (lax.broadcasted_iota(jnp.int32, (rows, 1), 0) & (QB - 1))
    tabs = (rc_ref[...], rs1_ref[...], rs2_ref[...])

    t_last = (i * QB) // SEL_TILE
    odd = (t_last & 1) == 1
    tq = i * QB + lax.broadcasted_iota(jnp.int32, (QB, 1), 0)
    ncmp = kc_ref.shape[1]
    cend = lax.broadcasted_iota(jnp.int32, (QB, ncmp), 1) * CMP_STRIDE + (CMP_BLOCK - 1)
    cmp_bias = jnp.where(cend <= tq, 0.0, NEG)
    w0 = jnp.maximum((i + 1) * QB - WIN_BAND, 0)
    diff = (tq - w0) - lax.broadcasted_iota(jnp.int32, (QB, WIN_BAND), 1)
    win_bias = jnp.where(lax.bitcast_convert_type(diff, jnp.uint32) < jnp.uint32(WINDOW), 0.0, NEG)
    kpos = t_last * SEL_TILE + lax.broadcasted_iota(jnp.int32, (QB, SEL_TILE), 1)
    diag_bias = jnp.where(kpos <= tq, 0.0, NEG)

    lock([_nsa_head(c, i, tq_col, tabs, ovt_ref[...], cmp_bias, win_bias) for c in grps])

    @pl.when(odd)
    def _():
        lock([_nsa_first_of_odd(c) for c in grps])

    def pair(t2, carry):
        lock([_nsa_pair(c, 2 * t2 + (t_last & 1)) for c in grps])
        return carry

    lax.fori_loop(0, t_last // 2, pair, 0)
    lock([_nsa_tail(c, t_last, diag_bias) for c in grps])


def nsa_attention(proj, gate_b, rope_q, overlap, shared, batch):
    N = proj.shape[0]
    T = N // batch
    nq = T // QB
    W = NSA_REP * HD
    k_cmp, v_cmp, k_sel, v_sel, k_win, v_win = shared
    rc, rs1, rs2 = rope_q
    ncmp = k_cmp.shape[2]
    n_blk = T // SEL_BLOCK
    P = NSA_GROUPS_PER_STEP
    steps = NSA_GROUPS // P
    rows = NSA_REP * QB
    row_blk = lambda b, g, i: (b * nq + i, g)
    kv = lambda w: pl.BlockSpec((None, P, T, w), lambda b, g, i: (b, g, 0, 0))
    cm = pl.BlockSpec((None, P, ncmp, HD), lambda b, g, i: (b, g, 0, 0))
    rope = pl.BlockSpec((QB, HD), lambda b, g, i: (i, 0))
    gate_blk0 = 2 * NSA_HEADS * HD // (P * 128)
    return pl.pallas_call(
        _nsa_body,
        out_shape=jax.ShapeDtypeStruct((N, NSA_HEADS * HD), BF16),
        grid=(batch, steps, nq),
        in_specs=[pl.BlockSpec((QB, P * W), row_blk),
                  pl.BlockSpec((QB, P * W), lambda b, g, i: (b * nq + i, steps + g)),
                  pl.BlockSpec((QB, P * 128), lambda b, g, i: (b * nq + i, gate_blk0 + g)),
                  pl.BlockSpec((1, P * 128), lambda b, g, i: (0, g)),
                  rope, rope, rope,
                  pl.BlockSpec((n_blk, ncmp), lambda b, g, i: (0, 0)),
                  cm, cm, kv(k_sel.shape[3]), kv(v_sel.shape[3]), kv(k_win.shape[3]), kv(v_win.shape[3])],
        out_specs=pl.BlockSpec((QB, P * W), row_blk),
        scratch_shapes=[pltpu.VMEM((P, rows, k_sel.shape[3]), BF16),
                        pltpu.VMEM((P, rows, SEL_TILE), F32),
                        pltpu.VMEM((P, rows, SEL_TILE), F32),
                        pltpu.VMEM((P, rows, 1), F32),
                        pltpu.VMEM((P, rows, v_sel.shape[3]), F32)],
        compiler_params=pltpu.CompilerParams(
            dimension_semantics=("parallel", "parallel", "arbitrary"), vmem_limit_bytes=VMEM_LIMIT),
        name="nsa_attention",
    )(proj, proj, proj, gate_b, rc, rs1, rs2, overlap, k_cmp, v_cmp, k_sel, v_sel, k_win, v_win)


def _rms(x, g):
    return x * lax.rsqrt(jnp.mean(x * x, axis=-1, keepdims=True) + NORM_EPS) * g


def _norm_body(x_ref, g_ref, o_ref):
    o_ref[...] = _rms(x_ref[...], g_ref[...]).astype(o_ref.dtype)


def rmsnorm(h, g, out_dtype, tm=512):
    N, D = h.shape
    return pl.pallas_call(
        _norm_body,
        out_shape=jax.ShapeDtypeStruct((N, D), out_dtype),
        grid=(N // tm,),
        in_specs=[pl.BlockSpec((tm, D), lambda i: (i, 0)), pl.BlockSpec((1, D), lambda i: (0, 0))],
        out_specs=pl.BlockSpec((tm, D), lambda i: (i, 0)),
        compiler_params=pltpu.CompilerParams(dimension_semantics=("parallel",)),
        name="rmsnorm",
    )(h, g.reshape(1, D))


def _mix_body(x_ref, xp_ref, g_ref, mu_ref, a_ref, o_ref, mid_ref, *, seq, lora_src):
    tm = x_ref.shape[0]
    g = g_ref[...]
    u = _rms(x_ref[...], g)
    last_prev = _rms(xp_ref[...], g)[xp_ref.shape[0] - 1:, :]
    first = lax.rem(pl.program_id(0) * tm, seq) == 0
    row = lax.broadcasted_iota(jnp.int32, (tm, 1), 0)
    prev = jnp.where(row == 0, jnp.where(first, 0.0, last_prev), pltpu.roll(u, 1, 0))
    xx = prev - u
    mix = lambda n: (u + xx * mu_ref[n:n + 1, :]).astype(BF16)
    for n in range(o_ref.shape[0]):
        o_ref[n] = mix(n)
    for l, n in enumerate(lora_src):
        mid = _dot(mix(n), a_ref[l])
        mid_ref[l] = (jnp.tanh(mid) if l == 0 else mid).astype(mid_ref.dtype)


def rwkv_mix(h, g, mu, a_w, lora_src, n_out, seq, tm=256):
    N, D = h.shape
    L, _, R = a_w.shape
    assert seq % tm == 0
    sub = 8
    return pl.pallas_call(
        functools.partial(_mix_body, seq=seq, lora_src=tuple(lora_src)),
        out_shape=(jax.ShapeDtypeStruct((n_out, N, D), BF16), jax.ShapeDtypeStruct((L, N, R), BF16)),
        grid=(N // tm,),
        in_specs=[pl.BlockSpec((tm, D), lambda i: (i, 0)),
                  pl.BlockSpec((sub, D), lambda i: (jnp.maximum(i * (tm // sub) - 1, 0), 0)),
                  pl.BlockSpec((1, D), lambda i: (0, 0)),
                  pl.BlockSpec(mu.shape, lambda i: (0, 0)),
                  pl.BlockSpec((L, D, R), lambda i: (0, 0, 0))],
        out_specs=(pl.BlockSpec((n_out, tm, D), lambda i: (0, i, 0)), pl.BlockSpec((L, tm, R), lambda i: (0, i, 0))),
        compiler_params=pltpu.CompilerParams(dimension_semantics=("parallel",), vmem_limit_bytes=VMEM_LIMIT),
        name="rwkv_mix",
    )(h, h, g.reshape(1, D), mu, a_w)


def _kvpost_body(kv_ref, rc_ref, rs1_ref, rs2_ref, pek_ref, pev_ref, cx_ref, ks_ref, vs_ref, kw_ref, vw_ref):
    tt = kv_ref.shape[0]
    aug = ks_ref.shape[-1] - HD
    tok = pl.program_id(1) * tt + lax.broadcasted_iota(jnp.int32, (tt, aug), 0)
    lane = lax.broadcasted_iota(jnp.int32, (tt, aug), 1)
    onehot = jnp.where(lane == (tok >> (SEL_BLOCK.bit_length() - 1)), 1.0, 0.0).astype(BF16)
    ones_col = jnp.where(lax.broadcasted_iota(jnp.int32, (tt, vs_ref.shape[-1] - HD), 1) == 0, 1.0, 0.0).astype(BF16)
    tabs = (rc_ref[...], rs1_ref[...], rs2_ref[...])
    G = NSA_GROUPS
    n16 = tt // CMP_STRIDE
    for g in range(G):
        for j, pe_ref in ((0, pek_ref), (1, pev_ref)):
            x3 = kv_ref[:, (j * G + g) * HD:(j * G + g + 1) * HD].reshape(n16, CMP_STRIDE, HD)
            x16 = jnp.concatenate([x3[:, l, :] for l in range(CMP_STRIDE)], axis=1)
            cx_ref[j, g] = (x16 + pe_ref[0:1, :]).astype(BF16)
            cx_ref[2 + j, g] = (x16 + pe_ref[1:2, :]).astype(BF16)
        k_s, v_s, k_w, v_w = [kv_ref[:, (j * G + g) * HD:(j * G + g + 1) * HD] for j in range(2, 6)]
        ks_ref[g] = jnp.concatenate([_rope(k_s, *tabs).astype(BF16), onehot], axis=1)
        vs_ref[g] = jnp.concatenate([v_s.astype(BF16), ones_col], axis=1)
        kw_ref[g] = _rope(k_w, *tabs).astype(BF16)
        vw_ref[g] = jnp.concatenate([v_w.astype(BF16), ones_col], axis=1)


def kv_post(kv, rope_t, cmp_pe, batch, tt=512):
    N = kv.shape[0]
    T = N // batch
    G = NSA_GROUPS
    nt = T // tt
    n_blk = T // SEL_BLOCK
    aug = -(-n_blk // 128) * 128
    half = CMP_STRIDE * HD
    pe = cmp_pe.reshape(2, 2, half)
    tab = pl.BlockSpec((tt, HD), lambda b, i: (i, 0))
    pes = pl.BlockSpec((2, half), lambda b, i: (0, 0))
    out4 = lambda w: pl.BlockSpec((None, G, tt, w), lambda b, i: (b, 0, i, 0))
    sds = lambda w: jax.ShapeDtypeStruct((batch, G, T, w), BF16)
    n16 = tt // CMP_STRIDE
    return pl.pallas_call(
        _kvpost_body,
        out_shape=(jax.ShapeDtypeStruct((4, batch, G, T // CMP_STRIDE, half), BF16),
                   sds(HD + aug), sds(2 * HD), sds(HD), sds(2 * HD)),
        grid=(batch, nt),
        in_specs=[pl.BlockSpec((tt, kv.shape[1]), lambda b, i: (b * nt + i, 0)), tab, tab, tab, pes, pes],
        out_specs=(pl.BlockSpec((4, None, G, n16, half), lambda b, i: (0, b, 0, i, 0)),
                   out4(HD + aug), out4(2 * HD), out4(HD), out4(2 * HD)),
        compiler_params=pltpu.CompilerParams(dimension_semantics=("parallel", "parallel"), vmem_limit_bytes=VMEM_LIMIT),
        name="kv_post",
    )(kv, *rope_t, pe[0], pe[1])


def _cmp2_body(lo_ref, hi_ref, w2_ref, rc_ref, rs1_ref, rs2_ref, o_ref):
    n = lo_ref.shape[0]
    hid = lo_ref[...] + pltpu.roll(hi_ref[...], n - 1, 0)
    out = _dot(_silu(hid).astype(BF16), w2_ref[...])
    roped = _rope(out, rc_ref[...], rs1_ref[...], rs2_ref[...])
    o_ref[...] = jnp.where(pl.program_id(0) == 0, roped, out).astype(o_ref.dtype)


def cmp_mlp2(hd, w2, rope_c, groups, ng):
    H = hd.shape[2]
    tab = pl.BlockSpec((ng, HD), lambda j, m: (0, 0))
    return pl.pallas_call(
        _cmp2_body,
        out_shape=jax.ShapeDtypeStruct((2, groups, ng, HD), BF16),
        grid=(2, groups),
        in_specs=[pl.BlockSpec((None, ng, H), lambda j, m: (j, m, 0)),
                  pl.BlockSpec((None, ng, H), lambda j, m: (2 + j, m, 0)),
                  pl.BlockSpec((None, H, HD), lambda j, m: (j, 0, 0)), tab, tab, tab],
        out_specs=pl.BlockSpec((None, None, ng, HD), lambda j, m: (j, m, 0, 0)),
        compiler_params=pltpu.CompilerParams(dimension_semantics=("parallel", "parallel")),
        name="cmp_mlp2",
    )(hd, hd, w2, *rope_c)


def _rope_tables(pos):
    half = ROPE_DIM // 2
    inv = ROPE_THETA ** (-jnp.arange(half, dtype=F32) / half)
    ang = pos.astype(F32)[:, None] * inv[None, :]
    cos, sin = jnp.cos(ang), jnp.sin(ang)
    n = pos.shape[0]
    c = jnp.concatenate([cos, cos, jnp.ones((n, HD - ROPE_DIM), F32)], axis=1)
    s1 = jnp.concatenate([-sin, jnp.zeros((n, HD - half), F32)], axis=1)
    s2 = jnp.concatenate([jnp.zeros((n, half), F32), sin, jnp.zeros((n, HD - ROPE_DIM), F32)], axis=1)
    return c, s1, s2


def _rwkv_layer(h, batch, v_first, norm_g, mu, w_rkvz, w0, w1, w2, a0, a1, a2, vres, k_k, k_a, r_k, ln_g, ln_b, w_o,
                gains, final):
    N, D = h.shape
    loras = [(w1, w2, w0), (a1, a2, a0)]
    idx = [4, 5]
    if vres is not None:
        loras.append((vres[1], vres[2], vres[0]))
        idx.append(2)
    rpad = 128
    a_w = jnp.stack([jnp.pad(l[0], ((0, 0), (0, rpad - l[0].shape[1]))) for l in loras]).astype(BF16)
    b_w = jnp.stack([jnp.pad(l[1], ((0, rpad - l[1].shape[0]), (0, 0))) for l in loras]).astype(BF16)
    bias = jnp.stack([l[2].reshape(1, D) for l in loras])

    mixed, mid = rwkv_mix(h, norm_g, mu, a_w, idx, 4, N // batch)
    rkvz = matmul(mixed, w_rkvz.astype(BF16))

    o = wkv(rkvz, mid, b_w, bias, v_first, k_k, k_a, r_k.reshape(-1), ln_g, ln_b, batch)
    h, normed = out_proj(o, w_o.astype(BF16), h, gains, final)
    return h, normed, rkvz


def _shared_kv(hn, batch, rope_t, kv_w, cmp_pe, cmp_w1, cmp_w2):
    N, D = hn.shape
    T = N // batch
    G = NSA_GROUPS
    kv = matmul(hn, kv_w.astype(BF16))
    cx, k_sel, v_sel, k_win, v_win = kv_post(kv, rope_t, cmp_pe, batch)

    ng = T // CMP_STRIDE
    half = CMP_STRIDE * HD
    w1 = jnp.concatenate([cmp_w1[:, :half], cmp_w1[:, half:]], axis=0).astype(BF16)
    hd = matmul(cx.reshape(4, batch * G * ng, half), w1)
    rope_c = _rope_tables(jnp.arange(ng) * CMP_STRIDE + CMP_BLOCK - 1)
    cmp = cmp_mlp2(hd, cmp_w2.astype(BF16), rope_c, batch * G, ng).reshape(2, batch, G, ng, HD)
    return cmp[0], cmp[1], k_sel, v_sel, k_win, v_win


def _nsa_layer(h, u, batch, shared, rope_t, overlap, w_in, gate_b, w_o, gains, final):
    N, D = h.shape
    G = NSA_GROUPS
    width = NSA_HEADS * HD
    ng = N_BRANCH * NSA_HEADS
    per = N_BRANCH * NSA_REP
    w_g = jnp.pad(w_in[:, width:width + ng].reshape(D, G, per), ((0, 0), (0, 0), (0, 128 - per))).reshape(D, G * 128)
    w_all = jnp.concatenate([w_in[:, :width], w_in[:, width + ng:], w_g], axis=1).astype(BF16)
    b_g = jnp.pad(gate_b.reshape(G, per), ((0, 0), (0, 128 - per))).reshape(1, G * 128)
    proj = matmul(u, w_all, tn=768)
    o = nsa_attention(proj, b_g, rope_t, overlap, shared, batch)
    return out_proj(o, w_o.astype(BF16), h, gains, final)


def kernel(x, a_norm_g, a_mu, a_w_rkvz, a_w0, a_w1, a_w2, a_a0, a_a1, a_a2, a_v0, a_v1, a_v2, a_k_k, a_k_a, a_r_k,
           a_ln_g, a_ln_b, a_w_o, kv_norm_g, kv_w, cmp_pe, cmp_w1, cmp_w2, b_norm_g, b_w_in, b_gate_b, b_w_o, final_g):
    B, T, D = x.shape
    N = B * T
    n_a = a_norm_g.shape[0]
    n_b = b_norm_g.shape[0]
    h = x.reshape(N, D)
    v_first = None
    normed = None
    for i in range(n_a):
        vres = None if i == 0 else (a_v0[i - 1], a_v1[i - 1], a_v2[i - 1])
        last = i == n_a - 1
        gains = ([kv_norm_g, b_norm_g[0]] if n_b else [final_g]) if last else []
        h, normed, rkvz = _rwkv_layer(h, B, v_first, a_norm_g[i], a_mu[i], a_w_rkvz[i], a_w0[i], a_w1[i], a_w2[i],
                                      a_a0[i], a_a1[i], a_a2[i], vres, a_k_k[i], a_k_a[i], a_r_k[i],
                                      a_ln_g[i], a_ln_b[i], a_w_o[i], gains, last and not n_b)
        if i == 0:
            v_first = rkvz
    if not n_b:
        return (normed[0] if n_a else rmsnorm(h, final_g, F32)).reshape(B, T, D)
    if not n_a:
        normed = [rmsnorm(h, kv_norm_g, BF16), rmsnorm(h, b_norm_g[0], BF16)]
    hn_kv, u = normed
    rope_t = _rope_tables(jnp.arange(T))
    shared = _shared_kv(hn_kv, B, rope_t, kv_w, cmp_pe, cmp_w1, cmp_w2)
    ncmp = T // CMP_STRIDE
    n_blk = T // SEL_BLOCK
    cpos = jnp.arange(ncmp)[:, None] * CMP_STRIDE + jnp.arange(CMP_BLOCK)[None, :]
    overlap = jax.nn.one_hot(cpos // SEL_BLOCK, n_blk, dtype=F32).mean(axis=1).T.astype(BF16)
    for j in range(n_b):
        last = j == n_b - 1
        h, (u,) = _nsa_layer(h, u, B, shared, rope_t, overlap, b_w_in[j], b_gate_b[j], b_w_o[j],
                             [final_g if last else b_norm_g[j + 1]], last)
    return u.reshape(B, T, D)
```

```python
import functools
import math
import types

import jax
import jax.numpy as jnp
from jax import lax
from jax.experimental import pallas as pl
from jax.experimental.pallas import tpu as pltpu

F32 = jnp.float32
BF16 = jnp.bfloat16

NORM_EPS = 1e-6
GN_EPS = 64e-5
HEAD = 64
CHUNK = 64
UNIT = 256
UNIT_HEADS = UNIT // HEAD
assert CHUNK == HEAD
WKV_CHUNKS = 4
WKV_GROUP = 8
WKV_LAG = 10
EXP_M05 = math.exp(-0.5)

NSA_HEADS = 16
NSA_GROUPS = 4
NSA_REP = NSA_HEADS // NSA_GROUPS
HD = 128
N_BRANCH = 3
CMP_BLOCK = 32
CMP_STRIDE = 16
SEL_BLOCK = 64
SEL_TOP_N = 16
WINDOW = 512
QB = 128
NSA_GROUPS_PER_STEP = 2
LOG2E = math.log2(math.e)
ROPE_DIM = HD // 4
ROPE_THETA = 500000.0
NEG = -1e30
FORCE_BONUS = 1e4
SEL_TILE = 512
WIN_BAND = WINDOW + 128

VMEM_LIMIT = 56 * 1024 * 1024


def _dot(a, b):
    return jnp.dot(a, b, preferred_element_type=F32)


def _dot_nt(a, b):
    return lax.dot_general(a, b, (((1,), (1,)), ((), ())), preferred_element_type=F32)


def _dot_tn(a, b):
    return lax.dot_general(a, b, (((0,), (0,)), ((), ())), preferred_element_type=F32)


def _split3(x):
    h = x.astype(BF16)
    r1 = x - h.astype(F32)
    m = r1.astype(BF16)
    l = (r1 - m.astype(F32)).astype(BF16)
    return h, m, l


def _sigmoid(x):
    return 1.0 / (1.0 + jnp.exp(-x))


def _silu(x):
    return x * _sigmoid(x)


def _mm_body(*refs, has_res):
    if has_res:
        a_ref, w_ref, r_ref, o_ref = refs
    else:
        a_ref, w_ref, o_ref = refs
    acc = _dot(a_ref[...].astype(BF16), w_ref[...].astype(BF16))
    if has_res:
        acc = acc + r_ref[...]
    o_ref[...] = acc.astype(o_ref.dtype)


def matmul(a, w, res=None, out_dtype=F32, tm=None, tn=1024):
    squeeze = a.ndim == 2
    if squeeze:
        a, w = a[None], w[None]
        res = None if res is None else res[None]
    G, K, N = w.shape
    M = a.shape[1]
    if tm is None:
        tm = 2048 if res is None and M % 2048 == 0 else 1024
    tm = min(tm, M)
    tn = min(tn, N)
    assert M % tm == 0 and N % tn == 0, (M, N, tm, tn)
    in_specs = [pl.BlockSpec((None, tm, K), lambda g, i, j: (g, i, 0)),
                pl.BlockSpec((None, K, tn), lambda g, i, j: (g, 0, j))]
    args = [a, w]
    if res is not None:
        in_specs.append(pl.BlockSpec((None, tm, tn), lambda g, i, j: (g, i, j)))
        args.append(res)
    out = pl.pallas_call(
        functools.partial(_mm_body, has_res=res is not None),
        out_shape=jax.ShapeDtypeStruct((G, M, N), out_dtype),
        grid=(G, M // tm, N // tn),
        in_specs=in_specs,
        out_specs=pl.BlockSpec((None, tm, tn), lambda g, i, j: (g, i, j)),
        compiler_params=pltpu.CompilerParams(
            dimension_semantics=("parallel", "parallel", "arbitrary"), vmem_limit_bytes=VMEM_LIMIT),
        name="matmul",
    )(*args)
    return out[0] if squeeze else out


def _out_proj_body(a_ref, w_ref, r_ref, g_ref, *o_refs, keep_h):
    h = _dot(a_ref[...], w_ref[...]) + r_ref[...]
    outs = list(o_refs)
    if keep_h:
        outs.pop(0)[...] = h
    y = h * lax.rsqrt(jnp.mean(h * h, axis=-1, keepdims=True) + NORM_EPS)
    for n, o_ref in enumerate(outs):
        o_ref[...] = (y * g_ref[n:n + 1, :]).astype(o_ref.dtype)


def out_proj(a, w, res, gains, final, tm=512):
    if not gains:
        return matmul(a, w, res=res), []
    M, K = a.shape
    N = w.shape[1]
    tm = min(tm, M)
    row = pl.BlockSpec((tm, N), lambda i: (i, 0))
    dt = F32 if final else BF16
    out_shape = [jax.ShapeDtypeStruct((M, N), dt) for _ in gains]
    if not final:
        out_shape.insert(0, jax.ShapeDtypeStruct((M, N), F32))
    outs = pl.pallas_call(
        functools.partial(_out_proj_body, keep_h=not final),
        out_shape=tuple(out_shape),
        grid=(M // tm,),
        in_specs=[pl.BlockSpec((tm, K), lambda i: (i, 0)), pl.BlockSpec((K, N), lambda i: (0, 0)), row,
                  pl.BlockSpec((len(gains), N), lambda i: (0, 0))],
        out_specs=tuple(row for _ in out_shape),
        compiler_params=pltpu.CompilerParams(dimension_semantics=("parallel",), vmem_limit_bytes=VMEM_LIMIT),
        name="out_proj",
    )(a, w, res, jnp.stack(gains))
    return (None, list(outs)) if final else (outs[0], list(outs[1:]))


def _expand(x, lane_head):
    return jnp.concatenate([jnp.where(lane_head == h, x, 0.0) for h in range(UNIT_HEADS)], axis=0)


def _segsums(xs, ones_bd):
    pieces = []
    for x in xs:
        h = x.astype(BF16)
        pieces += [h, (x - h.astype(F32)).astype(BF16)]
    out = _dot(jnp.concatenate(pieces, axis=0), ones_bd)
    n = xs[0].shape[0]
    return [out[2 * i * n:(2 * i + 1) * n] + out[(2 * i + 1) * n:(2 * i + 2) * n] for i in range(len(xs))]


def _wkv_unit_stages(r, k, v, z, lora, vf, k_k, k_a, r_k, ln_g, ln_b, s_ref, o_ref, consts, turn):
    state_written, chunk_idx = turn
    tri, ones_bd, same, strict, incl, lane_head, eye = consts
    bf = lambda x: x.astype(BF16)
    ex = lambda x: _expand(x, lane_head)

    lw = -EXP_M05 * _sigmoid(lora(0))
    a = _sigmoid(lora(1))
    v = v[...].astype(F32)
    if vf is not None:
        v = v + (vf[...].astype(F32) - v) * _sigmoid(lora(2))
    yield
    k = k[...].astype(F32)
    r = r[...].astype(F32)
    kk = k * k_k[...]
    k2 = k * (1.0 + (a - 1.0) * k_a[...])
    n2, bonus = _segsums([kk * kk, r * k2 * r_k[...]], ones_bd)
    yield
    lh, lm, ll = _split3(lw)
    cum = _dot(tri, lh) + _dot(tri, lm) + _dot(tri, ll)
    yield

    kk = kk / jnp.maximum(jnp.sqrt(n2), 1e-12)
    p_incl = jnp.exp(cum)
    p_inv = jnp.exp(-cum)
    kt = k2 * p_inv
    bt = kk * a * p_inv
    at = -kk * jnp.exp(cum - lw)
    rt = r * p_incl
    eb = bf(ex(bt))
    ek = bf(ex(kt))
    ev = bf(ex(v))
    yield

    top, bot = slice(0, CHUNK), slice(CHUNK, 2 * CHUNK)
    ar = bf(jnp.concatenate([at, rt], axis=0))
    sb = _dot_nt(ar, eb)
    yield
    sk = _dot_nt(ar, ek)
    yield
    a_ab = jnp.where(strict, sb[top], 0.0)
    a_rb = bf(jnp.where(incl, sb[bot], 0.0))
    kv2 = _dot(bf(jnp.concatenate([jnp.where(strict, sk[top], 0.0), jnp.where(incl, sk[bot], 0.0)], axis=0)), ev)
    akv, rkv = kv2[top], kv2[bot]
    yield

    tm = eye + a_ab
    pw = _dot(bf(a_ab), bf(ex(a_ab)))
    yield
    for _ in range(4):
        both = _dot(bf(jnp.concatenate([pw, tm], axis=0)), bf(ex(pw)))
        pw, tm = both[top], tm + both[bot]
        yield
    tm = tm + _dot(bf(tm), bf(ex(pw)))
    yield

    assert len(state_written) == chunk_idx, "stage skew too small: state read before the previous chunk's update"
    s = s_ref[...]
    xs = _dot_nt(ar, bf(s))
    yield
    u = _dot(bf(tm), bf(ex(akv + xs[top])))
    yield
    o = xs[bot] + _dot(a_rb, bf(ex(u))) + rkv
    yield
    upd = _dot_tn(bf(jnp.concatenate([v, u], axis=0)), bf(jnp.concatenate([kt, bt], axis=0)))
    yield
    s_ref[...] = (s + jnp.where(same, upd, 0.0)) * p_incl[CHUNK - 1:CHUNK, :]
    state_written.append(chunk_idx)
    mean = _segsums([o], ones_bd)[0] * (1.0 / HEAD)
    yield

    oc = o - mean
    var = _segsums([oc * oc], ones_bd)[0] * (1.0 / HEAD)
    yield

    o_ref[...] = ((oc * lax.rsqrt(var + GN_EPS) * ln_g[...] + ln_b[...] + bonus * v)
                  * _silu(z[...].astype(F32))).astype(o_ref.dtype)


def _run_skewed(gens, group, lag):
    done = [False] * len(gens)
    rnd = 0
    while not all(done):
        for u, g in enumerate(gens):
            if (u // group) * lag <= rnd and not done[u]:
                try:
                    next(g)
                except StopIteration:
                    done[u] = True
        rnd += 1


def _wkv_body(*refs, has_vres, units):
    if has_vres:
        (r_ref, k_ref, v_ref, z_ref, mid_ref, bw_ref, bias_ref, vf_ref,
         kk_ref, ka_ref, rk_ref, lg_ref, lb_ref, o_ref, s_ref) = refs
    else:
        (r_ref, k_ref, v_ref, z_ref, mid_ref, bw_ref, bias_ref,
         kk_ref, ka_ref, rk_ref, lg_ref, lb_ref, o_ref, s_ref) = refs
        vf_ref = None

    @pl.when(pl.program_id(2) == 0)
    def _():
        s_ref[...] = jnp.zeros_like(s_ref)

    row = lax.broadcasted_iota(jnp.int32, (UNIT, UNIT), 0)
    col = lax.broadcasted_iota(jnp.int32, (UNIT, UNIT), 1)
    same = (row >> 6) == (col >> 6)
    ones_bd = jnp.where(same, 1.0, 0.0).astype(BF16)
    tr = lax.broadcasted_iota(jnp.int32, (CHUNK, CHUNK), 0)
    tc = lax.broadcasted_iota(jnp.int32, (CHUNK, CHUNK), 1)
    tri = jnp.where(tc <= tr, 1.0, 0.0).astype(BF16)
    lane = lax.broadcasted_iota(jnp.int32, (CHUNK, UNIT), 1)
    lane_head = lane >> 6
    t_row = lax.broadcasted_iota(jnp.int32, (CHUNK, UNIT), 0)
    s_lane = lane & (CHUNK - 1)
    strict = s_lane < t_row
    incl = s_lane <= t_row
    eye = jnp.where(s_lane == t_row, 1.0, 0.0).astype(F32)
    consts = (tri, ones_bd, same, strict, incl, lane_head, eye)

    def view(ref, cc, uu):
        if ref is None:
            return None
        rows = pl.ds(cc * CHUNK, CHUNK) if ref.shape[0] > 1 else slice(None)
        return ref.at[rows, pl.ds(uu * UNIT, UNIT)]

    def lora(cc, uu):
        lanes = pl.ds(uu * UNIT, UNIT)
        return lambda l: (_dot(mid_ref[l, pl.ds(cc * CHUNK, CHUNK), :], bw_ref[l, :, lanes]) + bias_ref[l, :, lanes])

    written = [[] for _ in range(units)]
    _run_skewed([
        _wkv_unit_stages(*(view(ref, cc, uu) for ref in (r_ref, k_ref, v_ref, z_ref)), lora(cc, uu),
                         *(view(ref, cc, uu) for ref in (vf_ref, kk_ref, ka_ref, rk_ref, lg_ref, lb_ref)),
                         s_ref.at[uu], view(o_ref, cc, uu), consts, (written[uu], cc))
        for cc in range(WKV_CHUNKS) for uu in range(units)], WKV_GROUP, WKV_LAG)


def wkv(rkvz, mid, b_w, bias, v_first, k_k, k_a, r_k, ln_g, ln_b, batch, units=8):
    _, N, D = rkvz.shape
    L, _, R = mid.shape
    T = N // batch
    step_rows = WKV_CHUNKS * CHUNK
    nc = T // step_rows
    W = units * UNIT
    has_vres = v_first is not None
    assert L == (3 if has_vres else 2)
    row_map = lambda b, u, c: (b * nc + c, u)

    def lead(n):
        return pl.BlockSpec((None, step_rows, W), lambda b, u, c: (n, b * nc + c, u))

    in_specs = [lead(0), lead(1), lead(2), lead(3),
                pl.BlockSpec((L, step_rows, R), lambda b, u, c: (0, b * nc + c, 0)),
                pl.BlockSpec((L, R, W), lambda b, u, c: (0, 0, u)),
                pl.BlockSpec((L, 1, W), lambda b, u, c: (0, 0, u))]
    args = [rkvz, rkvz, rkvz, rkvz, mid, b_w, bias]
    if has_vres:
        in_specs += [lead(2)]
        args += [v_first]
    par = pl.BlockSpec((1, W), lambda b, u, c: (0, u))
    in_specs += [par] * 5
    args += [k_k.reshape(1, D), k_a.reshape(1, D), r_k.reshape(1, D), ln_g.reshape(1, D), ln_b.reshape(1, D)]
    return pl.pallas_call(
        functools.partial(_wkv_body, has_vres=has_vres, units=units),
        out_shape=jax.ShapeDtypeStruct((N, D), BF16),
        grid=(batch, D // W, nc),
        in_specs=in_specs,
        out_specs=pl.BlockSpec((step_rows, W), row_map),
        scratch_shapes=[pltpu.VMEM((units, UNIT, UNIT), F32)],
        compiler_params=pltpu.CompilerParams(
            dimension_semantics=("parallel", "parallel", "arbitrary"), vmem_limit_bytes=VMEM_LIMIT),
        name="wkv7",
    )(*args)


def _attend(s, v_aug):
    m = jnp.max(s, axis=1, keepdims=True)
    acc = _dot(jnp.exp2(s - m).astype(BF16), v_aug)
    return acc[:, :HD] * (1.0 / acc[:, HD:HD + 1])


def _rope(x, c, s1, s2):
    n = x.shape[1] // HD
    if n > 1:
        c, s1, s2 = (jnp.concatenate([t] * n, axis=1) for t in (c, s1, s2))
    half = ROPE_DIM // 2
    return x * c + pltpu.roll(x, x.shape[1] - half, 1) * s1 + pltpu.roll(x, half, 1) * s2


def _tile_rows(t):
    return pl.ds(pl.multiple_of(t * SEL_TILE, SEL_TILE), SEL_TILE)


def _nsa_scores(c, t, dst):
    dst[...] = _dot_nt(c.qa[...], c.ks[_tile_rows(t), :])


def _nsa_absorb(c, t, s):
    m = c.m[...]
    m_new = jnp.maximum(m, jnp.max(s, axis=1, keepdims=True))
    pv = _dot(jnp.exp2(s - m_new).astype(BF16), c.vs[_tile_rows(t), :])
    c.acc[...] = jnp.exp2(m - m_new) * c.acc[...] + pv
    c.m[...] = m_new


def _nsa_pair(c, t):
    _nsa_scores(c, t + 1, c.s1)
    yield
    _nsa_absorb(c, t, c.s0[...])
    yield
    _nsa_scores(c, t + 2, c.s0)
    yield
    _nsa_absorb(c, t + 1, c.s1[...])
    yield


def _nsa_first_of_odd(c):
    _nsa_scores(c, 1, c.s0)
    yield
    _nsa_absorb(c, 0, c.s1[...])
    yield


def _add_rows(s, bias):
    return jnp.concatenate([s[r * QB:(r + 1) * QB] + bias for r in range(NSA_REP)], axis=0)


def _nsa_head(c, i, tq_col, tabs, ov_t, cmp_bias, win_bias):
    rows = NSA_REP * QB
    q4 = _rope(c.q[...], *tabs) * (HD ** -0.5 * LOG2E)
    qs = jnp.concatenate([q4[:, r * HD:(r + 1) * HD] for r in range(NSA_REP)], axis=0)
    qb = qs.astype(BF16)

    band = WIN_BAND
    w0 = pl.multiple_of(jnp.maximum((i + 1) * QB - band, 0), QB)
    s_cmp = _dot_nt(qb, c.kc[...])
    yield
    s_win = _dot_nt(qb, c.kw[pl.ds(w0, band), :])
    yield

    s_cmp = _add_rows(s_cmp, cmp_bias)
    e = jnp.exp2(s_cmp - jnp.max(s_cmp, axis=1, keepdims=True))
    p = e * (jnp.where(tq_col >= CMP_BLOCK - 1, 1.0, 0.0) / jnp.sum(e, axis=1, keepdims=True))
    c.o_cmp = _dot(p.astype(BF16), c.vc[...])
    yield

    psum = p[0:QB]
    for r in range(1, NSA_REP):
        psum = psum + p[r * QB:(r + 1) * QB]
    ph, pm, pl_ = _split3(psum)
    n_blk = ov_t.shape[0]
    imp_t = _dot_nt(ov_t, ph) + _dot_nt(ov_t, pm) + _dot_nt(ov_t, pl_)
    yield

    c.o_win = _attend(_add_rows(s_win, win_bias), c.vw[pl.ds(w0, band), :])
    yield

    jsub = lax.broadcasted_iota(jnp.int32, (n_blk, QB), 0)
    cur = (i * QB + lax.broadcasted_iota(jnp.int32, (n_blk, QB), 1)) >> (SEL_BLOCK.bit_length() - 1)
    forced = (jsub == 0) | (jsub == cur) | (jsub == cur - 1)
    imp_t = jnp.where(forced, FORCE_BONUS, jnp.where(jsub > cur, NEG, imp_t))
    sub8 = lax.broadcasted_iota(jnp.int32, (8, QB), 0)
    grp = [imp_t[8 * a:8 * a + 8] for a in range(n_blk // 8)]
    cnt = [jnp.zeros((8, QB), F32) for _ in grp]
    for jp in range(n_blk):
        rowv = jnp.broadcast_to(imp_t[jp:jp + 1, :], (8, QB))
        for a, x in enumerate(grp):
            ge = lambda: jnp.where(rowv >= x, 1.0, 0.0)
            gt = lambda: jnp.where(rowv > x, 1.0, 0.0)
            if 8 * a > jp:
                inc = ge()
            elif 8 * a + 7 <= jp:
                inc = gt()
            else:
                inc = jnp.where(sub8 + 8 * a > jp, ge(), gt())
            cnt[a] = cnt[a] + inc
    cnt = jnp.concatenate(cnt, axis=0)
    bias_t = jnp.where(cnt < float(min(SEL_TOP_N, n_blk)), 0.0, NEG).astype(BF16)
    qi = lax.broadcasted_iota(jnp.int32, (QB, QB), 0)
    qj = lax.broadcasted_iota(jnp.int32, (QB, QB), 1)
    eye_q = jnp.where(qi == qj, 1.0, 0.0).astype(BF16)
    bias = _dot_nt(eye_q, bias_t)
    yield
    bias4 = jnp.concatenate([bias] * NSA_REP, axis=0).astype(BF16)
    pad = c.ks.shape[1] - HD - n_blk
    parts = [qb, bias4] + ([jnp.zeros((rows, pad), BF16)] if pad else [])
    qa = jnp.concatenate(parts, axis=1)

    c.qa[...] = qa
    s_first = _dot_nt(qa, c.ks[_tile_rows(0), :])
    c.s0[...] = s_first
    c.s1[...] = s_first
    c.m[...] = jnp.full(c.m.shape, NEG, F32)
    c.acc[...] = jnp.zeros(c.acc.shape, F32)
    yield


def _nsa_tail(c, t_last, diag_bias):
    _nsa_absorb(c, t_last, _add_rows(c.s0[...], diag_bias))
    yield
    acc = c.acc[...]
    o_sel = acc[:, :HD] * (1.0 / acc[:, HD:HD + 1])
    g = _sigmoid(c.g[...] + c.gb[...])
    outs = []
    for r in range(NSA_REP):
        rs = slice(r * QB, (r + 1) * QB)
        outs.append(g[:, 3 * r:3 * r + 1] * c.o_cmp[rs] + g[:, 3 * r + 1:3 * r + 2] * o_sel[rs]
                    + g[:, 3 * r + 2:3 * r + 3] * c.o_win[rs])
    o4 = jnp.concatenate(outs, axis=1)
    c.o[...] = (o4 * _silu(c.z[...])).astype(c.o.dtype)


def _nsa_body(q_ref, z_ref, g_ref, gb_ref, rc_ref, rs1_ref, rs2_ref, ovt_ref,
              kc_ref, vc_ref, ks_ref, vs_ref, kw_ref, vw_ref, o_ref, qa_scr, s0_scr, s1_scr, m_scr, acc_scr):
    i = pl.program_id(2)
    rows = NSA_REP * QB
    W = NSA_REP * HD
    n_grp = kc_ref.shape[0]
    grps = []
    for g in range(n_grp):
        lanes = pl.ds(g * W, W)
        glanes = pl.ds(g * 128, 128)
        grps.append(types.SimpleNamespace(
            q=q_ref.at[:, lanes], z=z_ref.at[:, lanes], g=g_ref.at[:, glanes], gb=gb_ref.at[:, glanes],
            kc=kc_ref.at[g], vc=vc_ref.at[g], ks=ks_ref.at[g], vs=vs_ref.at[g], kw=kw_ref.at[g], vw=vw_ref.at[g],
            o=o_ref.at[:, lanes], qa=qa_scr.at[g], s0=s0_scr.at[g], s1=s1_scr.at[g], m=m_scr.at[g], acc=acc_scr.at[g]))
    lock = lambda gens: _run_skewed(gens, n_grp, 0)

    tq_col = i * QB + (lax.broadcasted_iota(jnp.int32, (rows, 1), 0) & (QB - 1))
    tabs = (rc_ref[...], rs1_ref[...], rs2_ref[...])

    t_last = (i * QB) // SEL_TILE
    odd = (t_last & 1) == 1
    tq = i * QB + lax.broadcasted_iota(jnp.int32, (QB, 1), 0)
    ncmp = kc_ref.shape[1]
    cend = lax.broadcasted_iota(jnp.int32, (QB, ncmp), 1) * CMP_STRIDE + (CMP_BLOCK - 1)
    cmp_bias = jnp.where(cend <= tq, 0.0, NEG)
    w0 = jnp.maximum((i + 1) * QB - WIN_BAND, 0)
    diff = (tq - w0) - lax.broadcasted_iota(jnp.int32, (QB, WIN_BAND), 1)
    win_bias = jnp.where(lax.bitcast_convert_type(diff, jnp.uint32) < jnp.uint32(WINDOW), 0.0, NEG)
    kpos = t_last * SEL_TILE + lax.broadcasted_iota(jnp.int32, (QB, SEL_TILE), 1)
    diag_bias = jnp.where(kpos <= tq, 0.0, NEG)

    lock([_nsa_head(c, i, tq_col, tabs, ovt_ref[...], cmp_bias, win_bias) for c in grps])

    @pl.when(odd)
    def _():
        lock([_nsa_first_of_odd(c) for c in grps])

    def pair(t2, carry):
        lock([_nsa_pair(c, 2 * t2 + (t_last & 1)) for c in grps])
        return carry

    lax.fori_loop(0, t_last // 2, pair, 0)
    lock([_nsa_tail(c, t_last, diag_bias) for c in grps])


def nsa_attention(proj, gate_b, rope_q, overlap, shared, batch):
    N = proj.shape[0]
    T = N // batch
    nq = T // QB
    W = NSA_REP * HD
    k_cmp, v_cmp, k_sel, v_sel, k_win, v_win = shared
    rc, rs1, rs2 = rope_q
    ncmp = k_cmp.shape[2]
    n_blk = T // SEL_BLOCK
    P = NSA_GROUPS_PER_STEP
    steps = NSA_GROUPS // P
    rows = NSA_REP * QB
    row_blk = lambda b, g, i: (b * nq + i, g)
    kv = lambda w: pl.BlockSpec((None, P, T, w), lambda b, g, i: (b, g, 0, 0))
    cm = pl.BlockSpec((None, P, ncmp, HD), lambda b, g, i: (b, g, 0, 0))
    rope = pl.BlockSpec((QB, HD), lambda b, g, i: (i, 0))
    gate_blk0 = 2 * NSA_HEADS * HD // (P * 128)
    return pl.pallas_call(
        _nsa_body,
        out_shape=jax.ShapeDtypeStruct((N, NSA_HEADS * HD), BF16),
        grid=(batch, steps, nq),
        in_specs=[pl.BlockSpec((QB, P * W), row_blk),
                  pl.BlockSpec((QB, P * W), lambda b, g, i: (b * nq + i, steps + g)),
                  pl.BlockSpec((QB, P * 128), lambda b, g, i: (b * nq + i, gate_blk0 + g)),
                  pl.BlockSpec((1, P * 128), lambda b, g, i: (0, g)),
                  rope, rope, rope,
                  pl.BlockSpec((n_blk, ncmp), lambda b, g, i: (0, 0)),
                  cm, cm, kv(k_sel.shape[3]), kv(v_sel.shape[3]), kv(k_win.shape[3]), kv(v_win.shape[3])],
        out_specs=pl.BlockSpec((QB, P * W), row_blk),
        scratch_shapes=[pltpu.VMEM((P, rows, k_sel.shape[3]), BF16),
                        pltpu.VMEM((P, rows, SEL_TILE), F32),
                        pltpu.VMEM((P, rows, SEL_TILE), F32),
                        pltpu.VMEM((P, rows, 1), F32),
                        pltpu.VMEM((P, rows, v_sel.shape[3]), F32)],
        compiler_params=pltpu.CompilerParams(
            dimension_semantics=("parallel", "parallel", "arbitrary"), vmem_limit_bytes=VMEM_LIMIT),
        name="nsa_attention",
    )(proj, proj, proj, gate_b, rc, rs1, rs2, overlap, k_cmp, v_cmp, k_sel, v_sel, k_win, v_win)


def _rms(x, g):
    return x * lax.rsqrt(jnp.mean(x * x, axis=-1, keepdims=True) + NORM_EPS) * g


def _norm_body(x_ref, g_ref, o_ref):
    o_ref[...] = _rms(x_ref[...], g_ref[...]).astype(o_ref.dtype)


def rmsnorm(h, g, out_dtype, tm=512):
    N, D = h.shape
    return pl.pallas_call(
        _norm_body,
        out_shape=jax.ShapeDtypeStruct((N, D), out_dtype),
        grid=(N // tm,),
        in_specs=[pl.BlockSpec((tm, D), lambda i: (i, 0)), pl.BlockSpec((1, D), lambda i: (0, 0))],
        out_specs=pl.BlockSpec((tm, D), lambda i: (i, 0)),
        compiler_params=pltpu.CompilerParams(dimension_semantics=("parallel",)),
        name="rmsnorm",
    )(h, g.reshape(1, D))


def _mix_body(x_ref, xp_ref, g_ref, mu_ref, a_ref, o_ref, mid_ref, *, seq, lora_src):
    tm = x_ref.shape[0]
    g = g_ref[...]
    u = _rms(x_ref[...], g)
    last_prev = _rms(xp_ref[...], g)[xp_ref.shape[0] - 1:, :]
    first = lax.rem(pl.program_id(0) * tm, seq) == 0
    row = lax.broadcasted_iota(jnp.int32, (tm, 1), 0)
    prev = jnp.where(row == 0, jnp.where(first, 0.0, last_prev), pltpu.roll(u, 1, 0))
    xx = prev - u
    mix = lambda n: (u + xx * mu_ref[n:n + 1, :]).astype(BF16)
    for n in range(o_ref.shape[0]):
        o_ref[n] = mix(n)
    for l, n in enumerate(lora_src):
        mid = _dot(mix(n), a_ref[l])
        mid_ref[l] = (jnp.tanh(mid) if l == 0 else mid).astype(mid_ref.dtype)


def rwkv_mix(h, g, mu, a_w, lora_src, n_out, seq, tm=256):
    N, D = h.shape
    L, _, R = a_w.shape
    assert seq % tm == 0
    sub = 8
    return pl.pallas_call(
        functools.partial(_mix_body, seq=seq, lora_src=tuple(lora_src)),
        out_shape=(jax.ShapeDtypeStruct((n_out, N, D), BF16), jax.ShapeDtypeStruct((L, N, R), BF16)),
        grid=(N // tm,),
        in_specs=[pl.BlockSpec((tm, D), lambda i: (i, 0)),
                  pl.BlockSpec((sub, D), lambda i: (jnp.maximum(i * (tm // sub) - 1, 0), 0)),
                  pl.BlockSpec((1, D), lambda i: (0, 0)),
                  pl.BlockSpec(mu.shape, lambda i: (0, 0)),
                  pl.BlockSpec((L, D, R), lambda i: (0, 0, 0))],
        out_specs=(pl.BlockSpec((n_out, tm, D), lambda i: (0, i, 0)), pl.BlockSpec((L, tm, R), lambda i: (0, i, 0))),
        compiler_params=pltpu.CompilerParams(dimension_semantics=("parallel",), vmem_limit_bytes=VMEM_LIMIT),
        name="rwkv_mix",
    )(h, h, g.reshape(1, D), mu, a_w)


def _kvpost_body(kv_ref, rc_ref, rs1_ref, rs2_ref, pek_ref, pev_ref, cx_ref, ks_ref, vs_ref, kw_ref, vw_ref):
    tt = kv_ref.shape[0]
    aug = ks_ref.shape[-1] - HD
    tok = pl.program_id(1) * tt + lax.broadcasted_iota(jnp.int32, (tt, aug), 0)
    lane = lax.broadcasted_iota(jnp.int32, (tt, aug), 1)
    onehot = jnp.where(lane == (tok >> (SEL_BLOCK.bit_length() - 1)), 1.0, 0.0).astype(BF16)
    ones_col = jnp.where(lax.broadcasted_iota(jnp.int32, (tt, vs_ref.shape[-1] - HD), 1) == 0, 1.0, 0.0).astype(BF16)
    tabs = (rc_ref[...], rs1_ref[...], rs2_ref[...])
    G = NSA_GROUPS
    n16 = tt // CMP_STRIDE
    for g in range(G):
        for j, pe_ref in ((0, pek_ref), (1, pev_ref)):
            x3 = kv_ref[:, (j * G + g) * HD:(j * G + g + 1) * HD].reshape(n16, CMP_STRIDE, HD)
            x16 = jnp.concatenate([x3[:, l, :] for l in range(CMP_STRIDE)], axis=1)
            cx_ref[j, g] = (x16 + pe_ref[0:1, :]).astype(BF16)
            cx_ref[2 + j, g] = (x16 + pe_ref[1:2, :]).astype(BF16)
        k_s, v_s, k_w, v_w = [kv_ref[:, (j * G + g) * HD:(j * G + g + 1) * HD] for j in range(2, 6)]
        ks_ref[g] = jnp.concatenate([_rope(k_s, *tabs).astype(BF16), onehot], axis=1)
        vs_ref[g] = jnp.concatenate([v_s.astype(BF16), ones_col], axis=1)
        kw_ref[g] = _rope(k_w, *tabs).astype(BF16)
        vw_ref[g] = jnp.concatenate([v_w.astype(BF16), ones_col], axis=1)


def kv_post(kv, rope_t, cmp_pe, batch, tt=512):
    N = kv.shape[0]
    T = N // batch
    G = NSA_GROUPS
    nt = T // tt
    n_blk = T // SEL_BLOCK
    aug = -(-n_blk // 128) * 128
    half = CMP_STRIDE * HD
    pe = cmp_pe.reshape(2, 2, half)
    tab = pl.BlockSpec((tt, HD), lambda b, i: (i, 0))
    pes = pl.BlockSpec((2, half), lambda b, i: (0, 0))
    out4 = lambda w: pl.BlockSpec((None, G, tt, w), lambda b, i: (b, 0, i, 0))
    sds = lambda w: jax.ShapeDtypeStruct((batch, G, T, w), BF16)
    n16 = tt // CMP_STRIDE
    return pl.pallas_call(
        _kvpost_body,
        out_shape=(jax.ShapeDtypeStruct((4, batch, G, T // CMP_STRIDE, half), BF16),
                   sds(HD + aug), sds(2 * HD), sds(HD), sds(2 * HD)),
        grid=(batch, nt),
        in_specs=[pl.BlockSpec((tt, kv.shape[1]), lambda b, i: (b * nt + i, 0)), tab, tab, tab, pes, pes],
        out_specs=(pl.BlockSpec((4, None, G, n16, half), lambda b, i: (0, b, 0, i, 0)),
                   out4(HD + aug), out4(2 * HD), out4(HD), out4(2 * HD)),
        compiler_params=pltpu.CompilerParams(dimension_semantics=("parallel", "parallel"), vmem_limit_bytes=VMEM_LIMIT),
        name="kv_post",
    )(kv, *rope_t, pe[0], pe[1])


def _cmp2_body(lo_ref, hi_ref, w2_ref, rc_ref, rs1_ref, rs2_ref, o_ref):
    n = lo_ref.shape[0]
    hid = lo_ref[...] + pltpu.roll(hi_ref[...], n - 1, 0)
    out = _dot(_silu(hid).astype(BF16), w2_ref[...])
    roped = _rope(out, rc_ref[...], rs1_ref[...], rs2_ref[...])
    o_ref[...] = jnp.where(pl.program_id(0) == 0, roped, out).astype(o_ref.dtype)


def cmp_mlp2(hd, w2, rope_c, groups, ng):
    H = hd.shape[2]
    tab = pl.BlockSpec((ng, HD), lambda j, m: (0, 0))
    return pl.pallas_call(
        _cmp2_body,
        out_shape=jax.ShapeDtypeStruct((2, groups, ng, HD), BF16),
        grid=(2, groups),
        in_specs=[pl.BlockSpec((None, ng, H), lambda j, m: (j, m, 0)),
                  pl.BlockSpec((None, ng, H), lambda j, m: (2 + j, m, 0)),
                  pl.BlockSpec((None, H, HD), lambda j, m: (j, 0, 0)), tab, tab, tab],
        out_specs=pl.BlockSpec((None, None, ng, HD), lambda j, m: (j, m, 0, 0)),
        compiler_params=pltpu.CompilerParams(dimension_semantics=("parallel", "parallel")),
        name="cmp_mlp2",
    )(hd, hd, w2, *rope_c)


def _rope_tables(pos):
    half = ROPE_DIM // 2
    inv = ROPE_THETA ** (-jnp.arange(half, dtype=F32) / half)
    ang = pos.astype(F32)[:, None] * inv[None, :]
    cos, sin = jnp.cos(ang), jnp.sin(ang)
    n = pos.shape[0]
    c = jnp.concatenate([cos, cos, jnp.ones((n, HD - ROPE_DIM), F32)], axis=1)
    s1 = jnp.concatenate([-sin, jnp.zeros((n, HD - half), F32)], axis=1)
    s2 = jnp.concatenate([jnp.zeros((n, half), F32), sin, jnp.zeros((n, HD - ROPE_DIM), F32)], axis=1)
    return c, s1, s2


def _rwkv_layer(h, batch, v_first, norm_g, mu, w_rkvz, w0, w1, w2, a0, a1, a2, vres, k_k, k_a, r_k, ln_g, ln_b, w_o,
                gains, final):
    N, D = h.shape
    loras = [(w1, w2, w0), (a1, a2, a0)]
    idx = [4, 5]
    if vres is not None:
        loras.append((vres[1], vres[2], vres[0]))
        idx.append(2)
    rpad = 128
    a_w = jnp.stack([jnp.pad(l[0], ((0, 0), (0, rpad - l[0].shape[1]))) for l in loras]).astype(BF16)
    b_w = jnp.stack([jnp.pad(l[1], ((0, rpad - l[1].shape[0]), (0, 0))) for l in loras]).astype(BF16)
    bias = jnp.stack([l[2].reshape(1, D) for l in loras])

    mixed, mid = rwkv_mix(h, norm_g, mu, a_w, idx, 4, N // batch)
    rkvz = matmul(mixed, w_rkvz.astype(BF16), out_dtype=BF16)

    o = wkv(rkvz, mid, b_w, bias, v_first, k_k, k_a, r_k.reshape(-1), ln_g, ln_b, batch)
    h, normed = out_proj(o, w_o.astype(BF16), h, gains, final)
    return h, normed, rkvz


def _shared_kv(hn, batch, rope_t, kv_w, cmp_pe, cmp_w1, cmp_w2):
    N, D = hn.shape
    T = N // batch
    G = NSA_GROUPS
    kv = matmul(hn, kv_w.astype(BF16))
    cx, k_sel, v_sel, k_win, v_win = kv_post(kv, rope_t, cmp_pe, batch)

    ng = T // CMP_STRIDE
    half = CMP_STRIDE * HD
    w1 = jnp.concatenate([cmp_w1[:, :half], cmp_w1[:, half:]], axis=0).astype(BF16)
    hd = matmul(cx.reshape(4, batch * G * ng, half), w1)
    rope_c = _rope_tables(jnp.arange(ng) * CMP_STRIDE + CMP_BLOCK - 1)
    cmp = cmp_mlp2(hd, cmp_w2.astype(BF16), rope_c, batch * G, ng).reshape(2, batch, G, ng, HD)
    return cmp[0], cmp[1], k_sel, v_sel, k_win, v_win


def _nsa_layer(h, u, batch, shared, rope_t, overlap, w_in, gate_b, w_o, gains, final):
    N, D = h.shape
    G = NSA_GROUPS
    width = NSA_HEADS * HD
    ng = N_BRANCH * NSA_HEADS
    per = N_BRANCH * NSA_REP
    w_g = jnp.pad(w_in[:, width:width + ng].reshape(D, G, per), ((0, 0), (0, 0), (0, 128 - per))).reshape(D, G * 128)
    w_all = jnp.concatenate([w_in[:, :width], w_in[:, width + ng:], w_g], axis=1).astype(BF16)
    b_g = jnp.pad(gate_b.reshape(G, per), ((0, 0), (0, 128 - per))).reshape(1, G * 128)
    proj = matmul(u, w_all, tn=768)
    o = nsa_attention(proj, b_g, rope_t, overlap, shared, batch)
    return out_proj(o, w_o.astype(BF16), h, gains, final)


def kernel(x, a_norm_g, a_mu, a_w_rkvz, a_w0, a_w1, a_w2, a_a0, a_a1, a_a2, a_v0, a_v1, a_v2, a_k_k, a_k_a, a_r_k,
           a_ln_g, a_ln_b, a_w_o, kv_norm_g, kv_w, cmp_pe, cmp_w1, cmp_w2, b_norm_g, b_w_in, b_gate_b, b_w_o, final_g):
    B, T, D = x.shape
    N = B * T
    n_a = a_norm_g.shape[0]
    n_b = b_norm_g.shape[0]
    h = x.reshape(N, D)
    v_first = None
    normed = None
    for i in range(n_a):
        vres = None if i == 0 else (a_v0[i - 1], a_v1[i - 1], a_v2[i - 1])
        last = i == n_a - 1
        gains = ([kv_norm_g, b_norm_g[0]] if n_b else [final_g]) if last else []
        h, normed, rkvz = _rwkv_layer(h, B, v_first, a_norm_g[i], a_mu[i], a_w_rkvz[i], a_w0[i], a_w1[i], a_w2[i],
                                      a_a0[i], a_a1[i], a_a2[i], vres, a_k_k[i], a_k_a[i], a_r_k[i],
                                      a_ln_g[i], a_ln_b[i], a_w_o[i], gains, last and not n_b)
        if i == 0:
            v_first = rkvz
    if not n_b:
        return (normed[0] if n_a else rmsnorm(h, final_g, F32)).reshape(B, T, D)
    if not n_a:
        normed = [rmsnorm(h, kv_norm_g, BF16), rmsnorm(h, b_norm_g[0], BF16)]
    hn_kv, u = normed
    rope_t = _rope_tables(jnp.arange(T))
    shared = _shared_kv(hn_kv, B, rope_t, kv_w, cmp_pe, cmp_w1, cmp_w2)
    ncmp = T // CMP_STRIDE
    n_blk = T // SEL_BLOCK
    cpos = jnp.arange(ncmp)[:, None] * CMP_STRIDE + jnp.arange(CMP_BLOCK)[None, :]
    overlap = jax.nn.one_hot(cpos // SEL_BLOCK, n_blk, dtype=F32).mean(axis=1).T.astype(BF16)
    for j in range(n_b):
        last = j == n_b - 1
        h, (u,) = _nsa_layer(h, u, B, shared, rope_t, overlap, b_w_in[j], b_gate_b[j], b_w_o[j],
                             [final_g if last else b_norm_g[j + 1]], last)
    return u.reshape(B, T, D)
```
